```python
import math
import jax
import jax.numpy as jnp
from jax import lax
import numpy as np


D_MODEL = 1024
BATCH = 8
SEQ = 2048
DEPTH = 4

HEAD_DIM = 64
RET_HEADS = 4
RET_DIM = RET_HEADS * HEAD_DIM
SSD_HEADS = 8
SSD_HEAD_DIM = 64
SSD_INNER = SSD_HEADS * SSD_HEAD_DIM
SSD_GROUPS = 2
SSD_STATE = 64
SSD_CONV = 4
SSD_CONV_DIM = SSD_INNER + 2 * SSD_GROUPS * SSD_STATE
SB_HEADS = 4
SB_DIM = SB_HEADS * HEAD_DIM
D_MIX = RET_DIM + SSD_INNER + SB_DIM
IN_SPLITS = (RET_DIM, RET_DIM, RET_DIM, RET_DIM, SSD_INNER, SSD_CONV_DIM, SSD_HEADS, SB_DIM, SB_DIM, SB_DIM)
D_IN = 4 * RET_DIM + SSD_INNER + SSD_CONV_DIM + SSD_HEADS + 3 * SB_DIM
CHUNK = 128
Q_BLOCK = 128
D_FF = 3584
N_EXPERTS = 8
TOP_K = 2
ROPE_BASE = 10000.0
EPS = 1e-6
N_DENSE = (DEPTH + 1) // 2
N_MOE = DEPTH // 2

kernel_name = 'hybrid_retention_ssd_stickbreaking_moe'


def rmsnorm(x, w):
    xf = x.astype(jnp.float32)
    y = xf * lax.rsqrt(jnp.mean(xf * xf, axis=-1, keepdims=True) + EPS)
    return (y * w.astype(jnp.float32)).astype(x.dtype)


def rotary(x, pos):
    half = x.shape[-1] // 2
    inv = 1.0 / (ROPE_BASE ** (jnp.arange(half, dtype=jnp.float32) / half))
    ang = pos.astype(jnp.float32)[:, None] * inv[None, :]
    cos = jnp.cos(ang)[None, :, None, :]
    sin = jnp.sin(ang)[None, :, None, :]
    x1, x2 = x[..., :half], x[..., half:]
    return jnp.concatenate([x1 * cos - x2 * sin, x1 * sin + x2 * cos], axis=-1).astype(x.dtype)


def retention_chunkwise(q, k, v):
    bsz, seq, nh, dh = q.shape
    nc = seq // CHUNK
    log_gamma = jnp.log(1.0 - 2.0 ** (-5.0 - jnp.arange(nh, dtype=jnp.float32)))
    idx = jnp.arange(CHUNK, dtype=jnp.float32)
    diff = idx[:, None] - idx[None, :]
    intra = jnp.where(diff >= 0, jnp.exp(log_gamma[:, None, None] * jnp.maximum(diff, 0.0)), 0.0)
    q_decay = jnp.exp(log_gamma[None, :] * (idx[:, None] + 1.0))
    k_decay = jnp.exp(log_gamma[None, :] * (CHUNK - 1.0 - idx[:, None]))
    chunk_decay = jnp.exp(log_gamma * CHUNK)
    qc = q.reshape(bsz, nc, CHUNK, nh, dh)
    kc = (k * dh ** -0.5).reshape(bsz, nc, CHUNK, nh, dh)
    vc = v.reshape(bsz, nc, CHUNK, nh, dh)
    scores = jnp.einsum('bnihd,bnjhd->bnhij', qc, kc) * intra
    y_intra = jnp.einsum('bnhij,bnjhe->bnihe', scores, vc)
    kv = jnp.einsum('bnjhd,bnjhe->bnhde', kc * k_decay[:, :, None], vc)

    def step(state, kv_n):
        return (state * chunk_decay[None, :, None, None] + kv_n).astype(kv.dtype), state

    init = jnp.zeros((bsz, nh, dh, dh), kv.dtype)
    _, prev = lax.scan(step, init, jnp.moveaxis(kv, 1, 0))
    prev = jnp.moveaxis(prev, 0, 1)
    y_inter = jnp.einsum('bnihd,bnhde->bnihe', qc * q_decay[:, :, None], prev)
    return (y_intra + y_inter).reshape(bsz, seq, nh, dh)


def head_groupnorm(y, w):
    yf = y.astype(jnp.float32)
    mu = jnp.mean(yf, axis=-1, keepdims=True)
    var = jnp.mean(jnp.square(yf - mu), axis=-1, keepdims=True)
    yn = (yf - mu) * lax.rsqrt(var + EPS)
    return yn.reshape(y.shape[0], y.shape[1], -1) * w.astype(jnp.float32)


def causal_dwconv(x, w, b):
    k = w.shape[0]
    out = lax.conv_general_dilated(x, w[:, None, :].astype(x.dtype), window_strides=(1,), padding=[(k - 1, 0)],
                                   dimension_numbers=('NWC', 'WIO', 'NWC'), feature_group_count=x.shape[-1])
    return out + b.astype(x.dtype)


def ssd_chunked(xh, dt, a_head, bm, cm):
    bsz, seq, nh, p = xh.shape
    g = bm.shape[2]
    hg = nh // g
    n = bm.shape[3]
    nc = seq // CHUNK
    x = (xh * dt[..., None]).reshape(bsz, nc, CHUNK, g, hg, p)
    a = (dt * a_head).reshape(bsz, nc, CHUNK, g, hg)
    a_cs = jnp.cumsum(a, axis=2)
    bc = bm.reshape(bsz, nc, CHUNK, g, n)
    cc = cm.reshape(bsz, nc, CHUNK, g, n)
    seg = a_cs[:, :, :, None] - a_cs[:, :, None, :]
    lower = (jnp.arange(CHUNK)[:, None] >= jnp.arange(CHUNK)[None, :])[None, None, :, :, None, None]
    l_mat = jnp.exp(jnp.where(lower, seg, -jnp.inf))
    cb = jnp.einsum('bclgn,bcsgn->bclsg', cc, bc)
    y_diag = jnp.einsum('bclsgh,bcsghp->bclghp', cb[..., None] * l_mat, x)
    decay_states = jnp.exp(a_cs[:, :, -1:] - a_cs)
    states = jnp.einsum('bclgn,bclghp->bcghpn', bc, x * decay_states[..., None])
    chunk_decay = jnp.exp(a_cs[:, :, -1])

    def step(state, inp):
        st, dec = inp
        return (state * dec[..., None, None] + st).astype(states.dtype), state

    init = jnp.zeros((bsz, g, hg, p, n), states.dtype)
    _, prev = lax.scan(step, init, (jnp.moveaxis(states, 1, 0), jnp.moveaxis(chunk_decay, 1, 0)))
    prev = jnp.moveaxis(prev, 0, 1)
    y_off = jnp.einsum('bclgn,bcghpn->bclghp', cc, prev) * jnp.exp(a_cs)[..., None]
    return (y_diag + y_off).reshape(bsz, seq, nh, p)


def gated_rmsnorm(y, z, w):
    gy = y.astype(jnp.float32) * jax.nn.silu(z.astype(jnp.float32))
    gg = gy.reshape(gy.shape[:-1] + (SSD_GROUPS, -1))
    gg = gg * lax.rsqrt(jnp.mean(gg * gg, axis=-1, keepdims=True) + EPS)
    return gg.reshape(gy.shape) * w.astype(jnp.float32)


def stick_breaking(q, k, v):
    seq = q.shape[1]
    scale = q.shape[-1] ** -0.5
    outs = []
    for blk in range(seq // Q_BLOCK):
        q0 = blk * Q_BLOCK
        end = q0 + Q_BLOCK
        z = jnp.einsum('bthd,bshd->bhts', q[:, q0:end], k[:, :end]).astype(jnp.float32) * scale
        t_idx = q0 + jnp.arange(Q_BLOCK)
        mask = jnp.arange(end)[None, :] < t_idx[:, None]
        sp = jnp.where(mask, jax.nn.softplus(z), 0.0)
        log_w = z - lax.cumsum(sp, axis=3, reverse=True)
        w = jnp.where(mask, jnp.exp(log_w), 0.0)
        outs.append(jnp.einsum('bhts,bshd->bthd', w.astype(v.dtype), v[:, :end]))
    return jnp.concatenate(outs, axis=1)


def hybrid_mixer(h, w_in, ret_norm_w, conv_w, conv_b, dt_bias, a_log, d_skip, ssd_norm_w, w_out):
    bsz, seq, _ = h.shape
    pos = jnp.arange(seq)
    proj = jnp.einsum('bsd,de->bse', h, w_in)
    offs = np.cumsum(IN_SPLITS)[:-1].tolist()
    rq, rk, rv, rg, sz, sxbc, sdt, bq, bk, bv = jnp.split(proj, offs, axis=-1)
    rq = rotary(rq.reshape(bsz, seq, RET_HEADS, HEAD_DIM), pos)
    rk = rotary(rk.reshape(bsz, seq, RET_HEADS, HEAD_DIM), pos)
    ro = retention_chunkwise(rq, rk, rv.reshape(bsz, seq, RET_HEADS, HEAD_DIM))
    ro = head_groupnorm(ro, ret_norm_w) * jax.nn.silu(rg.astype(jnp.float32))
    xbc = jax.nn.silu(causal_dwconv(sxbc, conv_w, conv_b))
    sx, sb, sc = jnp.split(xbc, [SSD_INNER, SSD_INNER + SSD_GROUPS * SSD_STATE], axis=-1)
    dt = jax.nn.softplus(sdt.astype(jnp.float32) + dt_bias.astype(jnp.float32))
    a_head = -jnp.exp(a_log.astype(jnp.float32))
    xh = sx.reshape(bsz, seq, SSD_HEADS, SSD_HEAD_DIM)
    y = ssd_chunked(xh, dt, a_head, sb.reshape(bsz, seq, SSD_GROUPS, SSD_STATE), sc.reshape(bsz, seq, SSD_GROUPS, SSD_STATE))
    y = y + d_skip[:, None] * xh
    so = gated_rmsnorm(y.reshape(bsz, seq, SSD_INNER), sz, ssd_norm_w)
    bo = stick_breaking(bq.reshape(bsz, seq, SB_HEADS, HEAD_DIM), bk.reshape(bsz, seq, SB_HEADS, HEAD_DIM),
                        bv.reshape(bsz, seq, SB_HEADS, HEAD_DIM)).reshape(bsz, seq, SB_DIM)
    mix = jnp.concatenate([ro.astype(h.dtype), so.astype(h.dtype), bo.astype(h.dtype)], axis=-1)
    return jnp.einsum('bse,ed->bsd', mix, w_out)


def swiglu(h, wg, wu, wd):
    a = jnp.einsum('bsd,df->bsf', h, wg)
    b = jnp.einsum('bsd,df->bsf', h, wu)
    return jnp.einsum('bsf,fd->bsd', jax.nn.silu(a) * b, wd)


def moe_swiglu(h, router_w, wg, wu, wd):
    logits = jnp.einsum('bsd,de->bse', h, router_w).astype(jnp.float32)
    top_val, top_idx = lax.top_k(logits, TOP_K)
    gates = jax.nn.softmax(top_val, axis=-1)
    combine = jnp.sum(jax.nn.one_hot(top_idx, N_EXPERTS, dtype=jnp.float32) * gates[..., None], axis=-2)
    out = jnp.zeros_like(h)
    for e in range(N_EXPERTS):
        out = out + combine[..., e:e + 1].astype(h.dtype) * swiglu(h, wg[e], wu[e], wd[e])
    return out


def setup_inputs(seed: int = 0) -> dict:
    key = jax.random.key(seed)
    ks = jax.random.split(key, 20)
    f32 = jnp.float32
    nrm = lambda k, shape, fan: jax.random.normal(k, shape, f32) * (fan ** -0.5)
    gain = lambda k, shape: 1.0 + 0.02 * jax.random.normal(k, shape, f32)
    dt = jnp.exp(jax.random.uniform(ks[6], (DEPTH, SSD_HEADS), f32) * (math.log(0.1) - math.log(0.001)) + math.log(0.001))
    return {
        'x': jax.random.normal(ks[0], (BATCH, SEQ, D_MODEL), f32),
        'norm1_w': gain(ks[1], (DEPTH, D_MODEL)),
        'w_in': nrm(ks[2], (DEPTH, D_MODEL, D_IN), D_MODEL),
        'ret_norm_w': gain(ks[3], (DEPTH, RET_DIM)),
        'ssd_conv_w': nrm(ks[4], (DEPTH, SSD_CONV, SSD_CONV_DIM), SSD_CONV),
        'ssd_conv_b': 0.01 * jax.random.normal(ks[5], (DEPTH, SSD_CONV_DIM), f32),
        'ssd_dt_bias': dt + jnp.log(-jnp.expm1(-dt)),
        'ssd_a_log': jnp.log(jax.random.uniform(ks[7], (DEPTH, SSD_HEADS), f32, 1.0, 16.0)),
        'ssd_d': gain(ks[8], (DEPTH, SSD_HEADS)),
        'ssd_norm_w': gain(ks[9], (DEPTH, SSD_INNER)),
        'w_out': nrm(ks[10], (DEPTH, D_MIX, D_MODEL), D_MIX),
        'norm2_w': gain(ks[11], (DEPTH, D_MODEL)),
        'ffn_w_gate': nrm(ks[12], (N_DENSE, D_MODEL, D_FF), D_MODEL),
        'ffn_w_up': nrm(ks[13], (N_DENSE, D_MODEL, D_FF), D_MODEL),
        'ffn_w_down': nrm(ks[14], (N_DENSE, D_FF, D_MODEL), D_FF),
        'moe_router': nrm(ks[15], (N_MOE, D_MODEL, N_EXPERTS), D_MODEL),
        'moe_w_gate': nrm(ks[16], (N_MOE, N_EXPERTS, D_MODEL, D_FF), D_MODEL),
        'moe_w_up': nrm(ks[17], (N_MOE, N_EXPERTS, D_MODEL, D_FF), D_MODEL),
        'moe_w_down': nrm(ks[18], (N_MOE, N_EXPERTS, D_FF, D_MODEL), D_FF),
        'final_norm_w': gain(ks[19], (D_MODEL,)),
    }


def reference(x, norm1_w, w_in, ret_norm_w, ssd_conv_w, ssd_conv_b, ssd_dt_bias, ssd_a_log, ssd_d, ssd_norm_w,
              w_out, norm2_w, ffn_w_gate, ffn_w_up, ffn_w_down, moe_router, moe_w_gate, moe_w_up, moe_w_down,
              final_norm_w):
    for layer in range(DEPTH):
        h = rmsnorm(x, norm1_w[layer])
        x = x + hybrid_mixer(h, w_in[layer], ret_norm_w[layer], ssd_conv_w[layer], ssd_conv_b[layer],
                             ssd_dt_bias[layer], ssd_a_log[layer], ssd_d[layer], ssd_norm_w[layer],
                             w_out[layer]).astype(x.dtype)
        h = rmsnorm(x, norm2_w[layer])
        j = layer // 2
        if layer % 2 == 0:
            f = swiglu(h, ffn_w_gate[j], ffn_w_up[j], ffn_w_down[j])
        else:
            f = moe_swiglu(h, moe_router[j], moe_w_gate[j], moe_w_up[j], moe_w_down[j])
        x = x + f.astype(x.dtype)
    return rmsnorm(x, final_norm_w)
```

```python
import functools
import math

import numpy as np
import jax
import jax.numpy as jnp
from jax import lax
from jax.experimental import pallas as pl
from jax.experimental.pallas import tpu as pltpu

F32 = jnp.float32
BF16 = jnp.bfloat16

D_MODEL = 1024
HEAD_DIM = 64
RET_HEADS = 4
RET_DIM = RET_HEADS * HEAD_DIM
SSD_HEADS = 8
SSD_INNER = SSD_HEADS * HEAD_DIM
SSD_GROUPS = 2
SSD_STATE = 64
SSD_CONV = 4
SSD_BC = SSD_GROUPS * SSD_STATE
SSD_CONV_DIM = SSD_INNER + 2 * SSD_BC
SB_HEADS = 4
SB_DIM = SB_HEADS * HEAD_DIM
D_MIX = RET_DIM + SSD_INNER + SB_DIM
CHUNK = 128
D_FF = 3584
N_EXPERTS = 8
ROPE_BASE = 10000.0
EPS = 1e-6

LANES = 128
D_PROJ = 3072
OFF_RQ, OFF_RK, OFF_RV, OFF_RG = 0, 256, 512, 768
OFF_SZ, OFF_SXBC = 1024, 1536
OFF_BQ, OFF_BK, OFF_BV = 2304, 2560, 2816

VMEM_LIMIT = 56 * 1024 * 1024


def _cparams(sem):
    return pltpu.CompilerParams(dimension_semantics=sem, vmem_limit_bytes=VMEM_LIMIT)


def _dot(a, b):
    return jnp.dot(a, b, preferred_element_type=F32)


def _dot_nt(a, b):
    return lax.dot_general(a, b, (((1,), (1,)), ((), ())), preferred_element_type=F32)


def _split2(x):
    hi = x.astype(BF16)
    lo = (x - hi.astype(F32)).astype(BF16)
    return hi, lo


def _split3(x):
    hi = x.astype(BF16)
    r = x - hi.astype(F32)
    mid = r.astype(BF16)
    lo = (r - mid.astype(F32)).astype(BF16)
    return hi, mid, lo


def _dot_x_const(x, c):
    hi, lo = _split2(x)
    return _dot(hi, c) + _dot(lo, c)


def _silu(x):
    return x * jax.nn.sigmoid(x)


def _softplus(x):
    return jnp.maximum(x, 0.0) + jnp.log1p(jnp.exp(-jnp.abs(x)))


def _rms_h(x, w):
    ms = jnp.mean(x * x, axis=-1, keepdims=True)
    return x * lax.rsqrt(ms + EPS) * w


def _inproj_kernel(x_ref, nw_ref, w_ref, wdt_ref, o_ref, dt_ref):
    h = _rms_h(x_ref[...], nw_ref[...]).astype(BF16)
    step = 512
    for c in range(D_PROJ // step):
        o_ref[:, c * step:(c + 1) * step] = _dot(h, w_ref[:, c * step:(c + 1) * step]).astype(BF16)
    dt_ref[...] = _dot(h, wdt_ref[...])


def _inproj(x, nw, w, wdt, tm=512):
    t = x.shape[0]
    return pl.pallas_call(
        _inproj_kernel,
        grid=(t // tm,),
        in_specs=[
            pl.BlockSpec((tm, D_MODEL), lambda m: (m, 0)),
            pl.BlockSpec((1, D_MODEL), lambda m: (0, 0)),
            pl.BlockSpec((D_MODEL, D_PROJ), lambda m: (0, 0)),
            pl.BlockSpec((D_MODEL, LANES), lambda m: (0, 0)),
        ],
        out_specs=[
            pl.BlockSpec((tm, D_PROJ), lambda m: (m, 0)),
            pl.BlockSpec((tm, LANES), lambda m: (m, 0)),
        ],
        out_shape=[jax.ShapeDtypeStruct((t, D_PROJ), BF16), jax.ShapeDtypeStruct((t, LANES), F32)],
        compiler_params=_cparams(("parallel",)),
        name="inproj",
    )(x, nw, w, wdt)


def _retention_kernel(q_ref, k_ref, v_ref, g_ref, cos_ref, sin_ref, intra_ref, qdec_ref, kdec_ref,
                      cd_ref, avg_ref, nw_ref, o_ref, state_ref):
    c = pl.program_id(1)

    @pl.when(c == 0)
    def _():
        state_ref[...] = jnp.zeros_like(state_ref)

    cos = cos_ref[...]
    sin = sin_ref[...]

    def rot(t):
        t1, t2 = t[:, :LANES], t[:, LANES:]
        return jnp.concatenate([t1 * cos - t2 * sin, t1 * sin + t2 * cos], axis=-1)

    rq = rot(q_ref[...].astype(F32))
    rk = rot(k_ref[...].astype(F32))
    vb = v_ref[...]

    lane = lax.broadcasted_iota(jnp.int32, (1, RET_DIM), 1)
    head_qk = (lane % LANES) // (HEAD_DIM // 2)
    head_v = lane // HEAD_DIM

    q_stack = jnp.concatenate([jnp.where(head_qk == h, rq, 0.0) for h in range(RET_HEADS)], axis=0)
    scores = _dot_nt(q_stack.astype(BF16), rk.astype(BF16)) * intra_ref[...]
    y_stack = _dot(scores.astype(BF16), vb)
    y = jnp.where(head_v == 0, y_stack[:CHUNK], 0.0)
    for h in range(1, RET_HEADS):
        y = y + jnp.where(head_v == h, y_stack[h * CHUNK:(h + 1) * CHUNK], 0.0)

    state = state_ref[...]
    y = y + _dot((rq * qdec_ref[...]).astype(BF16), state.astype(BF16))
    kd = (rk * kdec_ref[...]).astype(BF16)
    kv = lax.dot_general(kd, vb, (((0,), (0,)), ((), ())), preferred_element_type=F32)
    cd = cd_ref[...]
    state_ref[...] = state * cd + jnp.where(cd != 0.0, kv, 0.0)

    avg = avg_ref[...]
    mu = _dot_x_const(y, avg)
    d = y - mu
    var = _dot_x_const(d * d, avg)
    yn = d * lax.rsqrt(var + EPS)
    o_ref[...] = (yn * nw_ref[...] * _silu(g_ref[...].astype(F32))).astype(BF16)


def _retention_tables(seq):
    f = np.float32
    half = HEAD_DIM // 2
    inv = (1.0 / (f(ROPE_BASE) ** (np.arange(half, dtype=f) / f(half)))).astype(f)
    ang = np.arange(seq, dtype=f)[:, None] * inv[None, :]
    cos = np.tile(np.cos(ang).astype(f), (1, RET_HEADS))
    sin = np.tile(np.sin(ang).astype(f), (1, RET_HEADS))
    log_gamma = np.log(f(1.0) - f(2.0) ** (f(-5.0) - np.arange(RET_HEADS, dtype=f))).astype(f)
    idx = np.arange(CHUNK, dtype=f)
    diff = idx[:, None] - idx[None, :]
    scale = f(HEAD_DIM ** -0.5)
    intra = np.where(diff >= 0, np.exp(log_gamma[:, None, None] * np.maximum(diff, 0.0)), 0.0).astype(f)
    intra = (intra * scale).reshape(RET_HEADS * CHUNK, CHUNK)
    head_qk = (np.arange(RET_DIM) % LANES) // half
    head_v = np.arange(RET_DIM) // HEAD_DIM
    qdec = np.exp(log_gamma[head_qk][None, :] * (idx[:, None] + 1.0)).astype(f)
    kdec = (np.exp(log_gamma[head_qk][None, :] * (CHUNK - 1.0 - idx[:, None])) * scale).astype(f)
    chunk_decay = np.exp(log_gamma * f(CHUNK)).astype(f)
    same = head_qk[:, None] == head_v[None, :]
    cd = np.where(same, chunk_decay[head_qk][:, None], 0.0).astype(f)
    avg = np.where(head_v[:, None] == head_v[None, :], 1.0 / HEAD_DIM, 0.0).astype(f)
    return (jnp.asarray(cos), jnp.asarray(sin), jnp.asarray(intra), jnp.asarray(qdec), jnp.asarray(kdec),
            jnp.asarray(cd), jnp.asarray(avg, dtype=BF16))


def _retention(proj, tables, nw, bsz, seq):
    nc = seq // CHUNK
    cos, sin, intra, qdec, kdec, cd, avg = tables
    w = RET_DIM

    def col(j):
        return pl.BlockSpec((CHUNK, w), lambda b, c: (b * nc + c, j))

    def const(shape):
        return pl.BlockSpec(shape, lambda b, c: (0, 0))

    return pl.pallas_call(
        _retention_kernel,
        grid=(bsz, nc),
        in_specs=[
            col(OFF_RQ // w), col(OFF_RK // w), col(OFF_RV // w), col(OFF_RG // w),
            pl.BlockSpec((CHUNK, LANES), lambda b, c: (c, 0)),
            pl.BlockSpec((CHUNK, LANES), lambda b, c: (c, 0)),
            const((RET_HEADS * CHUNK, CHUNK)), const((CHUNK, w)), const((CHUNK, w)),
            const((w, w)), const((w, w)), const((1, w)),
        ],
        out_specs=pl.BlockSpec((CHUNK, w), lambda b, c: (b * nc + c, 0)),
        out_shape=jax.ShapeDtypeStruct((bsz * seq, w), BF16),
        scratch_shapes=[pltpu.VMEM((w, w), F32)],
        compiler_params=_cparams(("parallel", "arbitrary")),
        name="retention",
    )(proj, proj, proj, proj, cos, sin, intra, qdec, kdec, cd, avg, nw)


def _ssd_kernel(z_ref, xbc_ref, dt_ref, cw_ref, cb_ref, dtb_ref, alog_ref, dskip_ref, nw_ref,
                tri_ref, exp_ref, o_ref, prev_ref, state_ref):
    c = pl.program_id(1)

    @pl.when(c == 0)
    def _():
        prev_ref[...] = jnp.zeros_like(prev_ref)
        state_ref[...] = jnp.zeros_like(state_ref)

    cur = xbc_ref[...].astype(F32)
    prv = prev_ref[...]
    row = lax.broadcasted_iota(jnp.int32, (CHUNK, 1), 0)
    acc = cur * cw_ref[SSD_CONV - 1:SSD_CONV, :] + cb_ref[...]
    for j in range(1, SSD_CONV):
        shifted = jnp.where(row >= j, pltpu.roll(cur, j, 0), pltpu.roll(prv, j, 0))
        acc = acc + shifted * cw_ref[SSD_CONV - 1 - j:SSD_CONV - j, :]
    prev_ref[...] = cur
    xa = _silu(acc)
    x = xa[:, :SSD_INNER]
    bm = xa[:, SSD_INNER:SSD_INNER + SSD_BC]
    cm = xa[:, SSD_INNER + SSD_BC:]

    dt = _softplus(dt_ref[...] + dtb_ref[...])
    a = dt * (-jnp.exp(alog_ref[...]))
    tri = tri_ref[...]
    a_hi, a_mid, a_lo = _split3(a)
    a_cs = _dot(tri, a_hi) + _dot(tri, a_mid) + _dot(tri, a_lo)
    a_cs_t = a_cs.T
    stack = jnp.concatenate([dt, jnp.exp(a_cs), jnp.exp(a_cs[CHUNK - 1:CHUNK, :] - a_cs)], axis=0)
    stack_x = _dot_x_const(stack, exp_ref[...])
    dt_x = stack_x[:CHUNK]
    ea_x = stack_x[CHUNK:2 * CHUNK]
    dec_x = stack_x[2 * CHUNK:]

    xdt = x * dt_x
    xdt_b = xdt.astype(BF16)
    bm_b = bm.astype(BF16)
    cm_b = cm.astype(BF16)
    lane_bc = lax.broadcasted_iota(jnp.int32, (1, SSD_BC), 1)
    cb = [_dot_nt(jnp.where(lane_bc // SSD_STATE == g, cm, 0.0).astype(BF16), bm_b) for g in range(SSD_GROUPS)]

    li = lax.broadcasted_iota(jnp.int32, (CHUNK, CHUNK), 0)
    si = lax.broadcasted_iota(jnp.int32, (CHUNK, CHUNK), 1)
    lower = li >= si
    heads_per_group = SSD_HEADS // SSD_GROUPS
    m_list = []
    for h in range(SSD_HEADS):
        seg = a_cs[:, h:h + 1] - a_cs_t[h:h + 1, :]
        l_mat = jnp.exp(jnp.where(lower, seg, -jnp.inf))
        m_list.append((cb[h // heads_per_group] * l_mat).astype(BF16))
    y_stack = _dot(jnp.concatenate(m_list, axis=0), xdt_b)
    lane_x = lax.broadcasted_iota(jnp.int32, (1, SSD_INNER), 1)
    head_x = lane_x // HEAD_DIM
    y = jnp.where(head_x == 0, y_stack[:CHUNK], 0.0)
    for h in range(1, SSD_HEADS):
        y = y + jnp.where(head_x == h, y_stack[h * CHUNK:(h + 1) * CHUNK], 0.0)

    state = state_ref[...]
    y = y + _dot(cm_b, state.astype(BF16)) * ea_x
    contrib = _dot(bm.T.astype(BF16), (xdt * dec_x).astype(BF16))
    row_g = lax.broadcasted_iota(jnp.int32, (SSD_BC, 1), 0) // SSD_STATE
    same_group = row_g == lane_x // (SSD_INNER // SSD_GROUPS)
    state_ref[...] = state * ea_x[CHUNK - 1:CHUNK, :] + jnp.where(same_group, contrib, 0.0)

    y = y + dskip_ref[...] * x
    gy = y * _silu(z_ref[...].astype(F32))
    gw = SSD_INNER // SSD_GROUPS
    outs = []
    for g in range(SSD_GROUPS):
        part = gy[:, g * gw:(g + 1) * gw]
        outs.append(part * lax.rsqrt(jnp.mean(part * part, axis=-1, keepdims=True) + EPS))
    o_ref[...] = (jnp.concatenate(outs, axis=-1) * nw_ref[...]).astype(BF16)


def _ssd_tables():
    idx = np.arange(CHUNK)
    tri = (idx[:, None] >= idx[None, :]).astype(np.float32)
    expand = np.zeros((LANES, SSD_INNER), np.float32)
    for h in range(SSD_HEADS):
        expand[h, h * HEAD_DIM:(h + 1) * HEAD_DIM] = 1.0
    return jnp.asarray(tri, dtype=BF16), jnp.asarray(expand, dtype=BF16)


def _ssd(proj, dt_raw, conv_w, conv_b, dt_bias, a_log, d_skip, nw, tables, bsz, seq):
    nc = seq // CHUNK
    tri, expand = tables

    def const(shape):
        return pl.BlockSpec(shape, lambda b, c: (0, 0))

    return pl.pallas_call(
        _ssd_kernel,
        grid=(bsz, nc),
        in_specs=[
            pl.BlockSpec((CHUNK, SSD_INNER), lambda b, c: (b * nc + c, OFF_SZ // SSD_INNER)),
            pl.BlockSpec((CHUNK, SSD_CONV_DIM), lambda b, c: (b * nc + c, OFF_SXBC // SSD_CONV_DIM)),
            pl.BlockSpec((CHUNK, LANES), lambda b, c: (b * nc + c, 0)),
            const((SSD_CONV, SSD_CONV_DIM)), const((1, SSD_CONV_DIM)), const((1, LANES)), const((1, LANES)),
            const((1, SSD_INNER)), const((1, SSD_INNER)), const((CHUNK, CHUNK)), const((LANES, SSD_INNER)),
        ],
        out_specs=pl.BlockSpec((CHUNK, SSD_INNER), lambda b, c: (b * nc + c, 0)),
        out_shape=jax.ShapeDtypeStruct((bsz * seq, SSD_INNER), BF16),
        scratch_shapes=[pltpu.VMEM((CHUNK, SSD_CONV_DIM), F32), pltpu.VMEM((SSD_BC, SSD_INNER), F32)],
        compiler_params=_cparams(("parallel", "arbitrary")),
        name="ssd",
    )(proj, proj, dt_raw, conv_w, conv_b, dt_bias, a_log, d_skip, nw, tri, expand)


def _stickbreak_kernel(q_ref, k_ref, v_ref, u_ref, o_ref, acc_ref):
    i = pl.program_id(1)
    blk = CHUNK
    q = q_ref[...]
    lane = lax.broadcasted_iota(jnp.int32, (1, SB_DIM), 1)
    head = lane // HEAD_DIM
    q_stack = jnp.concatenate([jnp.where(head == h, q, jnp.zeros_like(q)) for h in range(SB_HEADS)], axis=0)
    rows = SB_HEADS * blk
    t_idx = i * blk + lax.broadcasted_iota(jnp.int32, (rows, 1), 0) % blk
    s_lane = lax.broadcasted_iota(jnp.int32, (1, blk), 1)
    scale = HEAD_DIM ** -0.5
    acc_ref[...] = jnp.zeros_like(acc_ref)
    u = u_ref[...]

    def body(jj, carry):
        j = i - jj
        start = pl.multiple_of(j * blk, blk)
        kb = k_ref[pl.ds(start, blk), :]
        vb = v_ref[pl.ds(start, blk), :]
        z = _dot_nt(q_stack, kb) * scale
        mask = (j * blk + s_lane) < t_idx
        sp = jnp.where(mask, _softplus(z), 0.0)
        sp_hi, sp_lo = _split2(sp)
        cst = _dot(sp_hi, u) + _dot(sp_lo, u)
        log_w = z - (cst[:, :blk] + carry)
        w = jnp.where(mask, jnp.exp(log_w), 0.0)
        acc_ref[...] += _dot(w.astype(BF16), vb)
        return carry + cst[:, blk:]

    lax.fori_loop(0, i + 1, body, jnp.zeros((rows, blk), F32))
    acc = acc_ref[...]
    out = jnp.where(head == 0, acc[:blk], 0.0)
    for h in range(1, SB_HEADS):
        out = out + jnp.where(head == h, acc[h * blk:(h + 1) * blk], 0.0)
    o_ref[...] = out.astype(BF16)


def _stickbreak_tables():
    idx = np.arange(CHUNK)
    rev = (idx[:, None] >= idx[None, :]).astype(np.float32)
    u = np.concatenate([rev, np.ones((CHUNK, CHUNK), np.float32)], axis=1)
    return jnp.asarray(u, dtype=BF16)


def _stickbreak(proj, u, bsz, seq):
    nq = seq // CHUNK
    w = SB_DIM
    return pl.pallas_call(
        _stickbreak_kernel,
        grid=(bsz, nq),
        in_specs=[
            pl.BlockSpec((CHUNK, w), lambda b, i: (b * nq + i, OFF_BQ // w)),
            pl.BlockSpec((seq, w), lambda b, i: (b, OFF_BK // w)),
            pl.BlockSpec((seq, w), lambda b, i: (b, OFF_BV // w)),
            pl.BlockSpec((CHUNK, 2 * CHUNK), lambda b, i: (0, 0)),
        ],
        out_specs=pl.BlockSpec((CHUNK, w), lambda b, i: (b * nq + i, 0)),
        out_shape=jax.ShapeDtypeStruct((bsz * seq, w), BF16),
        scratch_shapes=[pltpu.VMEM((SB_HEADS * CHUNK, w), F32)],
        compiler_params=_cparams(("parallel", "arbitrary")),
        name="stickbreak",
    )(proj, proj, proj, u)


def _outproj_kernel(x_ref, ro_ref, so_ref, bo_ref, w_ref, o_ref):
    acc = _dot(ro_ref[...], w_ref[:RET_DIM, :])
    acc = acc + _dot(so_ref[...], w_ref[RET_DIM:RET_DIM + SSD_INNER, :])
    acc = acc + _dot(bo_ref[...], w_ref[RET_DIM + SSD_INNER:, :])
    o_ref[...] = x_ref[...] + acc


def _outproj(x, ro, so, bo, w, tm=512):
    t = x.shape[0]
    return pl.pallas_call(
        _outproj_kernel,
        grid=(t // tm,),
        in_specs=[
            pl.BlockSpec((tm, D_MODEL), lambda m: (m, 0)),
            pl.BlockSpec((tm, RET_DIM), lambda m: (m, 0)),
            pl.BlockSpec((tm, SSD_INNER), lambda m: (m, 0)),
            pl.BlockSpec((tm, SB_DIM), lambda m: (m, 0)),
            pl.BlockSpec((D_MIX, D_MODEL), lambda m: (0, 0)),
        ],
        out_specs=pl.BlockSpec((tm, D_MODEL), lambda m: (m, 0)),
        out_shape=jax.ShapeDtypeStruct((t, D_MODEL), F32),
        compiler_params=_cparams(("parallel",)),
        name="outproj",
    )(x, ro, so, bo, w)


def _ffn_kernel(x_ref, nw_ref, wg_ref, wu_ref, wd_ref, o_ref, h_ref, acc_ref):
    f = pl.program_id(1)

    @pl.when(f == 0)
    def _():
        h_ref[...] = _rms_h(x_ref[...], nw_ref[...]).astype(BF16)
        acc_ref[...] = jnp.zeros_like(acc_ref)

    h = h_ref[...]
    act = _silu(_dot(h, wg_ref[...])) * _dot(h, wu_ref[...])
    acc_ref[...] += _dot(act.astype(BF16), wd_ref[...])

    @pl.when(f == pl.num_programs(1) - 1)
    def _():
        o_ref[...] = x_ref[...] + acc_ref[...]


def _ffn(x, nw, wg, wu, wd, tm=1024, tf=512):
    t = x.shape[0]
    return pl.pallas_call(
        _ffn_kernel,
        grid=(t // tm, D_FF // tf),
        in_specs=[
            pl.BlockSpec((tm, D_MODEL), lambda m, f: (m, 0)),
            pl.BlockSpec((1, D_MODEL), lambda m, f: (0, 0)),
            pl.BlockSpec((D_MODEL, tf), lambda m, f: (0, f)),
            pl.BlockSpec((D_MODEL, tf), lambda m, f: (0, f)),
            pl.BlockSpec((tf, D_MODEL), lambda m, f: (f, 0)),
        ],
        out_specs=pl.BlockSpec((tm, D_MODEL), lambda m, f: (m, 0)),
        out_shape=jax.ShapeDtypeStruct((t, D_MODEL), F32),
        scratch_shapes=[pltpu.VMEM((tm, D_MODEL), BF16), pltpu.VMEM((tm, D_MODEL), F32)],
        compiler_params=_cparams(("parallel", "arbitrary")),
        name="ffn",
    )(x, nw, wg, wu, wd)


def _router_kernel(x_ref, nw_ref, rw_ref, h_ref, comb_ref):
    h = _rms_h(x_ref[...], nw_ref[...])
    h_ref[...] = h.astype(BF16)
    h_hi, h_lo = _split2(h)
    w_hi, w_lo = _split2(rw_ref[...])
    logits = _dot(h_hi, w_hi) + _dot(h_hi, w_lo) + _dot(h_lo, w_hi)
    lane = lax.broadcasted_iota(jnp.int32, (1, LANES), 1)
    lg = jnp.where(lane < N_EXPERTS, logits, -jnp.inf)
    m1 = jnp.max(lg, axis=-1, keepdims=True)
    i1 = jnp.min(jnp.where(lg == m1, lane, LANES), axis=-1, keepdims=True)
    lg2 = jnp.where(lane == i1, -jnp.inf, lg)
    m2 = jnp.max(lg2, axis=-1, keepdims=True)
    i2 = jnp.min(jnp.where(lg2 == m2, lane, LANES), axis=-1, keepdims=True)
    e = jnp.exp(m2 - m1)
    g1 = 1.0 / (1.0 + e)
    g2 = e / (1.0 + e)
    comb_ref[...] = jnp.where(lane == i1, g1, 0.0) + jnp.where(lane == i2, g2, 0.0)


def _router(x, nw, rw, tm=512):
    t = x.shape[0]
    return pl.pallas_call(
        _router_kernel,
        grid=(t // tm,),
        in_specs=[
            pl.BlockSpec((tm, D_MODEL), lambda m: (m, 0)),
            pl.BlockSpec((1, D_MODEL), lambda m: (0, 0)),
            pl.BlockSpec((D_MODEL, LANES), lambda m: (0, 0)),
        ],
        out_specs=[
            pl.BlockSpec((tm, D_MODEL), lambda m: (m, 0)),
            pl.BlockSpec((tm, LANES), lambda m: (m, 0)),
        ],
        out_shape=[jax.ShapeDtypeStruct((t, D_MODEL), BF16), jax.ShapeDtypeStruct((t, LANES), F32)],
        compiler_params=_cparams(("parallel",)),
        name="router",
    )(x, nw, rw)


def _moe_kernel(x_ref, h_ref, comb_ref, wg_ref, wu_ref, wd_ref, o_ref, acc_ref):
    e = pl.program_id(1)
    f = pl.program_id(2)

    @pl.when((e == 0) & (f == 0))
    def _():
        acc_ref[...] = jnp.zeros_like(acc_ref)

    lane = lax.broadcasted_iota(jnp.int32, (1, LANES), 1)
    gate = jnp.sum(jnp.where(lane == e, comb_ref[...], 0.0), axis=-1, keepdims=True)
    h = h_ref[...]
    act = _silu(_dot(h, wg_ref[0])) * _dot(h, wu_ref[0]) * gate
    acc_ref[...] += _dot(act.astype(BF16), wd_ref[0])

    @pl.when((e == pl.num_programs(1) - 1) & (f == pl.num_programs(2) - 1))
    def _():
        o_ref[...] = x_ref[...] + acc_ref[...]


def _moe_dense(x, h, comb, wg, wu, wd, tm=1024, tf=512):
    t = x.shape[0]
    return pl.pallas_call(
        _moe_kernel,
        grid=(t // tm, N_EXPERTS, D_FF // tf),
        in_specs=[
            pl.BlockSpec((tm, D_MODEL), lambda m, e, f: (m, 0)),
            pl.BlockSpec((tm, D_MODEL), lambda m, e, f: (m, 0)),
            pl.BlockSpec((tm, LANES), lambda m, e, f: (m, 0)),
            pl.BlockSpec((1, D_MODEL, tf), lambda m, e, f: (e, 0, f)),
            pl.BlockSpec((1, D_MODEL, tf), lambda m, e, f: (e, 0, f)),
            pl.BlockSpec((1, tf, D_MODEL), lambda m, e, f: (e, f, 0)),
        ],
        out_specs=pl.BlockSpec((tm, D_MODEL), lambda m, e, f: (m, 0)),
        out_shape=jax.ShapeDtypeStruct((t, D_MODEL), F32),
        scratch_shapes=[pltpu.VMEM((tm, D_MODEL), F32)],
        compiler_params=_cparams(("parallel", "arbitrary", "arbitrary")),
        name="moe",
    )(x, h, comb, wg, wu, wd)


def _moe_layer(xf, norm_w, router_w, wg, wu, wd):
    rw = jnp.pad(router_w.astype(F32), ((0, 0), (0, LANES - N_EXPERTS)))
    h, comb = _router(xf, _row(norm_w), rw)
    return _moe_dense(xf, h, comb, wg.astype(BF16), wu.astype(BF16), wd.astype(BF16))


def _final_norm_kernel(x_ref, nw_ref, o_ref):
    o_ref[...] = _rms_h(x_ref[...], nw_ref[...])


def _final_norm(x, nw, tm=1024):
    t = x.shape[0]
    return pl.pallas_call(
        _final_norm_kernel,
        grid=(t // tm,),
        in_specs=[pl.BlockSpec((tm, D_MODEL), lambda m: (m, 0)), pl.BlockSpec((1, D_MODEL), lambda m: (0, 0))],
        out_specs=pl.BlockSpec((tm, D_MODEL), lambda m: (m, 0)),
        out_shape=jax.ShapeDtypeStruct((t, D_MODEL), F32),
        compiler_params=_cparams(("parallel",)),
        name="final_norm",
    )(x, nw)


def _rotary_perm():
    half = HEAD_DIM // 2
    first = [h * HEAD_DIM + d for h in range(RET_HEADS) for d in range(half)]
    second = [h * HEAD_DIM + half + d for h in range(RET_HEADS) for d in range(half)]
    return np.asarray(first + second, np.int32)


def _layout_w_in(w_in):
    perm = _rotary_perm()
    dt0 = 4 * RET_DIM + SSD_INNER + SSD_CONV_DIM
    main = jnp.concatenate([
        w_in[:, perm], w_in[:, RET_DIM + perm], w_in[:, 2 * RET_DIM:dt0], w_in[:, dt0 + SSD_HEADS:],
    ], axis=1).astype(BF16)
    wdt = jnp.pad(w_in[:, dt0:dt0 + SSD_HEADS], ((0, 0), (0, LANES - SSD_HEADS))).astype(BF16)
    return main, wdt


def _row(v, width=None):
    v = v.reshape(1, -1).astype(F32)
    if width is not None and v.shape[1] < width:
        v = jnp.pad(v, ((0, 0), (0, width - v.shape[1])))
    return v


def kernel(x, norm1_w, w_in, ret_norm_w, ssd_conv_w, ssd_conv_b, ssd_dt_bias, ssd_a_log, ssd_d, ssd_norm_w, w_out, norm2_w, ffn_w_gate, ffn_w_up, ffn_w_down, moe_router, moe_w_gate, moe_w_up, moe_w_down, final_norm_w):
    bsz, seq, _ = x.shape
    depth = w_in.shape[0]
    ret_tables = _retention_tables(seq)
    ssd_tables = _ssd_tables()
    sb_u = _stickbreak_tables()

    xf = x.reshape(bsz * seq, D_MODEL)
    for layer in range(depth):
        w_main, w_dt = _layout_w_in(w_in[layer])
        proj, dt_raw = _inproj(xf, _row(norm1_w[layer]), w_main, w_dt)
        ro = _retention(proj, ret_tables, _row(ret_norm_w[layer]), bsz, seq)
        so = _ssd(proj, dt_raw, ssd_conv_w[layer].astype(F32), _row(ssd_conv_b[layer]),
                  _row(ssd_dt_bias[layer], LANES), _row(ssd_a_log[layer], LANES),
                  _row(jnp.repeat(ssd_d[layer], HEAD_DIM)), _row(ssd_norm_w[layer]), ssd_tables, bsz, seq)
        bo = _stickbreak(proj, sb_u, bsz, seq)
        xf = _outproj(xf, ro, so, bo, w_out[layer].astype(BF16))
        j = layer // 2
        if layer % 2 == 0:
            xf = _ffn(xf, _row(norm2_w[layer]), ffn_w_gate[j].astype(BF16), ffn_w_up[j].astype(BF16),
                      ffn_w_down[j].astype(BF16))
        else:
            xf = _moe_layer(xf, norm2_w[layer], moe_router[j], moe_w_gate[j], moe_w_up[j], moe_w_down[j])
    out = _final_norm(xf, _row(final_norm_w))
    return out.reshape(bsz, seq, D_MODEL)
```

```python
import functools
import math

import numpy as np
import jax
import jax.numpy as jnp
from jax import lax
from jax.experimental import pallas as pl
from jax.experimental.pallas import tpu as pltpu

F32 = jnp.float32
BF16 = jnp.bfloat16

D_MODEL = 1024
HEAD_DIM = 64
RET_HEADS = 4
RET_DIM = RET_HEADS * HEAD_DIM
SSD_HEADS = 8
SSD_INNER = SSD_HEADS * HEAD_DIM
SSD_GROUPS = 2
SSD_STATE = 64
SSD_CONV = 4
SSD_BC = SSD_GROUPS * SSD_STATE
SSD_CONV_DIM = SSD_INNER + 2 * SSD_BC
SB_HEADS = 4
SB_DIM = SB_HEADS * HEAD_DIM
D_MIX = RET_DIM + SSD_INNER + SB_DIM
CHUNK = 128
D_FF = 3584
N_EXPERTS = 8
ROPE_BASE = 10000.0
EPS = 1e-6

LANES = 128
D_PROJ = 3072
OFF_RQ, OFF_RK, OFF_RV, OFF_RG = 0, 256, 512, 768
OFF_SZ, OFF_SXBC = 1024, 1536
OFF_BQ, OFF_BK, OFF_BV = 2304, 2560, 2816

VMEM_LIMIT = 56 * 1024 * 1024


def _cparams(sem):
    return pltpu.CompilerParams(dimension_semantics=sem, vmem_limit_bytes=VMEM_LIMIT)


def _dot(a, b):
    return jnp.dot(a, b, preferred_element_type=F32)


def _dot_nt(a, b):
    return lax.dot_general(a, b, (((1,), (1,)), ((), ())), preferred_element_type=F32)


def _split2(x):
    hi = x.astype(BF16)
    lo = (x - hi.astype(F32)).astype(BF16)
    return hi, lo


def _split3(x):
    hi = x.astype(BF16)
    r = x - hi.astype(F32)
    mid = r.astype(BF16)
    lo = (r - mid.astype(F32)).astype(BF16)
    return hi, mid, lo


def _dot_x_const(x, c):
    hi, lo = _split2(x)
    return _dot(hi, c) + _dot(lo, c)


def _silu(x):
    return x * jax.nn.sigmoid(x)


def _softplus(x):
    e = jnp.exp2(jnp.abs(x) * (-math.log2(math.e)))
    return jnp.maximum(x, 0.0) + jnp.log(1.0 + e)


def _rms_h(x, w):
    ms = jnp.mean(x * x, axis=-1, keepdims=True)
    return x * lax.rsqrt(ms + EPS) * w


def _inproj_kernel(x_ref, nw_ref, w_ref, wdt_ref, o_ref, dt_ref):
    h = _rms_h(x_ref[...], nw_ref[...]).astype(BF16)
    step = 512
    for c in range(D_PROJ // step):
        o_ref[:, c * step:(c + 1) * step] = _dot(h, w_ref[:, c * step:(c + 1) * step]).astype(BF16)
    dt_ref[...] = _dot(h, wdt_ref[...])


def _inproj(x, nw, w, wdt, tm=512):
    t = x.shape[0]
    return pl.pallas_call(
        _inproj_kernel,
        grid=(t // tm,),
        in_specs=[
            pl.BlockSpec((tm, D_MODEL), lambda m: (m, 0)),
            pl.BlockSpec((1, D_MODEL), lambda m: (0, 0)),
            pl.BlockSpec((D_MODEL, D_PROJ), lambda m: (0, 0)),
            pl.BlockSpec((D_MODEL, LANES), lambda m: (0, 0)),
        ],
        out_specs=[
            pl.BlockSpec((tm, D_PROJ), lambda m: (m, 0)),
            pl.BlockSpec((tm, LANES), lambda m: (m, 0)),
        ],
        out_shape=[jax.ShapeDtypeStruct((t, D_PROJ), BF16), jax.ShapeDtypeStruct((t, LANES), F32)],
        compiler_params=_cparams(("parallel",)),
        name="inproj",
    )(x, nw, w, wdt)


def _retention_kernel(q_ref, k_ref, v_ref, g_ref, cos_ref, sin_ref, intra_ref, qdec_ref, kdec_ref,
                      cd_ref, avg_ref, nw_ref, o_ref, state_ref):
    c = pl.program_id(1)

    @pl.when(c == 0)
    def _():
        state_ref[...] = jnp.zeros_like(state_ref)

    cos = cos_ref[...]
    sin = sin_ref[...]

    def rot(t):
        t1, t2 = t[:, :LANES], t[:, LANES:]
        return jnp.concatenate([t1 * cos - t2 * sin, t1 * sin + t2 * cos], axis=-1)

    rq = rot(q_ref[...].astype(F32))
    rk = rot(k_ref[...].astype(F32))
    vb = v_ref[...]

    lane = lax.broadcasted_iota(jnp.int32, (1, RET_DIM), 1)
    head_qk = (lane % LANES) // (HEAD_DIM // 2)
    head_v = lane // HEAD_DIM

    q_stack = jnp.concatenate([jnp.where(head_qk == h, rq, 0.0) for h in range(RET_HEADS)], axis=0)
    scores = _dot_nt(q_stack.astype(BF16), rk.astype(BF16)) * intra_ref[...]
    y_stack = _dot(scores.astype(BF16), vb)
    y = jnp.where(head_v == 0, y_stack[:CHUNK], 0.0)
    for h in range(1, RET_HEADS):
        y = y + jnp.where(head_v == h, y_stack[h * CHUNK:(h + 1) * CHUNK], 0.0)

    state = state_ref[...]
    y = y + _dot((rq * qdec_ref[...]).astype(BF16), state.astype(BF16))
    kd = (rk * kdec_ref[...]).astype(BF16)
    kv = lax.dot_general(kd, vb, (((0,), (0,)), ((), ())), preferred_element_type=F32)
    cd = cd_ref[...]
    state_ref[...] = state * cd + jnp.where(cd != 0.0, kv, 0.0)

    avg = avg_ref[...]
    mu = _dot_x_const(y, avg)
    d = y - mu
    var = _dot_x_const(d * d, avg)
    yn = d * lax.rsqrt(var + EPS)
    o_ref[...] = (yn * nw_ref[...] * _silu(g_ref[...].astype(F32))).astype(BF16)


def _retention_tables(seq):
    f = np.float32
    half = HEAD_DIM // 2
    inv = (1.0 / (f(ROPE_BASE) ** (np.arange(half, dtype=f) / f(half)))).astype(f)
    ang = np.arange(seq, dtype=f)[:, None] * inv[None, :]
    cos = np.tile(np.cos(ang).astype(f), (1, RET_HEADS))
    sin = np.tile(np.sin(ang).astype(f), (1, RET_HEADS))
    log_gamma = np.log(f(1.0) - f(2.0) ** (f(-5.0) - np.arange(RET_HEADS, dtype=f))).astype(f)
    idx = np.arange(CHUNK, dtype=f)
    diff = idx[:, None] - idx[None, :]
    scale = f(HEAD_DIM ** -0.5)
    intra = np.where(diff >= 0, np.exp(log_gamma[:, None, None] * np.maximum(diff, 0.0)), 0.0).astype(f)
    intra = (intra * scale).reshape(RET_HEADS * CHUNK, CHUNK)
    head_qk = (np.arange(RET_DIM) % LANES) // half
    head_v = np.arange(RET_DIM) // HEAD_DIM
    qdec = np.exp(log_gamma[head_qk][None, :] * (idx[:, None] + 1.0)).astype(f)
    kdec = (np.exp(log_gamma[head_qk][None, :] * (CHUNK - 1.0 - idx[:, None])) * scale).astype(f)
    chunk_decay = np.exp(log_gamma * f(CHUNK)).astype(f)
    same = head_qk[:, None] == head_v[None, :]
    cd = np.where(same, chunk_decay[head_qk][:, None], 0.0).astype(f)
    avg = np.where(head_v[:, None] == head_v[None, :], 1.0 / HEAD_DIM, 0.0).astype(f)
    return (jnp.asarray(cos), jnp.asarray(sin), jnp.asarray(intra), jnp.asarray(qdec), jnp.asarray(kdec),
            jnp.asarray(cd), jnp.asarray(avg, dtype=BF16))


def _retention(proj, tables, nw, bsz, seq):
    nc = seq // CHUNK
    cos, sin, intra, qdec, kdec, cd, avg = tables
    w = RET_DIM

    def col(j):
        return pl.BlockSpec((CHUNK, w), lambda b, c: (b * nc + c, j))

    def const(shape):
        return pl.BlockSpec(shape, lambda b, c: (0, 0))

    return pl.pallas_call(
        _retention_kernel,
        grid=(bsz, nc),
        in_specs=[
            col(OFF_RQ // w), col(OFF_RK // w), col(OFF_RV // w), col(OFF_RG // w),
            pl.BlockSpec((CHUNK, LANES), lambda b, c: (c, 0)),
            pl.BlockSpec((CHUNK, LANES), lambda b, c: (c, 0)),
            const((RET_HEADS * CHUNK, CHUNK)), const((CHUNK, w)), const((CHUNK, w)),
            const((w, w)), const((w, w)), const((1, w)),
        ],
        out_specs=pl.BlockSpec((CHUNK, w), lambda b, c: (b * nc + c, 0)),
        out_shape=jax.ShapeDtypeStruct((bsz * seq, w), BF16),
        scratch_shapes=[pltpu.VMEM((w, w), F32)],
        compiler_params=_cparams(("parallel", "arbitrary")),
        name="retention",
    )(proj, proj, proj, proj, cos, sin, intra, qdec, kdec, cd, avg, nw)


def _ssd_kernel(z_ref, xbc_ref, dt_ref, cw_ref, cb_ref, dtb_ref, alog_ref, dskip_ref, nw_ref,
                tri_ref, exp_ref, o_ref, prev_ref, state_ref):
    c = pl.program_id(1)

    @pl.when(c == 0)
    def _():
        prev_ref[...] = jnp.zeros_like(prev_ref)
        state_ref[...] = jnp.zeros_like(state_ref)

    cur = xbc_ref[...].astype(F32)
    prv = prev_ref[...]
    row = lax.broadcasted_iota(jnp.int32, (CHUNK, 1), 0)
    acc = cur * cw_ref[SSD_CONV - 1:SSD_CONV, :] + cb_ref[...]
    for j in range(1, SSD_CONV):
        shifted = jnp.where(row >= j, pltpu.roll(cur, j, 0), pltpu.roll(prv, j, 0))
        acc = acc + shifted * cw_ref[SSD_CONV - 1 - j:SSD_CONV - j, :]
    prev_ref[...] = cur
    xa = _silu(acc)
    x = xa[:, :SSD_INNER]
    bm = xa[:, SSD_INNER:SSD_INNER + SSD_BC]
    cm = xa[:, SSD_INNER + SSD_BC:]

    dt = _softplus(dt_ref[...] + dtb_ref[...])
    a = dt * (-jnp.exp(alog_ref[...]))
    tri = tri_ref[...]
    a_hi, a_mid, a_lo = _split3(a)
    a_cs = _dot(tri, a_hi) + _dot(tri, a_mid) + _dot(tri, a_lo)
    a_cs_t = a_cs.T
    stack = jnp.concatenate([dt, jnp.exp(a_cs), jnp.exp(a_cs[CHUNK - 1:CHUNK, :] - a_cs)], axis=0)
    stack_x = _dot_x_const(stack, exp_ref[...])
    dt_x = stack_x[:CHUNK]
    ea_x = stack_x[CHUNK:2 * CHUNK]
    dec_x = stack_x[2 * CHUNK:]

    xdt = x * dt_x
    xdt_b = xdt.astype(BF16)
    bm_b = bm.astype(BF16)
    cm_b = cm.astype(BF16)
    lane_bc = lax.broadcasted_iota(jnp.int32, (1, SSD_BC), 1)
    cb = [_dot_nt(jnp.where(lane_bc // SSD_STATE == g, cm, 0.0).astype(BF16), bm_b) for g in range(SSD_GROUPS)]

    li = lax.broadcasted_iota(jnp.int32, (CHUNK, CHUNK), 0)
    si = lax.broadcasted_iota(jnp.int32, (CHUNK, CHUNK), 1)
    lower = li >= si
    heads_per_group = SSD_HEADS // SSD_GROUPS
    m_list = []
    for h in range(SSD_HEADS):
        seg = a_cs[:, h:h + 1] - a_cs_t[h:h + 1, :]
        l_mat = jnp.exp(jnp.where(lower, seg, -jnp.inf))
        m_list.append((cb[h // heads_per_group] * l_mat).astype(BF16))
    y_stack = _dot(jnp.concatenate(m_list, axis=0), xdt_b)
    lane_x = lax.broadcasted_iota(jnp.int32, (1, SSD_INNER), 1)
    head_x = lane_x // HEAD_DIM
    y = jnp.where(head_x == 0, y_stack[:CHUNK], 0.0)
    for h in range(1, SSD_HEADS):
        y = y + jnp.where(head_x == h, y_stack[h * CHUNK:(h + 1) * CHUNK], 0.0)

    state = state_ref[...]
    y = y + _dot(cm_b, state.astype(BF16)) * ea_x
    contrib = _dot(bm.T.astype(BF16), (xdt * dec_x).astype(BF16))
    row_g = lax.broadcasted_iota(jnp.int32, (SSD_BC, 1), 0) // SSD_STATE
    same_group = row_g == lane_x // (SSD_INNER // SSD_GROUPS)
    state_ref[...] = state * ea_x[CHUNK - 1:CHUNK, :] + jnp.where(same_group, contrib, 0.0)

    y = y + dskip_ref[...] * x
    gy = y * _silu(z_ref[...].astype(F32))
    gw = SSD_INNER // SSD_GROUPS
    outs = []
    for g in range(SSD_GROUPS):
        part = gy[:, g * gw:(g + 1) * gw]
        outs.append(part * lax.rsqrt(jnp.mean(part * part, axis=-1, keepdims=True) + EPS))
    o_ref[...] = (jnp.concatenate(outs, axis=-1) * nw_ref[...]).astype(BF16)


def _ssd_tables():
    idx = np.arange(CHUNK)
    tri = (idx[:, None] >= idx[None, :]).astype(np.float32)
    expand = np.zeros((LANES, SSD_INNER), np.float32)
    for h in range(SSD_HEADS):
        expand[h, h * HEAD_DIM:(h + 1) * HEAD_DIM] = 1.0
    return jnp.asarray(tri, dtype=BF16), jnp.asarray(expand, dtype=BF16)


def _ssd(proj, dt_raw, conv_w, conv_b, dt_bias, a_log, d_skip, nw, tables, bsz, seq):
    nc = seq // CHUNK
    tri, expand = tables

    def const(shape):
        return pl.BlockSpec(shape, lambda b, c: (0, 0))

    return pl.pallas_call(
        _ssd_kernel,
        grid=(bsz, nc),
        in_specs=[
            pl.BlockSpec((CHUNK, SSD_INNER), lambda b, c: (b * nc + c, OFF_SZ // SSD_INNER)),
            pl.BlockSpec((CHUNK, SSD_CONV_DIM), lambda b, c: (b * nc + c, OFF_SXBC // SSD_CONV_DIM)),
            pl.BlockSpec((CHUNK, LANES), lambda b, c: (b * nc + c, 0)),
            const((SSD_CONV, SSD_CONV_DIM)), const((1, SSD_CONV_DIM)), const((1, LANES)), const((1, LANES)),
            const((1, SSD_INNER)), const((1, SSD_INNER)), const((CHUNK, CHUNK)), const((LANES, SSD_INNER)),
        ],
        out_specs=pl.BlockSpec((CHUNK, SSD_INNER), lambda b, c: (b * nc + c, 0)),
        out_shape=jax.ShapeDtypeStruct((bsz * seq, SSD_INNER), BF16),
        scratch_shapes=[pltpu.VMEM((CHUNK, SSD_CONV_DIM), F32), pltpu.VMEM((SSD_BC, SSD_INNER), F32)],
        compiler_params=_cparams(("parallel", "arbitrary")),
        name="ssd",
    )(proj, proj, dt_raw, conv_w, conv_b, dt_bias, a_log, d_skip, nw, tri, expand)


SB_BLOCK = 256


def _stickbreak_kernel(q_ref, k_ref, v_ref, u_ref, o_ref, acc_ref):
    i = pl.program_id(1)
    blk = SB_BLOCK
    q = q_ref[...] * jnp.asarray(HEAD_DIM ** -0.5, BF16)
    lane = lax.broadcasted_iota(jnp.int32, (1, SB_DIM), 1)
    head = lane // HEAD_DIM
    q_heads = [jnp.where(head == h, q, jnp.zeros_like(q)) for h in range(SB_HEADS)]
    u = u_ref[...]
    strict_lower = (lax.broadcasted_iota(jnp.int32, (blk, blk), 1)
                    < lax.broadcasted_iota(jnp.int32, (blk, blk), 0))

    def block(j, carries, diagonal):
        start = pl.multiple_of(j * blk, blk)
        kb = k_ref[pl.ds(start, blk), :]
        vb = v_ref[pl.ds(start, blk), :]
        new = []
        for h in range(SB_HEADS):
            z = _dot_nt(q_heads[h], kb)
            sp = _softplus(z)
            if diagonal:
                sp = jnp.where(strict_lower, sp, 0.0)
            cs = _dot(sp.astype(BF16), u)
            w = jnp.exp(z - cs - carries[h])
            if diagonal:
                w = jnp.where(strict_lower, w, 0.0)
            pv = _dot(w.astype(BF16), vb)
            if diagonal:
                acc_ref[h] = pv
            else:
                acc_ref[h] += pv
            new.append(carries[h] + jnp.sum(sp, axis=-1, keepdims=True))
        return tuple(new)

    zero = jnp.zeros((blk, 1), F32)
    carries = block(i, (zero,) * SB_HEADS, True)
    lax.fori_loop(0, i, lambda jj, c: block(i - 1 - jj, c, False), carries)
    out = jnp.where(head == 0, acc_ref[0], 0.0)
    for h in range(1, SB_HEADS):
        out = out + jnp.where(head == h, acc_ref[h], 0.0)
    o_ref[...] = out.astype(BF16)


def _stickbreak_tables():
    idx = np.arange(SB_BLOCK)
    rev = (idx[:, None] >= idx[None, :]).astype(np.float32)
    return jnp.asarray(rev, dtype=BF16)


def _stickbreak(proj, u, bsz, seq):
    nq = seq // SB_BLOCK
    w = SB_DIM
    return pl.pallas_call(
        _stickbreak_kernel,
        grid=(bsz, nq),
        in_specs=[
            pl.BlockSpec((SB_BLOCK, w), lambda b, i: (b * nq + i, OFF_BQ // w)),
            pl.BlockSpec((seq, w), lambda b, i: (b, OFF_BK // w)),
            pl.BlockSpec((seq, w), lambda b, i: (b, OFF_BV // w)),
            pl.BlockSpec((SB_BLOCK, SB_BLOCK), lambda b, i: (0, 0)),
        ],
        out_specs=pl.BlockSpec((SB_BLOCK, w), lambda b, i: (b * nq + i, 0)),
        out_shape=jax.ShapeDtypeStruct((bsz * seq, w), BF16),
        scratch_shapes=[pltpu.VMEM((SB_HEADS, SB_BLOCK, w), F32)],
        compiler_params=_cparams(("parallel", "arbitrary")),
        name="stickbreak",
    )(proj, proj, proj, u)


def _outproj_kernel(x_ref, ro_ref, so_ref, bo_ref, w_ref, o_ref):
    acc = _dot(ro_ref[...], w_ref[:RET_DIM, :])
    acc = acc + _dot(so_ref[...], w_ref[RET_DIM:RET_DIM + SSD_INNER, :])
    acc = acc + _dot(bo_ref[...], w_ref[RET_DIM + SSD_INNER:, :])
    o_ref[...] = x_ref[...] + acc


def _outproj(x, ro, so, bo, w, tm=512):
    t = x.shape[0]
    return pl.pallas_call(
        _outproj_kernel,
        grid=(t // tm,),
        in_specs=[
            pl.BlockSpec((tm, D_MODEL), lambda m: (m, 0)),
            pl.BlockSpec((tm, RET_DIM), lambda m: (m, 0)),
            pl.BlockSpec((tm, SSD_INNER), lambda m: (m, 0)),
            pl.BlockSpec((tm, SB_DIM), lambda m: (m, 0)),
            pl.BlockSpec((D_MIX, D_MODEL), lambda m: (0, 0)),
        ],
        out_specs=pl.BlockSpec((tm, D_MODEL), lambda m: (m, 0)),
        out_shape=jax.ShapeDtypeStruct((t, D_MODEL), F32),
        compiler_params=_cparams(("parallel",)),
        name="outproj",
    )(x, ro, so, bo, w)


def _ffn_kernel(x_ref, nw_ref, wg_ref, wu_ref, wd_ref, o_ref, h_ref, acc_ref):
    f = pl.program_id(1)

    @pl.when(f == 0)
    def _():
        h_ref[...] = _rms_h(x_ref[...], nw_ref[...]).astype(BF16)
        acc_ref[...] = jnp.zeros_like(acc_ref)

    h = h_ref[...]
    act = _silu(_dot(h, wg_ref[...])) * _dot(h, wu_ref[...])
    acc_ref[...] += _dot(act.astype(BF16), wd_ref[...])

    @pl.when(f == pl.num_programs(1) - 1)
    def _():
        o_ref[...] = x_ref[...] + acc_ref[...]


def _ffn(x, nw, wg, wu, wd, tm=1024, tf=512):
    t = x.shape[0]
    return pl.pallas_call(
        _ffn_kernel,
        grid=(t // tm, D_FF // tf),
        in_specs=[
            pl.BlockSpec((tm, D_MODEL), lambda m, f: (m, 0)),
            pl.BlockSpec((1, D_MODEL), lambda m, f: (0, 0)),
            pl.BlockSpec((D_MODEL, tf), lambda m, f: (0, f)),
            pl.BlockSpec((D_MODEL, tf), lambda m, f: (0, f)),
            pl.BlockSpec((tf, D_MODEL), lambda m, f: (f, 0)),
        ],
        out_specs=pl.BlockSpec((tm, D_MODEL), lambda m, f: (m, 0)),
        out_shape=jax.ShapeDtypeStruct((t, D_MODEL), F32),
        scratch_shapes=[pltpu.VMEM((tm, D_MODEL), BF16), pltpu.VMEM((tm, D_MODEL), F32)],
        compiler_params=_cparams(("parallel", "arbitrary")),
        name="ffn",
    )(x, nw, wg, wu, wd)


def _router_kernel(x_ref, nw_ref, rw_ref, h_ref, idx_ref, gate_ref):
    h = _rms_h(x_ref[...], nw_ref[...])
    h_ref[...] = h.astype(BF16)
    h_hi, h_lo = _split2(h)
    w_hi, w_lo = _split2(rw_ref[...])
    logits = _dot(h_hi, w_hi) + _dot(h_hi, w_lo) + _dot(h_lo, w_hi)
    lane = lax.broadcasted_iota(jnp.int32, (1, LANES), 1)
    lg = jnp.where(lane < N_EXPERTS, logits, -jnp.inf)
    m1 = jnp.max(lg, axis=-1, keepdims=True)
    i1 = jnp.min(jnp.where(lg == m1, lane, LANES), axis=-1, keepdims=True)
    lg2 = jnp.where(lane == i1, -jnp.inf, lg)
    m2 = jnp.max(lg2, axis=-1, keepdims=True)
    i2 = jnp.min(jnp.where(lg2 == m2, lane, LANES), axis=-1, keepdims=True)
    e = jnp.exp(m2 - m1)
    g1 = 1.0 / (1.0 + e)
    g2 = e / (1.0 + e)
    idx_ref[...] = jnp.where(lane == 0, i1, jnp.where(lane == 1, i2, 0))
    gate_ref[...] = jnp.where(lane == 0, g1, jnp.where(lane == 1, g2, 0.0))


def _router(x, nw, rw, tm=512):
    t = x.shape[0]
    return pl.pallas_call(
        _router_kernel,
        grid=(t // tm,),
        in_specs=[
            pl.BlockSpec((tm, D_MODEL), lambda m: (m, 0)),
            pl.BlockSpec((1, D_MODEL), lambda m: (0, 0)),
            pl.BlockSpec((D_MODEL, LANES), lambda m: (0, 0)),
        ],
        out_specs=[
            pl.BlockSpec((tm, D_MODEL), lambda m: (m, 0)),
            pl.BlockSpec((tm, LANES), lambda m: (m, 0)),
            pl.BlockSpec((tm, LANES), lambda m: (m, 0)),
        ],
        out_shape=[jax.ShapeDtypeStruct((t, D_MODEL), BF16), jax.ShapeDtypeStruct((t, LANES), jnp.int32),
                   jax.ShapeDtypeStruct((t, LANES), F32)],
        compiler_params=_cparams(("parallel",)),
        name="router",
    )(x, nw, rw)


MOE_TM = 512


def _moe_plan(e2):
    t = e2.shape[0]
    p = 2 * t
    ef = e2.reshape(p)
    onehot = (ef[:, None] == jnp.arange(N_EXPERTS, dtype=jnp.int32)[None, :]).astype(jnp.int32)
    csum = jnp.cumsum(onehot, axis=0)
    counts = csum[-1]
    rank = jnp.sum((csum - onehot) * onehot, axis=1)
    padded = ((counts + MOE_TM - 1) // MOE_TM) * MOE_TM
    ends = jnp.cumsum(padded)
    starts = ends - padded
    dst = jnp.sum(onehot * starts[None, :], axis=1) + rank
    n_tiles = p // MOE_TM + N_EXPERTS
    tile_start = jnp.arange(n_tiles, dtype=jnp.int32) * MOE_TM
    tile_expert = jnp.sum((tile_start[:, None] >= ends[None, :]).astype(jnp.int32), axis=1)
    tile_expert = jnp.minimum(tile_expert, N_EXPERTS - 1)
    n_used = (ends[-1] // MOE_TM).reshape(1).astype(jnp.int32)
    src = jnp.zeros((n_tiles * MOE_TM,), jnp.int32).at[dst].set(jnp.arange(p, dtype=jnp.int32) // 2)
    return dst.reshape(t, 2), src, tile_expert.astype(jnp.int32), n_used


def _expert_kernel(te_ref, nu_ref, hs_ref, wg_ref, wu_ref, wd_ref, o_ref, acc_ref):
    i = pl.program_id(0)
    f = pl.program_id(1)

    @pl.when(f == 0)
    def _():
        acc_ref[...] = jnp.zeros_like(acc_ref)

    @pl.when(i < nu_ref[0])
    def _():
        h = hs_ref[...]
        act = _silu(_dot(h, wg_ref[0])) * _dot(h, wu_ref[0])
        acc_ref[...] += _dot(act.astype(BF16), wd_ref[0])

    @pl.when(f == pl.num_programs(1) - 1)
    def _():
        o_ref[...] = acc_ref[...].astype(BF16)


def _experts(hs, tile_expert, n_used, wg, wu, wd, tf=512):
    rows = hs.shape[0]
    nf = D_FF // tf

    def fblock(i, f, nu):
        return jnp.where(i < nu[0], f, nf - 1)

    grid_spec = pltpu.PrefetchScalarGridSpec(
        num_scalar_prefetch=2,
        grid=(rows // MOE_TM, nf),
        in_specs=[
            pl.BlockSpec((MOE_TM, D_MODEL), lambda i, f, te, nu: (i, 0)),
            pl.BlockSpec((1, D_MODEL, tf), lambda i, f, te, nu: (te[i], 0, fblock(i, f, nu))),
            pl.BlockSpec((1, D_MODEL, tf), lambda i, f, te, nu: (te[i], 0, fblock(i, f, nu))),
            pl.BlockSpec((1, tf, D_MODEL), lambda i, f, te, nu: (te[i], fblock(i, f, nu), 0)),
        ],
        out_specs=pl.BlockSpec((MOE_TM, D_MODEL), lambda i, f, te, nu: (i, 0)),
        scratch_shapes=[pltpu.VMEM((MOE_TM, D_MODEL), F32)],
    )
    return pl.pallas_call(
        _expert_kernel,
        grid_spec=grid_spec,
        out_shape=jax.ShapeDtypeStruct((rows, D_MODEL), BF16),
        compiler_params=_cparams(("arbitrary", "arbitrary")),
        name="experts",
    )(tile_expert, n_used, hs, wg, wu, wd)


def _combine_kernel(x_ref, y1_ref, y2_ref, gate_ref, nw_ref, o_ref, *, final_norm):
    g = gate_ref[...]
    out = x_ref[...] + g[:, 0:1] * y1_ref[...].astype(F32) + g[:, 1:2] * y2_ref[...].astype(F32)
    if final_norm:
        out = _rms_h(out, nw_ref[...])
    o_ref[...] = out


def _combine(x, y1, y2, gates, nw, final_norm, tm=1024):
    t = x.shape[0]
    return pl.pallas_call(
        functools.partial(_combine_kernel, final_norm=final_norm),
        grid=(t // tm,),
        in_specs=[
            pl.BlockSpec((tm, D_MODEL), lambda m: (m, 0)),
            pl.BlockSpec((tm, D_MODEL), lambda m: (m, 0)),
            pl.BlockSpec((tm, D_MODEL), lambda m: (m, 0)),
            pl.BlockSpec((tm, LANES), lambda m: (m, 0)),
            pl.BlockSpec((1, D_MODEL), lambda m: (0, 0)),
        ],
        out_specs=pl.BlockSpec((tm, D_MODEL), lambda m: (m, 0)),
        out_shape=jax.ShapeDtypeStruct((t, D_MODEL), F32),
        compiler_params=_cparams(("parallel",)),
        name="combine",
    )(x, y1, y2, gates, nw)


def _moe_layer(xf, norm_w, router_w, wg, wu, wd, final_w=None):
    rw = jnp.pad(router_w.astype(F32), ((0, 0), (0, LANES - N_EXPERTS)))
    h, idx, gates = _router(xf, _row(norm_w), rw)
    dst, src, tile_expert, n_used = _moe_plan(idx[:, :2])
    hs = jnp.take(h, src, axis=0)
    ys = _experts(hs, tile_expert, n_used, wg.astype(BF16), wu.astype(BF16), wd.astype(BF16))
    y1 = jnp.take(ys, dst[:, 0], axis=0)
    y2 = jnp.take(ys, dst[:, 1], axis=0)
    nw = _row(final_w) if final_w is not None else jnp.ones((1, D_MODEL), F32)
    return _combine(xf, y1, y2, gates, nw, final_w is not None)


def _final_norm_kernel(x_ref, nw_ref, o_ref):
    o_ref[...] = _rms_h(x_ref[...], nw_ref[...])


def _final_norm(x, nw, tm=1024):
    t = x.shape[0]
    return pl.pallas_call(
        _final_norm_kernel,
        grid=(t // tm,),
        in_specs=[pl.BlockSpec((tm, D_MODEL), lambda m: (m, 0)), pl.BlockSpec((1, D_MODEL), lambda m: (0, 0))],
        out_specs=pl.BlockSpec((tm, D_MODEL), lambda m: (m, 0)),
        out_shape=jax.ShapeDtypeStruct((t, D_MODEL), F32),
        compiler_params=_cparams(("parallel",)),
        name="final_norm",
    )(x, nw)


def _rotary_perm():
    half = HEAD_DIM // 2
    first = [h * HEAD_DIM + d for h in range(RET_HEADS) for d in range(half)]
    second = [h * HEAD_DIM + half + d for h in range(RET_HEADS) for d in range(half)]
    return np.asarray(first + second, np.int32)


def _layout_w_in(w_in):
    perm = _rotary_perm()
    dt0 = 4 * RET_DIM + SSD_INNER + SSD_CONV_DIM
    main = jnp.concatenate([
        w_in[:, perm], w_in[:, RET_DIM + perm], w_in[:, 2 * RET_DIM:dt0], w_in[:, dt0 + SSD_HEADS:],
    ], axis=1).astype(BF16)
    wdt = jnp.pad(w_in[:, dt0:dt0 + SSD_HEADS], ((0, 0), (0, LANES - SSD_HEADS))).astype(BF16)
    return main, wdt


def _row(v, width=None):
    v = v.reshape(1, -1).astype(F32)
    if width is not None and v.shape[1] < width:
        v = jnp.pad(v, ((0, 0), (0, width - v.shape[1])))
    return v


def kernel(x, norm1_w, w_in, ret_norm_w, ssd_conv_w, ssd_conv_b, ssd_dt_bias, ssd_a_log, ssd_d, ssd_norm_w, w_out, norm2_w, ffn_w_gate, ffn_w_up, ffn_w_down, moe_router, moe_w_gate, moe_w_up, moe_w_down, final_norm_w):
    bsz, seq, _ = x.shape
    depth = w_in.shape[0]
    ret_tables = _retention_tables(seq)
    ssd_tables = _ssd_tables()
    sb_u = _stickbreak_tables()

    xf = x.reshape(bsz * seq, D_MODEL)
    for layer in range(depth):
        w_main, w_dt = _layout_w_in(w_in[layer])
        proj, dt_raw = _inproj(xf, _row(norm1_w[layer]), w_main, w_dt)
        ro = _retention(proj, ret_tables, _row(ret_norm_w[layer]), bsz, seq)
        so = _ssd(proj, dt_raw, ssd_conv_w[layer].astype(F32), _row(ssd_conv_b[layer]),
                  _row(ssd_dt_bias[layer], LANES), _row(ssd_a_log[layer], LANES),
                  _row(jnp.repeat(ssd_d[layer], HEAD_DIM)), _row(ssd_norm_w[layer]), ssd_tables, bsz, seq)
        bo = _stickbreak(proj, sb_u, bsz, seq)
        xf = _outproj(xf, ro, so, bo, w_out[layer].astype(BF16))
        j = layer // 2
        if layer % 2 == 0:
            xf = _ffn(xf, _row(norm2_w[layer]), ffn_w_gate[j].astype(BF16), ffn_w_up[j].astype(BF16),
                      ffn_w_down[j].astype(BF16))
        else:
            last = layer == depth - 1
            xf = _moe_layer(xf, norm2_w[layer], moe_router[j], moe_w_gate[j], moe_w_up[j], moe_w_down[j],
                            final_norm_w if last else None)
    if depth % 2 == 1:
        xf = _final_norm(xf, _row(final_norm_w))
    return xf.reshape(bsz, seq, D_MODEL)
```

```python
import functools
import math

import numpy as np
import jax
import jax.numpy as jnp
from jax import lax
from jax.experimental import pallas as pl
from jax.experimental.pallas import tpu as pltpu

F32 = jnp.float32
BF16 = jnp.bfloat16

D_MODEL = 1024
HEAD_DIM = 64
RET_HEADS = 4
RET_DIM = RET_HEADS * HEAD_DIM
SSD_HEADS = 8
SSD_INNER = SSD_HEADS * HEAD_DIM
SSD_GROUPS = 2
SSD_STATE = 64
SSD_CONV = 4
SSD_BC = SSD_GROUPS * SSD_STATE
SSD_CONV_DIM = SSD_INNER + 2 * SSD_BC
SB_HEADS = 4
SB_DIM = SB_HEADS * HEAD_DIM
D_MIX = RET_DIM + SSD_INNER + SB_DIM
CHUNK = 128
D_FF = 3584
N_EXPERTS = 8
ROPE_BASE = 10000.0
EPS = 1e-6

LANES = 128
D_PROJ = 3072
OFF_RQ, OFF_RK, OFF_RV, OFF_RG = 0, 256, 512, 768
OFF_SZ, OFF_SXBC = 1024, 1536
OFF_BQ, OFF_BK, OFF_BV = 2304, 2560, 2816

VMEM_LIMIT = 56 * 1024 * 1024


def _cparams(sem):
    return pltpu.CompilerParams(dimension_semantics=sem, vmem_limit_bytes=VMEM_LIMIT)


def _dot(a, b):
    return jnp.dot(a, b, preferred_element_type=F32)


def _dot_nt(a, b):
    return lax.dot_general(a, b, (((1,), (1,)), ((), ())), preferred_element_type=F32)


def _split2(x):
    hi = x.astype(BF16)
    lo = (x - hi.astype(F32)).astype(BF16)
    return hi, lo


def _split3(x):
    hi = x.astype(BF16)
    r = x - hi.astype(F32)
    mid = r.astype(BF16)
    lo = (r - mid.astype(F32)).astype(BF16)
    return hi, mid, lo


def _dot_x_const(x, c):
    hi, lo = _split2(x)
    return _dot(hi, c) + _dot(lo, c)


def _silu(x):
    return x * jax.nn.sigmoid(x)


def _softplus(x):
    e = jnp.exp2(jnp.abs(x) * (-math.log2(math.e)))
    return jnp.maximum(x, 0.0) + jnp.log(1.0 + e)


def _rms_h(x, w):
    ms = jnp.mean(x * x, axis=-1, keepdims=True)
    return x * lax.rsqrt(ms + EPS) * w


def _inproj_kernel(x_ref, nw_ref, w_ref, wdt_ref, o_ref, dt_ref):
    h = _rms_h(x_ref[...], nw_ref[...]).astype(BF16)
    step = 512
    for c in range(D_PROJ // step):
        o_ref[:, c * step:(c + 1) * step] = _dot(h, w_ref[:, c * step:(c + 1) * step]).astype(BF16)
    dt_ref[...] = _dot(h, wdt_ref[...])


def _inproj(x, nw, w, wdt, tm=512):
    t = x.shape[0]
    return pl.pallas_call(
        _inproj_kernel,
        grid=(t // tm,),
        in_specs=[
            pl.BlockSpec((tm, D_MODEL), lambda m: (m, 0)),
            pl.BlockSpec((1, D_MODEL), lambda m: (0, 0)),
            pl.BlockSpec((D_MODEL, D_PROJ), lambda m: (0, 0)),
            pl.BlockSpec((D_MODEL, LANES), lambda m: (0, 0)),
        ],
        out_specs=[
            pl.BlockSpec((tm, D_PROJ), lambda m: (m, 0)),
            pl.BlockSpec((tm, LANES), lambda m: (m, 0)),
        ],
        out_shape=[jax.ShapeDtypeStruct((t, D_PROJ), BF16), jax.ShapeDtypeStruct((t, LANES), F32)],
        compiler_params=_cparams(("parallel",)),
        name="inproj",
    )(x, nw, w, wdt)


def _retention_kernel(q_ref, k_ref, v_ref, g_ref, cos_ref, sin_ref, intra_ref, qdec_ref, kdec_ref,
                      cd_ref, avg_ref, nw_ref, o_ref, state_ref):
    c = pl.program_id(1)

    @pl.when(c == 0)
    def _():
        state_ref[...] = jnp.zeros_like(state_ref)

    cos = cos_ref[...]
    sin = sin_ref[...]

    def rot(t):
        t1, t2 = t[:, :LANES], t[:, LANES:]
        return jnp.concatenate([t1 * cos - t2 * sin, t1 * sin + t2 * cos], axis=-1)

    rq = rot(q_ref[...].astype(F32))
    rk = rot(k_ref[...].astype(F32))
    vb = v_ref[...]

    lane = lax.broadcasted_iota(jnp.int32, (1, RET_DIM), 1)
    head_qk = (lane % LANES) // (HEAD_DIM // 2)
    head_v = lane // HEAD_DIM

    q_stack = jnp.concatenate([jnp.where(head_qk == h, rq, 0.0) for h in range(RET_HEADS)], axis=0)
    scores = _dot_nt(q_stack.astype(BF16), rk.astype(BF16)) * intra_ref[...]
    y_stack = _dot(scores.astype(BF16), vb)
    y = jnp.where(head_v == 0, y_stack[:CHUNK], 0.0)
    for h in range(1, RET_HEADS):
        y = y + jnp.where(head_v == h, y_stack[h * CHUNK:(h + 1) * CHUNK], 0.0)

    state = state_ref[...]
    y = y + _dot((rq * qdec_ref[...]).astype(BF16), state.astype(BF16))
    kd = (rk * kdec_ref[...]).astype(BF16)
    kv = lax.dot_general(kd, vb, (((0,), (0,)), ((), ())), preferred_element_type=F32)
    cd = cd_ref[...]
    state_ref[...] = state * cd + jnp.where(cd != 0.0, kv, 0.0)

    avg = avg_ref[...]
    mu = _dot_x_const(y, avg)
    d = y - mu
    var = _dot_x_const(d * d, avg)
    yn = d * lax.rsqrt(var + EPS)
    o_ref[...] = (yn * nw_ref[...] * _silu(g_ref[...].astype(F32))).astype(BF16)


def _retention_tables(seq):
    f = np.float32
    half = HEAD_DIM // 2
    inv = (1.0 / (f(ROPE_BASE) ** (np.arange(half, dtype=f) / f(half)))).astype(f)
    ang = np.arange(seq, dtype=f)[:, None] * inv[None, :]
    cos = np.tile(np.cos(ang).astype(f), (1, RET_HEADS))
    sin = np.tile(np.sin(ang).astype(f), (1, RET_HEADS))
    log_gamma = np.log(f(1.0) - f(2.0) ** (f(-5.0) - np.arange(RET_HEADS, dtype=f))).astype(f)
    idx = np.arange(CHUNK, dtype=f)
    diff = idx[:, None] - idx[None, :]
    scale = f(HEAD_DIM ** -0.5)
    intra = np.where(diff >= 0, np.exp(log_gamma[:, None, None] * np.maximum(diff, 0.0)), 0.0).astype(f)
    intra = (intra * scale).reshape(RET_HEADS * CHUNK, CHUNK)
    head_qk = (np.arange(RET_DIM) % LANES) // half
    head_v = np.arange(RET_DIM) // HEAD_DIM
    qdec = np.exp(log_gamma[head_qk][None, :] * (idx[:, None] + 1.0)).astype(f)
    kdec = (np.exp(log_gamma[head_qk][None, :] * (CHUNK - 1.0 - idx[:, None])) * scale).astype(f)
    chunk_decay = np.exp(log_gamma * f(CHUNK)).astype(f)
    same = head_qk[:, None] == head_v[None, :]
    cd = np.where(same, chunk_decay[head_qk][:, None], 0.0).astype(f)
    avg = np.where(head_v[:, None] == head_v[None, :], 1.0 / HEAD_DIM, 0.0).astype(f)
    return (jnp.asarray(cos), jnp.asarray(sin), jnp.asarray(intra), jnp.asarray(qdec), jnp.asarray(kdec),
            jnp.asarray(cd), jnp.asarray(avg, dtype=BF16))


def _retention(proj, tables, nw, bsz, seq):
    nc = seq // CHUNK
    cos, sin, intra, qdec, kdec, cd, avg = tables
    w = RET_DIM

    def col(j):
        return pl.BlockSpec((CHUNK, w), lambda b, c: (b * nc + c, j))

    def const(shape):
        return pl.BlockSpec(shape, lambda b, c: (0, 0))

    return pl.pallas_call(
        _retention_kernel,
        grid=(bsz, nc),
        in_specs=[
            col(OFF_RQ // w), col(OFF_RK // w), col(OFF_RV // w), col(OFF_RG // w),
            pl.BlockSpec((CHUNK, LANES), lambda b, c: (c, 0)),
            pl.BlockSpec((CHUNK, LANES), lambda b, c: (c, 0)),
            const((RET_HEADS * CHUNK, CHUNK)), const((CHUNK, w)), const((CHUNK, w)),
            const((w, w)), const((w, w)), const((1, w)),
        ],
        out_specs=pl.BlockSpec((CHUNK, w), lambda b, c: (b * nc + c, 0)),
        out_shape=jax.ShapeDtypeStruct((bsz * seq, w), BF16),
        scratch_shapes=[pltpu.VMEM((w, w), F32)],
        compiler_params=_cparams(("parallel", "arbitrary")),
        name="retention",
    )(proj, proj, proj, proj, cos, sin, intra, qdec, kdec, cd, avg, nw)


def _ssd_kernel(z_ref, xbc_ref, dt_ref, cw_ref, cb_ref, dtb_ref, alog_ref, dskip_ref, nw_ref,
                tri_ref, exp_ref, o_ref, prev_ref, state_ref):
    c = pl.program_id(1)

    @pl.when(c == 0)
    def _():
        prev_ref[...] = jnp.zeros_like(prev_ref)
        state_ref[...] = jnp.zeros_like(state_ref)

    cur = xbc_ref[...].astype(F32)
    prv = prev_ref[...]
    row = lax.broadcasted_iota(jnp.int32, (CHUNK, 1), 0)
    acc = cur * cw_ref[SSD_CONV - 1:SSD_CONV, :] + cb_ref[...]
    for j in range(1, SSD_CONV):
        shifted = jnp.where(row >= j, pltpu.roll(cur, j, 0), pltpu.roll(prv, j, 0))
        acc = acc + shifted * cw_ref[SSD_CONV - 1 - j:SSD_CONV - j, :]
    prev_ref[...] = cur
    xa = _silu(acc)
    x = xa[:, :SSD_INNER]
    bm = xa[:, SSD_INNER:SSD_INNER + SSD_BC]
    cm = xa[:, SSD_INNER + SSD_BC:]

    dt = _softplus(dt_ref[...] + dtb_ref[...])
    a = dt * (-jnp.exp(alog_ref[...]))
    tri = tri_ref[...]
    a_hi, a_mid, a_lo = _split3(a)
    a_cs = _dot(tri, a_hi) + _dot(tri, a_mid) + _dot(tri, a_lo)
    a_cs_t = a_cs.T
    stack = jnp.concatenate([dt, jnp.exp(a_cs), jnp.exp(a_cs[CHUNK - 1:CHUNK, :] - a_cs)], axis=0)
    stack_x = _dot_x_const(stack, exp_ref[...])
    dt_x = stack_x[:CHUNK]
    ea_x = stack_x[CHUNK:2 * CHUNK]
    dec_x = stack_x[2 * CHUNK:]

    xdt = x * dt_x
    xdt_b = xdt.astype(BF16)
    bm_b = bm.astype(BF16)
    cm_b = cm.astype(BF16)
    lane_bc = lax.broadcasted_iota(jnp.int32, (1, SSD_BC), 1)
    cb = [_dot_nt(jnp.where(lane_bc // SSD_STATE == g, cm, 0.0).astype(BF16), bm_b) for g in range(SSD_GROUPS)]

    li = lax.broadcasted_iota(jnp.int32, (CHUNK, CHUNK), 0)
    si = lax.broadcasted_iota(jnp.int32, (CHUNK, CHUNK), 1)
    lower = li >= si
    heads_per_group = SSD_HEADS // SSD_GROUPS
    m_list = []
    for h in range(SSD_HEADS):
        seg = a_cs[:, h:h + 1] - a_cs_t[h:h + 1, :]
        l_mat = jnp.exp(jnp.where(lower, seg, -jnp.inf))
        m_list.append((cb[h // heads_per_group] * l_mat).astype(BF16))
    y_stack = _dot(jnp.concatenate(m_list, axis=0), xdt_b)
    lane_x = lax.broadcasted_iota(jnp.int32, (1, SSD_INNER), 1)
    head_x = lane_x // HEAD_DIM
    y = jnp.where(head_x == 0, y_stack[:CHUNK], 0.0)
    for h in range(1, SSD_HEADS):
        y = y + jnp.where(head_x == h, y_stack[h * CHUNK:(h + 1) * CHUNK], 0.0)

    state = state_ref[...]
    y = y + _dot(cm_b, state.astype(BF16)) * ea_x
    contrib = _dot(bm.T.astype(BF16), (xdt * dec_x).astype(BF16))
    row_g = lax.broadcasted_iota(jnp.int32, (SSD_BC, 1), 0) // SSD_STATE
    same_group = row_g == lane_x // (SSD_INNER // SSD_GROUPS)
    state_ref[...] = state * ea_x[CHUNK - 1:CHUNK, :] + jnp.where(same_group, contrib, 0.0)

    y = y + dskip_ref[...] * x
    gy = y * _silu(z_ref[...].astype(F32))
    gw = SSD_INNER // SSD_GROUPS
    outs = []
    for g in range(SSD_GROUPS):
        part = gy[:, g * gw:(g + 1) * gw]
        outs.append(part * lax.rsqrt(jnp.mean(part * part, axis=-1, keepdims=True) + EPS))
    o_ref[...] = (jnp.concatenate(outs, axis=-1) * nw_ref[...]).astype(BF16)


def _ssd_tables():
    idx = np.arange(CHUNK)
    tri = (idx[:, None] >= idx[None, :]).astype(np.float32)
    expand = np.zeros((LANES, SSD_INNER), np.float32)
    for h in range(SSD_HEADS):
        expand[h, h * HEAD_DIM:(h + 1) * HEAD_DIM] = 1.0
    return jnp.asarray(tri, dtype=BF16), jnp.asarray(expand, dtype=BF16)


def _ssd(proj, dt_raw, conv_w, conv_b, dt_bias, a_log, d_skip, nw, tables, bsz, seq):
    nc = seq // CHUNK
    tri, expand = tables

    def const(shape):
        return pl.BlockSpec(shape, lambda b, c: (0, 0))

    return pl.pallas_call(
        _ssd_kernel,
        grid=(bsz, nc),
        in_specs=[
            pl.BlockSpec((CHUNK, SSD_INNER), lambda b, c: (b * nc + c, OFF_SZ // SSD_INNER)),
            pl.BlockSpec((CHUNK, SSD_CONV_DIM), lambda b, c: (b * nc + c, OFF_SXBC // SSD_CONV_DIM)),
            pl.BlockSpec((CHUNK, LANES), lambda b, c: (b * nc + c, 0)),
            const((SSD_CONV, SSD_CONV_DIM)), const((1, SSD_CONV_DIM)), const((1, LANES)), const((1, LANES)),
            const((1, SSD_INNER)), const((1, SSD_INNER)), const((CHUNK, CHUNK)), const((LANES, SSD_INNER)),
        ],
        out_specs=pl.BlockSpec((CHUNK, SSD_INNER), lambda b, c: (b * nc + c, 0)),
        out_shape=jax.ShapeDtypeStruct((bsz * seq, SSD_INNER), BF16),
        scratch_shapes=[pltpu.VMEM((CHUNK, SSD_CONV_DIM), F32), pltpu.VMEM((SSD_BC, SSD_INNER), F32)],
        compiler_params=_cparams(("parallel", "arbitrary")),
        name="ssd",
    )(proj, proj, dt_raw, conv_w, conv_b, dt_bias, a_log, d_skip, nw, tri, expand)


SB_BLOCK = 256


def _stickbreak_kernel(q_ref, k_ref, v_ref, u_ref, o_ref, acc_ref):
    i = pl.program_id(1)
    blk = SB_BLOCK
    q = q_ref[...] * jnp.asarray(HEAD_DIM ** -0.5, BF16)
    lane = lax.broadcasted_iota(jnp.int32, (1, SB_DIM), 1)
    head = lane // HEAD_DIM
    q_heads = [jnp.where(head == h, q, jnp.zeros_like(q)) for h in range(SB_HEADS)]
    u = u_ref[...]
    strict_lower = (lax.broadcasted_iota(jnp.int32, (blk, blk), 1)
                    < lax.broadcasted_iota(jnp.int32, (blk, blk), 0))

    def block(j, carries, diagonal):
        start = pl.multiple_of(j * blk, blk)
        kb = k_ref[pl.ds(start, blk), :]
        vb = v_ref[pl.ds(start, blk), :]
        new = []
        for h in range(SB_HEADS):
            z = _dot_nt(q_heads[h], kb)
            sp = _softplus(z)
            if diagonal:
                sp = jnp.where(strict_lower, sp, 0.0)
            cs = _dot(sp.astype(BF16), u)
            w = jnp.exp(z - cs - carries[h])
            if diagonal:
                w = jnp.where(strict_lower, w, 0.0)
            pv = _dot(w.astype(BF16), vb)
            if diagonal:
                acc_ref[h] = pv
            else:
                acc_ref[h] += pv
            new.append(carries[h] + jnp.sum(sp, axis=-1, keepdims=True))
        return tuple(new)

    zero = jnp.zeros((blk, 1), F32)
    carries = block(i, (zero,) * SB_HEADS, True)
    lax.fori_loop(0, i, lambda jj, c: block(i - 1 - jj, c, False), carries)
    out = jnp.where(head == 0, acc_ref[0], 0.0)
    for h in range(1, SB_HEADS):
        out = out + jnp.where(head == h, acc_ref[h], 0.0)
    o_ref[...] = out.astype(BF16)


def _stickbreak_tables():
    idx = np.arange(SB_BLOCK)
    rev = (idx[:, None] >= idx[None, :]).astype(np.float32)
    return jnp.asarray(rev, dtype=BF16)


def _stickbreak(proj, u, bsz, seq):
    nq = seq // SB_BLOCK
    w = SB_DIM
    return pl.pallas_call(
        _stickbreak_kernel,
        grid=(bsz, nq),
        in_specs=[
            pl.BlockSpec((SB_BLOCK, w), lambda b, i: (b * nq + i, OFF_BQ // w)),
            pl.BlockSpec((seq, w), lambda b, i: (b, OFF_BK // w)),
            pl.BlockSpec((seq, w), lambda b, i: (b, OFF_BV // w)),
            pl.BlockSpec((SB_BLOCK, SB_BLOCK), lambda b, i: (0, 0)),
        ],
        out_specs=pl.BlockSpec((SB_BLOCK, w), lambda b, i: (b * nq + i, 0)),
        out_shape=jax.ShapeDtypeStruct((bsz * seq, w), BF16),
        scratch_shapes=[pltpu.VMEM((SB_HEADS, SB_BLOCK, w), F32)],
        compiler_params=_cparams(("parallel", "arbitrary")),
        name="stickbreak",
    )(proj, proj, proj, u)


def _outproj_kernel(x_ref, ro_ref, so_ref, bo_ref, w_ref, o_ref, wb_ref):
    @pl.when(pl.program_id(0) == 0)
    def _():
        wb_ref[...] = w_ref[0].astype(BF16)

    acc = _dot(ro_ref[...], wb_ref[:RET_DIM, :])
    acc = acc + _dot(so_ref[...], wb_ref[RET_DIM:RET_DIM + SSD_INNER, :])
    acc = acc + _dot(bo_ref[...], wb_ref[RET_DIM + SSD_INNER:, :])
    o_ref[...] = x_ref[...] + acc


def _outproj(x, ro, so, bo, w, layer, tm=512):
    t = x.shape[0]
    return pl.pallas_call(
        _outproj_kernel,
        grid=(t // tm,),
        in_specs=[
            pl.BlockSpec((tm, D_MODEL), lambda m: (m, 0)),
            pl.BlockSpec((tm, RET_DIM), lambda m: (m, 0)),
            pl.BlockSpec((tm, SSD_INNER), lambda m: (m, 0)),
            pl.BlockSpec((tm, SB_DIM), lambda m: (m, 0)),
            pl.BlockSpec((1, D_MIX, D_MODEL), lambda m: (layer, 0, 0)),
        ],
        out_specs=pl.BlockSpec((tm, D_MODEL), lambda m: (m, 0)),
        out_shape=jax.ShapeDtypeStruct((t, D_MODEL), F32),
        scratch_shapes=[pltpu.VMEM((D_MIX, D_MODEL), BF16)],
        compiler_params=_cparams(("arbitrary",)),
        name="outproj",
    )(x, ro, so, bo, w)


def _ffn_kernel(x_ref, nw_ref, wg_ref, wu_ref, wd_ref, o_ref, h_ref, acc_ref):
    f = pl.program_id(1)

    @pl.when(f == 0)
    def _():
        h_ref[...] = _rms_h(x_ref[...], nw_ref[...]).astype(BF16)
        acc_ref[...] = jnp.zeros_like(acc_ref)

    h = h_ref[...]
    act = _silu(_dot(h, wg_ref[0].astype(BF16))) * _dot(h, wu_ref[0].astype(BF16))
    acc_ref[...] += _dot(act.astype(BF16), wd_ref[0].astype(BF16))

    @pl.when(f == pl.num_programs(1) - 1)
    def _():
        o_ref[...] = x_ref[...] + acc_ref[...]


def _ffn(x, nw, wg, wu, wd, layer, tm=1024, tf=512):
    t = x.shape[0]
    return pl.pallas_call(
        _ffn_kernel,
        grid=(t // tm, D_FF // tf),
        in_specs=[
            pl.BlockSpec((tm, D_MODEL), lambda m, f: (m, 0)),
            pl.BlockSpec((1, D_MODEL), lambda m, f: (0, 0)),
            pl.BlockSpec((1, D_MODEL, tf), lambda m, f: (layer, 0, f)),
            pl.BlockSpec((1, D_MODEL, tf), lambda m, f: (layer, 0, f)),
            pl.BlockSpec((1, tf, D_MODEL), lambda m, f: (layer, f, 0)),
        ],
        out_specs=pl.BlockSpec((tm, D_MODEL), lambda m, f: (m, 0)),
        out_shape=jax.ShapeDtypeStruct((t, D_MODEL), F32),
        scratch_shapes=[pltpu.VMEM((tm, D_MODEL), BF16), pltpu.VMEM((tm, D_MODEL), F32)],
        compiler_params=_cparams(("parallel", "arbitrary")),
        name="ffn",
    )(x, nw, wg, wu, wd)


def _router_kernel(x_ref, nw_ref, rw_ref, idx_ref, gate_ref):
    h = _rms_h(x_ref[...], nw_ref[...])
    h_hi, h_lo = _split2(h)
    w_hi, w_lo = _split2(rw_ref[...])
    logits = _dot(h_hi, w_hi) + _dot(h_hi, w_lo) + _dot(h_lo, w_hi)
    lane = lax.broadcasted_iota(jnp.int32, (1, LANES), 1)
    lg = jnp.where(lane < N_EXPERTS, logits, -jnp.inf)
    m1 = jnp.max(lg, axis=-1, keepdims=True)
    i1 = jnp.min(jnp.where(lg == m1, lane, LANES), axis=-1, keepdims=True)
    lg2 = jnp.where(lane == i1, -jnp.inf, lg)
    m2 = jnp.max(lg2, axis=-1, keepdims=True)
    i2 = jnp.min(jnp.where(lg2 == m2, lane, LANES), axis=-1, keepdims=True)
    e = jnp.exp(m2 - m1)
    g1 = 1.0 / (1.0 + e)
    g2 = e / (1.0 + e)
    idx_ref[...] = jnp.where(lane == 0, i1, jnp.where(lane == 1, i2, 0))
    gate_ref[...] = jnp.where(lane == 0, g1, jnp.where(lane == 1, g2, 0.0))


def _router(x, nw, rw, tm=512):
    t = x.shape[0]
    return pl.pallas_call(
        _router_kernel,
        grid=(t // tm,),
        in_specs=[
            pl.BlockSpec((tm, D_MODEL), lambda m: (m, 0)),
            pl.BlockSpec((1, D_MODEL), lambda m: (0, 0)),
            pl.BlockSpec((D_MODEL, LANES), lambda m: (0, 0)),
        ],
        out_specs=[
            pl.BlockSpec((tm, LANES), lambda m: (m, 0)),
            pl.BlockSpec((tm, LANES), lambda m: (m, 0)),
        ],
        out_shape=[jax.ShapeDtypeStruct((t, LANES), jnp.int32), jax.ShapeDtypeStruct((t, LANES), F32)],
        compiler_params=_cparams(("parallel",)),
        name="router",
    )(x, nw, rw)


MOE_TM = 512


def _moe_plan(e2):
    t = e2.shape[0]
    p = 2 * t
    ef = e2.reshape(p)
    onehot = (ef[:, None] == jnp.arange(N_EXPERTS, dtype=jnp.int32)[None, :]).astype(jnp.int32)
    csum = jnp.cumsum(onehot, axis=0)
    counts = csum[-1]
    rank = jnp.sum((csum - onehot) * onehot, axis=1)
    padded = ((counts + MOE_TM - 1) // MOE_TM) * MOE_TM
    ends = jnp.cumsum(padded)
    starts = ends - padded
    dst = jnp.sum(onehot * starts[None, :], axis=1) + rank
    n_tiles = p // MOE_TM + N_EXPERTS
    tile_start = jnp.arange(n_tiles, dtype=jnp.int32) * MOE_TM
    tile_expert = jnp.sum((tile_start[:, None] >= ends[None, :]).astype(jnp.int32), axis=1)
    tile_expert = jnp.minimum(tile_expert, N_EXPERTS - 1)
    n_used = (ends[-1] // MOE_TM).reshape(1).astype(jnp.int32)
    src = jnp.zeros((n_tiles * MOE_TM,), jnp.int32).at[dst].set(jnp.arange(p, dtype=jnp.int32) // 2)
    return dst.reshape(t, 2), src, tile_expert.astype(jnp.int32), n_used


def _expert_kernel(te_ref, nu_ref, src_ref, src_next_ref, x_hbm, nw_ref, wg_ref, wu_ref, wd_ref, o_ref,
                   rows_ref, sem, h_ref, acc_ref):
    i = pl.program_id(0)
    f = pl.program_id(1)
    n_used = nu_ref[0]
    slot = i % 2

    def gather_copy(row, r, s):
        return pltpu.make_async_copy(x_hbm.at[pl.ds(row, 1)], rows_ref.at[s, pl.ds(r, 1)], sem.at[s])

    def start_gather(idx_ref, s):
        def body(r, carry):
            gather_copy(idx_ref[0, 0, r], r, s).start()
            return carry
        lax.fori_loop(0, MOE_TM, body, 0, unroll=8)

    @pl.when(f == 0)
    def _():
        acc_ref[...] = jnp.zeros_like(acc_ref)

    @pl.when((f == 0) & (i < n_used))
    def _():
        @pl.when(i == 0)
        def _():
            start_gather(src_ref, 0)

        @pl.when(i + 1 < n_used)
        def _():
            start_gather(src_next_ref, 1 - slot)

        pltpu.make_async_copy(x_hbm.at[pl.ds(0, MOE_TM)], rows_ref.at[slot], sem.at[slot]).wait()
        h_ref[...] = _rms_h(rows_ref[slot], nw_ref[...]).astype(BF16)

    @pl.when(i < n_used)
    def _():
        h = h_ref[...]
        act = _silu(_dot(h, wg_ref[0, 0].astype(BF16))) * _dot(h, wu_ref[0, 0].astype(BF16))
        acc_ref[...] += _dot(act.astype(BF16), wd_ref[0, 0].astype(BF16))

    @pl.when(f == pl.num_programs(1) - 1)
    def _():
        o_ref[...] = acc_ref[...].astype(BF16)


def _experts(x, nw, src, tile_expert, n_used, wg, wu, wd, layer, tf=512):
    n_tiles = tile_expert.shape[0]
    nf = D_FF // tf
    src2 = src.reshape(n_tiles, 1, MOE_TM)

    def fblock(i, f, nu):
        return jnp.where(i < nu[0], f, nf - 1)

    smem_row = functools.partial(pl.BlockSpec, (1, 1, MOE_TM), memory_space=pltpu.SMEM)
    grid_spec = pltpu.PrefetchScalarGridSpec(
        num_scalar_prefetch=2,
        grid=(n_tiles, nf),
        in_specs=[
            smem_row(lambda i, f, te, nu: (i, 0, 0)),
            smem_row(lambda i, f, te, nu: (jnp.minimum(i + 1, n_tiles - 1), 0, 0)),
            pl.BlockSpec(memory_space=pl.ANY),
            pl.BlockSpec((1, D_MODEL), lambda i, f, te, nu: (0, 0)),
            pl.BlockSpec((1, 1, D_MODEL, tf), lambda i, f, te, nu: (layer, te[i], 0, fblock(i, f, nu))),
            pl.BlockSpec((1, 1, D_MODEL, tf), lambda i, f, te, nu: (layer, te[i], 0, fblock(i, f, nu))),
            pl.BlockSpec((1, 1, tf, D_MODEL), lambda i, f, te, nu: (layer, te[i], fblock(i, f, nu), 0)),
        ],
        out_specs=pl.BlockSpec((MOE_TM, D_MODEL), lambda i, f, te, nu: (i, 0)),
        scratch_shapes=[
            pltpu.VMEM((2, MOE_TM, D_MODEL), F32),
            pltpu.SemaphoreType.DMA((2,)),
            pltpu.VMEM((MOE_TM, D_MODEL), BF16),
            pltpu.VMEM((MOE_TM, D_MODEL), F32),
        ],
    )
    return pl.pallas_call(
        _expert_kernel,
        grid_spec=grid_spec,
        out_shape=jax.ShapeDtypeStruct((n_tiles * MOE_TM, D_MODEL), BF16),
        compiler_params=_cparams(("arbitrary", "arbitrary")),
        name="experts",
    )(tile_expert, n_used, src2, src2, x, nw, wg, wu, wd)


def _combine_kernel(x_ref, y1_ref, y2_ref, gate_ref, nw_ref, o_ref, *, final_norm):
    g = gate_ref[...]
    out = x_ref[...] + g[:, 0:1] * y1_ref[...].astype(F32) + g[:, 1:2] * y2_ref[...].astype(F32)
    if final_norm:
        out = _rms_h(out, nw_ref[...])
    o_ref[...] = out


def _combine(x, y1, y2, gates, nw, final_norm, tm=1024):
    t = x.shape[0]
    return pl.pallas_call(
        functools.partial(_combine_kernel, final_norm=final_norm),
        grid=(t // tm,),
        in_specs=[
            pl.BlockSpec((tm, D_MODEL), lambda m: (m, 0)),
            pl.BlockSpec((tm, D_MODEL), lambda m: (m, 0)),
            pl.BlockSpec((tm, D_MODEL), lambda m: (m, 0)),
            pl.BlockSpec((tm, LANES), lambda m: (m, 0)),
            pl.BlockSpec((1, D_MODEL), lambda m: (0, 0)),
        ],
        out_specs=pl.BlockSpec((tm, D_MODEL), lambda m: (m, 0)),
        out_shape=jax.ShapeDtypeStruct((t, D_MODEL), F32),
        compiler_params=_cparams(("parallel",)),
        name="combine",
    )(x, y1, y2, gates, nw)


def _moe_layer(xf, norm_w, router_w, wg, wu, wd, layer, final_w=None):
    rw = jnp.pad(router_w.astype(F32), ((0, 0), (0, LANES - N_EXPERTS)))
    idx, gates = _router(xf, _row(norm_w), rw)
    dst, src, tile_expert, n_used = _moe_plan(idx[:, :2])
    ys = _experts(xf, _row(norm_w), src, tile_expert, n_used, wg, wu, wd, layer)
    y1 = ys.at[dst[:, 0]].get(mode="promise_in_bounds")
    y2 = ys.at[dst[:, 1]].get(mode="promise_in_bounds")
    nw = _row(final_w) if final_w is not None else jnp.ones((1, D_MODEL), F32)
    return _combine(xf, y1, y2, gates, nw, final_w is not None)


def _final_norm_kernel(x_ref, nw_ref, o_ref):
    o_ref[...] = _rms_h(x_ref[...], nw_ref[...])


def _final_norm(x, nw, tm=1024):
    t = x.shape[0]
    return pl.pallas_call(
        _final_norm_kernel,
        grid=(t // tm,),
        in_specs=[pl.BlockSpec((tm, D_MODEL), lambda m: (m, 0)), pl.BlockSpec((1, D_MODEL), lambda m: (0, 0))],
        out_specs=pl.BlockSpec((tm, D_MODEL), lambda m: (m, 0)),
        out_shape=jax.ShapeDtypeStruct((t, D_MODEL), F32),
        compiler_params=_cparams(("parallel",)),
        name="final_norm",
    )(x, nw)


def _rotary_perm():
    half = HEAD_DIM // 2
    first = [h * HEAD_DIM + d for h in range(RET_HEADS) for d in range(half)]
    second = [h * HEAD_DIM + half + d for h in range(RET_HEADS) for d in range(half)]
    return np.asarray(first + second, np.int32)


def _layout_w_in(w_in):
    perm = _rotary_perm()
    dt0 = 4 * RET_DIM + SSD_INNER + SSD_CONV_DIM
    main = jnp.concatenate([
        w_in[:, perm], w_in[:, RET_DIM + perm], w_in[:, 2 * RET_DIM:dt0], w_in[:, dt0 + SSD_HEADS:],
    ], axis=1).astype(BF16)
    wdt = jnp.pad(w_in[:, dt0:dt0 + SSD_HEADS], ((0, 0), (0, LANES - SSD_HEADS))).astype(BF16)
    return main, wdt


def _row(v, width=None):
    v = v.reshape(1, -1).astype(F32)
    if width is not None and v.shape[1] < width:
        v = jnp.pad(v, ((0, 0), (0, width - v.shape[1])))
    return v


def kernel(x, norm1_w, w_in, ret_norm_w, ssd_conv_w, ssd_conv_b, ssd_dt_bias, ssd_a_log, ssd_d, ssd_norm_w, w_out, norm2_w, ffn_w_gate, ffn_w_up, ffn_w_down, moe_router, moe_w_gate, moe_w_up, moe_w_down, final_norm_w):
    bsz, seq, _ = x.shape
    depth = w_in.shape[0]
    ret_tables = _retention_tables(seq)
    ssd_tables = _ssd_tables()
    sb_u = _stickbreak_tables()

    xf = x.reshape(bsz * seq, D_MODEL)
    for layer in range(depth):
        w_main, w_dt = _layout_w_in(w_in[layer])
        proj, dt_raw = _inproj(xf, _row(norm1_w[layer]), w_main, w_dt)
        ro = _retention(proj, ret_tables, _row(ret_norm_w[layer]), bsz, seq)
        so = _ssd(proj, dt_raw, ssd_conv_w[layer].astype(F32), _row(ssd_conv_b[layer]),
                  _row(ssd_dt_bias[layer], LANES), _row(ssd_a_log[layer], LANES),
                  _row(jnp.repeat(ssd_d[layer], HEAD_DIM)), _row(ssd_norm_w[layer]), ssd_tables, bsz, seq)
        bo = _stickbreak(proj, sb_u, bsz, seq)
        xf = _outproj(xf, ro, so, bo, w_out, layer)
        j = layer // 2
        if layer % 2 == 0:
            xf = _ffn(xf, _row(norm2_w[layer]), ffn_w_gate, ffn_w_up, ffn_w_down, j)
        else:
            last = layer == depth - 1
            xf = _moe_layer(xf, norm2_w[layer], moe_router[j], moe_w_gate, moe_w_up, moe_w_down, j,
                            final_norm_w if last else None)
    if depth % 2 == 1:
        xf = _final_norm(xf, _row(final_norm_w))
    return xf.reshape(bsz, seq, D_MODEL)
```

```python
import functools
import math

import numpy as np
import jax
import jax.numpy as jnp
from jax import lax
from jax.experimental import pallas as pl
from jax.experimental.pallas import tpu as pltpu

F32 = jnp.float32
BF16 = jnp.bfloat16

D_MODEL = 1024
HEAD_DIM = 64
RET_HEADS = 4
RET_DIM = RET_HEADS * HEAD_DIM
SSD_HEADS = 8
SSD_INNER = SSD_HEADS * HEAD_DIM
SSD_GROUPS = 2
SSD_STATE = 64
SSD_CONV = 4
SSD_BC = SSD_GROUPS * SSD_STATE
SSD_CONV_DIM = SSD_INNER + 2 * SSD_BC
SB_HEADS = 4
SB_DIM = SB_HEADS * HEAD_DIM
D_MIX = RET_DIM + SSD_INNER + SB_DIM
CHUNK = 128
D_FF = 3584
N_EXPERTS = 8
ROPE_BASE = 10000.0
EPS = 1e-6

LANES = 128
SUBLANES = 8
D_PROJ = 3072
OFF_RQ, OFF_RK, OFF_RV, OFF_RG = 0, 256, 512, 768
OFF_SZ, OFF_SXBC = 1024, 1536
OFF_BQ, OFF_BK, OFF_BV = 2304, 2560, 2816

VMEM_LIMIT = 56 * 1024 * 1024


def _cparams(sem):
    return pltpu.CompilerParams(dimension_semantics=sem, vmem_limit_bytes=VMEM_LIMIT)


def _dot(a, b):
    return jnp.dot(a, b, preferred_element_type=F32)


def _dot_nt(a, b):
    return lax.dot_general(a, b, (((1,), (1,)), ((), ())), preferred_element_type=F32)


def _split2(x):
    hi = x.astype(BF16)
    lo = (x - hi.astype(F32)).astype(BF16)
    return hi, lo


def _split3(x):
    hi = x.astype(BF16)
    r = x - hi.astype(F32)
    mid = r.astype(BF16)
    lo = (r - mid.astype(F32)).astype(BF16)
    return hi, mid, lo


def _dot_x_const(x, c):
    hi, lo = _split2(x)
    return _dot(hi, c) + _dot(lo, c)


def _silu(x):
    return x * jax.nn.sigmoid(x)


def _softplus(x):
    e = jnp.exp2(jnp.abs(x) * (-math.log2(math.e)))
    return jnp.maximum(x, 0.0) + jnp.log(1.0 + e)


def _pair_heads_matmul(m_stack, x, n_heads):
    rows = m_stack.shape[0] // n_heads
    first_of_pair = lax.broadcasted_iota(jnp.int32, (1, LANES), 1) < HEAD_DIM
    parts = []
    for p in range(n_heads // 2):
        both = _dot(m_stack[2 * p * rows:(2 * p + 2) * rows], x[:, p * LANES:(p + 1) * LANES])
        parts.append(jnp.where(first_of_pair, both[:rows], both[rows:]))
    return jnp.concatenate(parts, axis=-1)


def _rms_h(x, w):
    ms = jnp.mean(x * x, axis=-1, keepdims=True)
    return x * lax.rsqrt(ms + EPS) * w


def _inproj_kernel(x_ref, nw_ref, w_ref, wdt_ref, o_ref, dt_ref):
    h = _rms_h(x_ref[...], nw_ref[...]).astype(BF16)
    step = 512
    for c in range(D_PROJ // step):
        o_ref[:, c * step:(c + 1) * step] = _dot(h, w_ref[:, c * step:(c + 1) * step]).astype(BF16)
    dt_ref[...] = _dot(h, wdt_ref[...])


def _inproj(x, nw, w, wdt, tm=512):
    t = x.shape[0]
    return pl.pallas_call(
        _inproj_kernel,
        grid=(t // tm,),
        in_specs=[
            pl.BlockSpec((tm, D_MODEL), lambda m: (m, 0)),
            pl.BlockSpec((1, D_MODEL), lambda m: (0, 0)),
            pl.BlockSpec((D_MODEL, D_PROJ), lambda m: (0, 0)),
            pl.BlockSpec((D_MODEL, LANES), lambda m: (0, 0)),
        ],
        out_specs=[
            pl.BlockSpec((tm, D_PROJ), lambda m: (m, 0)),
            pl.BlockSpec((tm, LANES), lambda m: (m, 0)),
        ],
        out_shape=[jax.ShapeDtypeStruct((t, D_PROJ), BF16), jax.ShapeDtypeStruct((t, LANES), F32)],
        compiler_params=_cparams(("parallel",)),
        name="inproj",
    )(x, nw, w, wdt)


def _retention_kernel(q_ref, k_ref, v_ref, g_ref, cos_ref, sin_ref, intra_ref, qdec_ref, kdec_ref,
                      cd_ref, avg_ref, nw_ref, o_ref, state_ref):
    c = pl.program_id(1)

    @pl.when(c == 0)
    def _():
        state_ref[...] = jnp.zeros_like(state_ref)

    cos = cos_ref[...]
    sin = sin_ref[...]

    def rot(t):
        t1, t2 = t[:, :LANES], t[:, LANES:]
        return jnp.concatenate([t1 * cos - t2 * sin, t1 * sin + t2 * cos], axis=-1)

    lane = lax.broadcasted_iota(jnp.int32, (1, RET_DIM), 1)
    head_qk = (lane % LANES) // (HEAD_DIM // 2)
    cd = cd_ref[...]
    avg = avg_ref[...]

    for b in range(q_ref.shape[0]):
        rq = rot(q_ref[b].astype(F32))
        rk = rot(k_ref[b].astype(F32))
        vb = v_ref[b]

        q_stack = jnp.concatenate([jnp.where(head_qk == h, rq, 0.0) for h in range(RET_HEADS)], axis=0)
        scores = _dot_nt(q_stack.astype(BF16), rk.astype(BF16)) * intra_ref[...]
        y = _pair_heads_matmul(scores.astype(BF16), vb, RET_HEADS)

        state = state_ref[b]
        y = y + _dot((rq * qdec_ref[...]).astype(BF16), state.astype(BF16))
        kd = (rk * kdec_ref[...]).astype(BF16)
        kv = lax.dot_general(kd, vb, (((0,), (0,)), ((), ())), preferred_element_type=F32)
        state_ref[b] = state * cd + jnp.where(cd != 0.0, kv, 0.0)

        mu = _dot_x_const(y, avg)
        d = y - mu
        var = _dot_x_const(d * d, avg)
        yn = d * lax.rsqrt(var + EPS)
        o_ref[b] = (yn * nw_ref[...] * _silu(g_ref[b].astype(F32))).astype(BF16)


def _retention_tables(seq):
    f = np.float32
    half = HEAD_DIM // 2
    inv = (1.0 / (f(ROPE_BASE) ** (np.arange(half, dtype=f) / f(half)))).astype(f)
    ang = np.arange(seq, dtype=f)[:, None] * inv[None, :]
    cos = np.tile(np.cos(ang).astype(f), (1, RET_HEADS))
    sin = np.tile(np.sin(ang).astype(f), (1, RET_HEADS))
    log_gamma = np.log(f(1.0) - f(2.0) ** (f(-5.0) - np.arange(RET_HEADS, dtype=f))).astype(f)
    idx = np.arange(CHUNK, dtype=f)
    diff = idx[:, None] - idx[None, :]
    scale = f(HEAD_DIM ** -0.5)
    intra = np.where(diff >= 0, np.exp(log_gamma[:, None, None] * np.maximum(diff, 0.0)), 0.0).astype(f)
    intra = (intra * scale).reshape(RET_HEADS * CHUNK, CHUNK)
    head_qk = (np.arange(RET_DIM) % LANES) // half
    head_v = np.arange(RET_DIM) // HEAD_DIM
    qdec = np.exp(log_gamma[head_qk][None, :] * (idx[:, None] + 1.0)).astype(f)
    kdec = (np.exp(log_gamma[head_qk][None, :] * (CHUNK - 1.0 - idx[:, None])) * scale).astype(f)
    chunk_decay = np.exp(log_gamma * f(CHUNK)).astype(f)
    same = head_qk[:, None] == head_v[None, :]
    cd = np.where(same, chunk_decay[head_qk][:, None], 0.0).astype(f)
    avg = np.where(head_v[:, None] == head_v[None, :], 1.0 / HEAD_DIM, 0.0).astype(f)
    return (jnp.asarray(cos), jnp.asarray(sin), jnp.asarray(intra), jnp.asarray(qdec), jnp.asarray(kdec),
            jnp.asarray(cd), jnp.asarray(avg, dtype=BF16))


MIXER_GROUP = 4


def _retention(proj, tables, nw, bsz, seq):
    nc = seq // CHUNK
    cos, sin, intra, qdec, kdec, cd, avg = tables
    w = RET_DIM
    grp = MIXER_GROUP if bsz % MIXER_GROUP == 0 else 1
    proj3 = proj.reshape(bsz, seq, D_PROJ)

    def col(j):
        return pl.BlockSpec((grp, CHUNK, w), lambda b, c: (b, c, j))

    def const(shape):
        return pl.BlockSpec(shape, lambda b, c: (0, 0))

    out = pl.pallas_call(
        _retention_kernel,
        grid=(bsz // grp, nc),
        in_specs=[
            col(OFF_RQ // w), col(OFF_RK // w), col(OFF_RV // w), col(OFF_RG // w),
            pl.BlockSpec((CHUNK, LANES), lambda b, c: (c, 0)),
            pl.BlockSpec((CHUNK, LANES), lambda b, c: (c, 0)),
            const((RET_HEADS * CHUNK, CHUNK)), const((CHUNK, w)), const((CHUNK, w)),
            const((w, w)), const((w, w)), const((1, w)),
        ],
        out_specs=pl.BlockSpec((grp, CHUNK, w), lambda b, c: (b, c, 0)),
        out_shape=jax.ShapeDtypeStruct((bsz, seq, w), BF16),
        scratch_shapes=[pltpu.VMEM((grp, w, w), F32)],
        compiler_params=_cparams(("parallel", "arbitrary")),
        name="retention",
    )(proj3, proj3, proj3, proj3, cos, sin, intra, qdec, kdec, cd, avg, nw)
    return out.reshape(bsz * seq, w)


def _ssd_kernel(z_ref, xbc_ref, dt_ref, cw_ref, cb_ref, dtb_ref, alog_ref, dskip_ref, nw_ref,
                tri_ref, exp_ref, o_ref, prev_ref, state_ref):
    c = pl.program_id(1)

    @pl.when(c == 0)
    def _():
        prev_ref[...] = jnp.zeros_like(prev_ref)
        state_ref[...] = jnp.zeros_like(state_ref)

    row = lax.broadcasted_iota(jnp.int32, (SUBLANES, 1), 0)
    tri = tri_ref[...]
    lane_bc = lax.broadcasted_iota(jnp.int32, (1, SSD_BC), 1)
    lower = (lax.broadcasted_iota(jnp.int32, (CHUNK, CHUNK), 0)
             >= lax.broadcasted_iota(jnp.int32, (CHUNK, CHUNK), 1))
    heads_per_group = SSD_HEADS // SSD_GROUPS
    lane_x = lax.broadcasted_iota(jnp.int32, (1, SSD_INNER), 1)
    row_g = lax.broadcasted_iota(jnp.int32, (SSD_BC, 1), 0) // SSD_STATE
    same_group = row_g == lane_x // (SSD_INNER // SSD_GROUPS)
    gw = SSD_INNER // SSD_GROUPS

    for b in range(z_ref.shape[0]):
        cur = xbc_ref[b].astype(F32)
        tail = prev_ref[b]
        acc = cur * cw_ref[SSD_CONV - 1:SSD_CONV, :] + cb_ref[...]
        for j in range(1, SSD_CONV):
            rolled = pltpu.roll(cur, j, 0)
            top = jnp.where(row >= j, rolled[:SUBLANES], pltpu.roll(tail, j, 0))
            shifted = jnp.concatenate([top, rolled[SUBLANES:]], axis=0)
            acc = acc + shifted * cw_ref[SSD_CONV - 1 - j:SSD_CONV - j, :]
        prev_ref[b] = cur[CHUNK - SUBLANES:]
        xa = _silu(acc)
        x = xa[:, :SSD_INNER]
        bm = xa[:, SSD_INNER:SSD_INNER + SSD_BC]
        cm = xa[:, SSD_INNER + SSD_BC:]

        dt = _softplus(dt_ref[b] + dtb_ref[...])
        a = dt * (-jnp.exp(alog_ref[...]))
        a_hi, a_mid, a_lo = _split3(a)
        a_cs = _dot(tri, a_hi) + _dot(tri, a_mid) + _dot(tri, a_lo)
        a_cs_t = a_cs.T
        stack = jnp.concatenate([dt, jnp.exp(a_cs), jnp.exp(a_cs[CHUNK - 1:CHUNK, :] - a_cs)], axis=0)
        stack_x = _dot_x_const(stack, exp_ref[...])
        dt_x = stack_x[:CHUNK]
        ea_x = stack_x[CHUNK:2 * CHUNK]
        dec_x = stack_x[2 * CHUNK:]

        xdt = x * dt_x
        xdt_b = xdt.astype(BF16)
        bm_b = bm.astype(BF16)
        cm_b = cm.astype(BF16)
        cb = [_dot_nt(jnp.where(lane_bc // SSD_STATE == g, cm, 0.0).astype(BF16), bm_b)
              for g in range(SSD_GROUPS)]

        m_list = []
        for h in range(SSD_HEADS):
            seg = a_cs[:, h:h + 1] - a_cs_t[h:h + 1, :]
            l_mat = jnp.exp(jnp.where(lower, seg, -jnp.inf))
            m_list.append((cb[h // heads_per_group] * l_mat).astype(BF16))
        y = _pair_heads_matmul(jnp.concatenate(m_list, axis=0), xdt_b, SSD_HEADS)

        state = state_ref[b]
        y = y + _dot(cm_b, state.astype(BF16)) * ea_x
        contrib = _dot(bm.T.astype(BF16), (xdt * dec_x).astype(BF16))
        state_ref[b] = state * ea_x[CHUNK - 1:CHUNK, :] + jnp.where(same_group, contrib, 0.0)

        y = y + dskip_ref[...] * x
        gy = y * _silu(z_ref[b].astype(F32))
        outs = []
        for g in range(SSD_GROUPS):
            part = gy[:, g * gw:(g + 1) * gw]
            outs.append(part * lax.rsqrt(jnp.mean(part * part, axis=-1, keepdims=True) + EPS))
        o_ref[b] = (jnp.concatenate(outs, axis=-1) * nw_ref[...]).astype(BF16)


def _ssd_tables():
    idx = np.arange(CHUNK)
    tri = (idx[:, None] >= idx[None, :]).astype(np.float32)
    expand = np.zeros((LANES, SSD_INNER), np.float32)
    for h in range(SSD_HEADS):
        expand[h, h * HEAD_DIM:(h + 1) * HEAD_DIM] = 1.0
    return jnp.asarray(tri, dtype=BF16), jnp.asarray(expand, dtype=BF16)


def _ssd(proj, dt_raw, conv_w, conv_b, dt_bias, a_log, d_skip, nw, tables, bsz, seq):
    nc = seq // CHUNK
    tri, expand = tables

    grp = MIXER_GROUP if bsz % MIXER_GROUP == 0 else 1
    proj3 = proj.reshape(bsz, seq, D_PROJ)
    dt3 = dt_raw.reshape(bsz, seq, LANES)

    def const(shape):
        return pl.BlockSpec(shape, lambda b, c: (0, 0))

    out = pl.pallas_call(
        _ssd_kernel,
        grid=(bsz // grp, nc),
        in_specs=[
            pl.BlockSpec((grp, CHUNK, SSD_INNER), lambda b, c: (b, c, OFF_SZ // SSD_INNER)),
            pl.BlockSpec((grp, CHUNK, SSD_CONV_DIM), lambda b, c: (b, c, OFF_SXBC // SSD_CONV_DIM)),
            pl.BlockSpec((grp, CHUNK, LANES), lambda b, c: (b, c, 0)),
            const((SSD_CONV, SSD_CONV_DIM)), const((1, SSD_CONV_DIM)), const((1, LANES)), const((1, LANES)),
            const((1, SSD_INNER)), const((1, SSD_INNER)), const((CHUNK, CHUNK)), const((LANES, SSD_INNER)),
        ],
        out_specs=pl.BlockSpec((grp, CHUNK, SSD_INNER), lambda b, c: (b, c, 0)),
        out_shape=jax.ShapeDtypeStruct((bsz, seq, SSD_INNER), BF16),
        scratch_shapes=[pltpu.VMEM((grp, SUBLANES, SSD_CONV_DIM), F32), pltpu.VMEM((grp, SSD_BC, SSD_INNER), F32)],
        compiler_params=_cparams(("parallel", "arbitrary")),
        name="ssd",
    )(proj3, proj3, dt3, conv_w, conv_b, dt_bias, a_log, d_skip, nw, tri, expand)
    return out.reshape(bsz * seq, SSD_INNER)


SB_BLOCK = 256


def _stickbreak_kernel(q_ref, k_ref, v_ref, u_ref, o_ref, acc_ref):
    i = pl.program_id(1)
    blk = SB_BLOCK
    q = q_ref[...] * jnp.asarray(HEAD_DIM ** -0.5, BF16)
    lane = lax.broadcasted_iota(jnp.int32, (1, SB_DIM), 1)
    head = lane // HEAD_DIM
    q_heads = [jnp.where(head == h, q, jnp.zeros_like(q)) for h in range(SB_HEADS)]
    u = u_ref[...]
    strict_lower = (lax.broadcasted_iota(jnp.int32, (blk, blk), 1)
                    < lax.broadcasted_iota(jnp.int32, (blk, blk), 0))

    def block(j, carries, diagonal):
        start = pl.multiple_of(j * blk, blk)
        kb = k_ref[pl.ds(start, blk), :]
        vb = v_ref[pl.ds(start, blk), :]
        new = []
        for h in range(SB_HEADS):
            z = _dot_nt(q_heads[h], kb)
            sp = _softplus(z)
            if diagonal:
                sp = jnp.where(strict_lower, sp, 0.0)
            cs = _dot(sp.astype(BF16), u)
            w = jnp.exp(z - cs - carries[h])
            if diagonal:
                w = jnp.where(strict_lower, w, 0.0)
            pair = h // 2
            pv = _dot(w.astype(BF16), vb[:, pair * LANES:(pair + 1) * LANES])
            if diagonal:
                acc_ref[h] = pv
            else:
                acc_ref[h] += pv
            new.append(carries[h] + jnp.sum(sp, axis=-1, keepdims=True))
        return tuple(new)

    zero = jnp.zeros((blk, 1), F32)
    carries = block(i, (zero,) * SB_HEADS, True)
    lax.fori_loop(0, i, lambda jj, c: block(i - 1 - jj, c, False), carries)
    first_of_pair = lax.broadcasted_iota(jnp.int32, (1, LANES), 1) < HEAD_DIM
    out = [jnp.where(first_of_pair, acc_ref[2 * p], acc_ref[2 * p + 1]) for p in range(SB_HEADS // 2)]
    o_ref[...] = jnp.concatenate(out, axis=-1).astype(BF16)


def _stickbreak_tables():
    idx = np.arange(SB_BLOCK)
    rev = (idx[:, None] >= idx[None, :]).astype(np.float32)
    return jnp.asarray(rev, dtype=BF16)


def _stickbreak(proj, u, bsz, seq):
    nq = seq // SB_BLOCK
    w = SB_DIM
    return pl.pallas_call(
        _stickbreak_kernel,
        grid=(bsz, nq),
        in_specs=[
            pl.BlockSpec((SB_BLOCK, w), lambda b, i: (b * nq + i, OFF_BQ // w)),
            pl.BlockSpec((seq, w), lambda b, i: (b, OFF_BK // w)),
            pl.BlockSpec((seq, w), lambda b, i: (b, OFF_BV // w)),
            pl.BlockSpec((SB_BLOCK, SB_BLOCK), lambda b, i: (0, 0)),
        ],
        out_specs=pl.BlockSpec((SB_BLOCK, w), lambda b, i: (b * nq + i, 0)),
        out_shape=jax.ShapeDtypeStruct((bsz * seq, w), BF16),
        scratch_shapes=[pltpu.VMEM((SB_HEADS, SB_BLOCK, LANES), F32)],
        compiler_params=_cparams(("parallel", "arbitrary")),
        name="stickbreak",
    )(proj, proj, proj, u)


def _outproj_kernel(x_ref, ro_ref, so_ref, bo_ref, w_ref, o_ref, wb_ref):
    @pl.when(pl.program_id(0) == 0)
    def _():
        wb_ref[...] = w_ref[0].astype(BF16)

    acc = _dot(ro_ref[...], wb_ref[:RET_DIM, :])
    acc = acc + _dot(so_ref[...], wb_ref[RET_DIM:RET_DIM + SSD_INNER, :])
    acc = acc + _dot(bo_ref[...], wb_ref[RET_DIM + SSD_INNER:, :])
    o_ref[...] = x_ref[...] + acc


def _outproj(x, ro, so, bo, w, layer, tm=512):
    t = x.shape[0]
    return pl.pallas_call(
        _outproj_kernel,
        grid=(t // tm,),
        in_specs=[
            pl.BlockSpec((tm, D_MODEL), lambda m: (m, 0)),
            pl.BlockSpec((tm, RET_DIM), lambda m: (m, 0)),
            pl.BlockSpec((tm, SSD_INNER), lambda m: (m, 0)),
            pl.BlockSpec((tm, SB_DIM), lambda m: (m, 0)),
            pl.BlockSpec((1, D_MIX, D_MODEL), lambda m: (layer, 0, 0)),
        ],
        out_specs=pl.BlockSpec((tm, D_MODEL), lambda m: (m, 0)),
        out_shape=jax.ShapeDtypeStruct((t, D_MODEL), F32),
        scratch_shapes=[pltpu.VMEM((D_MIX, D_MODEL), BF16)],
        compiler_params=_cparams(("arbitrary",)),
        name="outproj",
    )(x, ro, so, bo, w)


def _ffn_kernel(x_ref, nw_ref, wg_ref, wu_ref, wd_ref, o_ref, h_ref, acc_ref):
    f = pl.program_id(1)

    @pl.when(f == 0)
    def _():
        h_ref[...] = _rms_h(x_ref[...], nw_ref[...]).astype(BF16)
        acc_ref[...] = jnp.zeros_like(acc_ref)

    h = h_ref[...]
    act = _silu(_dot(h, wg_ref[0].astype(BF16))) * _dot(h, wu_ref[0].astype(BF16))
    acc_ref[...] += _dot(act.astype(BF16), wd_ref[0].astype(BF16))

    @pl.when(f == pl.num_programs(1) - 1)
    def _():
        o_ref[...] = x_ref[...] + acc_ref[...]


def _ffn(x, nw, wg, wu, wd, layer, tm=1024, tf=512):
    t = x.shape[0]
    return pl.pallas_call(
        _ffn_kernel,
        grid=(t // tm, D_FF // tf),
        in_specs=[
            pl.BlockSpec((tm, D_MODEL), lambda m, f: (m, 0)),
            pl.BlockSpec((1, D_MODEL), lambda m, f: (0, 0)),
            pl.BlockSpec((1, D_MODEL, tf), lambda m, f: (layer, 0, f)),
            pl.BlockSpec((1, D_MODEL, tf), lambda m, f: (layer, 0, f)),
            pl.BlockSpec((1, tf, D_MODEL), lambda m, f: (layer, f, 0)),
        ],
        out_specs=pl.BlockSpec((tm, D_MODEL), lambda m, f: (m, 0)),
        out_shape=jax.ShapeDtypeStruct((t, D_MODEL), F32),
        scratch_shapes=[pltpu.VMEM((tm, D_MODEL), BF16), pltpu.VMEM((tm, D_MODEL), F32)],
        compiler_params=_cparams(("parallel", "arbitrary")),
        name="ffn",
    )(x, nw, wg, wu, wd)


def _router_kernel(x_ref, nw_ref, rw_ref, idx_ref, gate_ref):
    h = _rms_h(x_ref[...], nw_ref[...])
    h_hi, h_lo = _split2(h)
    w_hi, w_lo = _split2(rw_ref[...])
    logits = _dot(h_hi, w_hi) + _dot(h_hi, w_lo) + _dot(h_lo, w_hi)
    lane = lax.broadcasted_iota(jnp.int32, (1, LANES), 1)
    lg = jnp.where(lane < N_EXPERTS, logits, -jnp.inf)
    m1 = jnp.max(lg, axis=-1, keepdims=True)
    i1 = jnp.min(jnp.where(lg == m1, lane, LANES), axis=-1, keepdims=True)
    lg2 = jnp.where(lane == i1, -jnp.inf, lg)
    m2 = jnp.max(lg2, axis=-1, keepdims=True)
    i2 = jnp.min(jnp.where(lg2 == m2, lane, LANES), axis=-1, keepdims=True)
    e = jnp.exp(m2 - m1)
    g1 = 1.0 / (1.0 + e)
    g2 = e / (1.0 + e)
    idx_ref[...] = jnp.where(lane == 0, i1, jnp.where(lane == 1, i2, 0))
    gate_ref[...] = jnp.where(lane == 0, g1, jnp.where(lane == 1, g2, 0.0))


def _router(x, nw, rw, tm=512):
    t = x.shape[0]
    return pl.pallas_call(
        _router_kernel,
        grid=(t // tm,),
        in_specs=[
            pl.BlockSpec((tm, D_MODEL), lambda m: (m, 0)),
            pl.BlockSpec((1, D_MODEL), lambda m: (0, 0)),
            pl.BlockSpec((D_MODEL, LANES), lambda m: (0, 0)),
        ],
        out_specs=[
            pl.BlockSpec((tm, LANES), lambda m: (m, 0)),
            pl.BlockSpec((tm, LANES), lambda m: (m, 0)),
        ],
        out_shape=[jax.ShapeDtypeStruct((t, LANES), jnp.int32), jax.ShapeDtypeStruct((t, LANES), F32)],
        compiler_params=_cparams(("parallel",)),
        name="router",
    )(x, nw, rw)


MOE_TM = 1024
MOE_TF = 512
MOE_FF_BLOCKS = D_FF // MOE_TF
MOE_SHARE = -(-MOE_TM // MOE_FF_BLOCKS)
MOE_ROWS_BUF = -(-MOE_SHARE * MOE_FF_BLOCKS // SUBLANES) * SUBLANES


def _moe_plan(e2):
    t = e2.shape[0]
    p = 2 * t
    ef = e2.reshape(p)
    onehot = (ef[:, None] == jnp.arange(N_EXPERTS, dtype=jnp.int32)[None, :]).astype(jnp.int32)
    csum = jnp.cumsum(onehot, axis=0)
    counts = csum[-1]
    rank = jnp.sum((csum - onehot) * onehot, axis=1)
    padded = ((counts + MOE_TM - 1) // MOE_TM) * MOE_TM
    ends = jnp.cumsum(padded)
    starts = ends - padded
    dst = jnp.sum(onehot * starts[None, :], axis=1) + rank
    n_tiles = p // MOE_TM + N_EXPERTS
    tile_start = jnp.arange(n_tiles, dtype=jnp.int32) * MOE_TM
    tile_expert = jnp.sum((tile_start[:, None] >= ends[None, :]).astype(jnp.int32), axis=1)
    tile_expert = jnp.minimum(tile_expert, N_EXPERTS - 1)
    n_used = (ends[-1] // MOE_TM).reshape(1).astype(jnp.int32)
    src = jnp.zeros((n_tiles * MOE_TM,), jnp.int32).at[dst].set(jnp.arange(p, dtype=jnp.int32) // 2)
    return dst.reshape(t, 2), src, tile_expert.astype(jnp.int32), n_used


def _expert_kernel(te_ref, nu_ref, src_ref, src_next_ref, x_hbm, nw_ref, wg_ref, wu_ref, wd_ref, o_ref,
                   rows_ref, sem, h_ref, acc_ref):
    i = pl.program_id(0)
    f = pl.program_id(1)
    nf = pl.num_programs(1)
    n_used = nu_ref[0]
    slot = i % 2
    share = MOE_SHARE
    issued = MOE_SHARE * MOE_FF_BLOCKS

    def gather_copy(idx_ref, r, s):
        row = idx_ref[0, 0, jnp.minimum(r, MOE_TM - 1)]
        return pltpu.make_async_copy(x_hbm.at[pl.ds(row, 1)], rows_ref.at[s, pl.ds(r, 1)], sem.at[s])

    def wait_slot(s):
        pltpu.make_async_copy(x_hbm.at[pl.ds(0, MOE_TM)], rows_ref.at[s, pl.ds(0, MOE_TM)], sem.at[s]).wait()
        for r in range(MOE_TM, issued):
            pltpu.make_async_copy(x_hbm.at[pl.ds(0, 1)], rows_ref.at[s, pl.ds(r, 1)], sem.at[s]).wait()

    @pl.when(f == 0)
    def _():
        acc_ref[...] = jnp.zeros_like(acc_ref)

    @pl.when((f == 0) & (i == 0))
    def _():
        def body(r, carry):
            gather_copy(src_ref, r, 0).start()
            return carry
        lax.fori_loop(0, issued, body, 0, unroll=8)

    @pl.when((f == 0) & (i < n_used))
    def _():
        wait_slot(slot)
        h_ref[...] = _rms_h(rows_ref[slot, :MOE_TM, :], nw_ref[...]).astype(BF16)

    @pl.when(i < n_used)
    def _():
        for k in range(share):
            gather_copy(src_next_ref, f * share + k, 1 - slot).start()
        h = h_ref[...]
        act = _silu(_dot(h, wg_ref[0, 0].astype(BF16))) * _dot(h, wu_ref[0, 0].astype(BF16))
        acc_ref[...] += _dot(act.astype(BF16), wd_ref[0, 0].astype(BF16))

    @pl.when((i == n_used - 1) & (f == nf - 1))
    def _():
        wait_slot(1 - slot)

    @pl.when(f == nf - 1)
    def _():
        o_ref[...] = acc_ref[...].astype(BF16)


def _experts(x, nw, src, tile_expert, n_used, wg, wu, wd, layer):
    n_tiles = tile_expert.shape[0]
    tf = MOE_TF
    nf = MOE_FF_BLOCKS
    src2 = src.reshape(n_tiles, 1, MOE_TM)

    def fblock(i, f, nu):
        return jnp.where(i < nu[0], f, nf - 1)

    smem_row = functools.partial(pl.BlockSpec, (1, 1, MOE_TM), memory_space=pltpu.SMEM)
    grid_spec = pltpu.PrefetchScalarGridSpec(
        num_scalar_prefetch=2,
        grid=(n_tiles, nf),
        in_specs=[
            smem_row(lambda i, f, te, nu: (i, 0, 0)),
            smem_row(lambda i, f, te, nu: (jnp.minimum(i + 1, n_tiles - 1), 0, 0)),
            pl.BlockSpec(memory_space=pl.ANY),
            pl.BlockSpec((1, D_MODEL), lambda i, f, te, nu: (0, 0)),
            pl.BlockSpec((1, 1, D_MODEL, tf), lambda i, f, te, nu: (layer, te[i], 0, fblock(i, f, nu))),
            pl.BlockSpec((1, 1, D_MODEL, tf), lambda i, f, te, nu: (layer, te[i], 0, fblock(i, f, nu))),
            pl.BlockSpec((1, 1, tf, D_MODEL), lambda i, f, te, nu: (layer, te[i], fblock(i, f, nu), 0)),
        ],
        out_specs=pl.BlockSpec((MOE_TM, D_MODEL), lambda i, f, te, nu: (i, 0)),
        scratch_shapes=[
            pltpu.VMEM((2, MOE_ROWS_BUF, D_MODEL), F32),
            pltpu.SemaphoreType.DMA((2,)),
            pltpu.VMEM((MOE_TM, D_MODEL), BF16),
            pltpu.VMEM((MOE_TM, D_MODEL), F32),
        ],
    )
    return pl.pallas_call(
        _expert_kernel,
        grid_spec=grid_spec,
        out_shape=jax.ShapeDtypeStruct((n_tiles * MOE_TM, D_MODEL), BF16),
        compiler_params=_cparams(("arbitrary", "arbitrary")),
        name="experts",
    )(tile_expert, n_used, src2, src2, x, nw, wg, wu, wd)


def _combine_kernel(x_ref, y1_ref, y2_ref, gate_ref, nw_ref, o_ref, *, final_norm):
    g = gate_ref[...]
    out = x_ref[...] + g[:, 0:1] * y1_ref[...].astype(F32) + g[:, 1:2] * y2_ref[...].astype(F32)
    if final_norm:
        out = _rms_h(out, nw_ref[...])
    o_ref[...] = out


def _combine(x, y1, y2, gates, nw, final_norm, tm=1024):
    t = x.shape[0]
    return pl.pallas_call(
        functools.partial(_combine_kernel, final_norm=final_norm),
        grid=(t // tm,),
        in_specs=[
            pl.BlockSpec((tm, D_MODEL), lambda m: (m, 0)),
            pl.BlockSpec((tm, D_MODEL), lambda m: (m, 0)),
            pl.BlockSpec((tm, D_MODEL), lambda m: (m, 0)),
            pl.BlockSpec((tm, LANES), lambda m: (m, 0)),
            pl.BlockSpec((1, D_MODEL), lambda m: (0, 0)),
        ],
        out_specs=pl.BlockSpec((tm, D_MODEL), lambda m: (m, 0)),
        out_shape=jax.ShapeDtypeStruct((t, D_MODEL), F32),
        compiler_params=_cparams(("parallel",)),
        name="combine",
    )(x, y1, y2, gates, nw)


def _moe_layer(xf, norm_w, router_w, wg, wu, wd, layer, final_w=None):
    rw = jnp.pad(router_w.astype(F32), ((0, 0), (0, LANES - N_EXPERTS)))
    idx, gates = _router(xf, _row(norm_w), rw)
    dst, src, tile_expert, n_used = _moe_plan(idx[:, :2])
    ys = _experts(xf, _row(norm_w), src, tile_expert, n_used, wg, wu, wd, layer)
    y1 = ys.at[dst[:, 0]].get(mode="promise_in_bounds")
    y2 = ys.at[dst[:, 1]].get(mode="promise_in_bounds")
    nw = _row(final_w) if final_w is not None else jnp.ones((1, D_MODEL), F32)
    return _combine(xf, y1, y2, gates, nw, final_w is not None)


def _final_norm_kernel(x_ref, nw_ref, o_ref):
    o_ref[...] = _rms_h(x_ref[...], nw_ref[...])


def _final_norm(x, nw, tm=1024):
    t = x.shape[0]
    return pl.pallas_call(
        _final_norm_kernel,
        grid=(t // tm,),
        in_specs=[pl.BlockSpec((tm, D_MODEL), lambda m: (m, 0)), pl.BlockSpec((1, D_MODEL), lambda m: (0, 0))],
        out_specs=pl.BlockSpec((tm, D_MODEL), lambda m: (m, 0)),
        out_shape=jax.ShapeDtypeStruct((t, D_MODEL), F32),
        compiler_params=_cparams(("parallel",)),
        name="final_norm",
    )(x, nw)


def _rotary_perm():
    half = HEAD_DIM // 2
    first = [h * HEAD_DIM + d for h in range(RET_HEADS) for d in range(half)]
    second = [h * HEAD_DIM + half + d for h in range(RET_HEADS) for d in range(half)]
    return np.asarray(first + second, np.int32)


def _layout_w_in(w_in):
    perm = _rotary_perm()
    dt0 = 4 * RET_DIM + SSD_INNER + SSD_CONV_DIM
    main = jnp.concatenate([
        w_in[:, perm], w_in[:, RET_DIM + perm], w_in[:, 2 * RET_DIM:dt0], w_in[:, dt0 + SSD_HEADS:],
    ], axis=1).astype(BF16)
    wdt = jnp.pad(w_in[:, dt0:dt0 + SSD_HEADS], ((0, 0), (0, LANES - SSD_HEADS))).astype(BF16)
    return main, wdt


def _row(v, width=None):
    v = v.reshape(1, -1).astype(F32)
    if width is not None and v.shape[1] < width:
        v = jnp.pad(v, ((0, 0), (0, width - v.shape[1])))
    return v


def kernel(x, norm1_w, w_in, ret_norm_w, ssd_conv_w, ssd_conv_b, ssd_dt_bias, ssd_a_log, ssd_d, ssd_norm_w, w_out, norm2_w, ffn_w_gate, ffn_w_up, ffn_w_down, moe_router, moe_w_gate, moe_w_up, moe_w_down, final_norm_w):
    bsz, seq, _ = x.shape
    depth = w_in.shape[0]
    ret_tables = _retention_tables(seq)
    ssd_tables = _ssd_tables()
    sb_u = _stickbreak_tables()

    xf = x.reshape(bsz * seq, D_MODEL)
    for layer in range(depth):
        w_main, w_dt = _layout_w_in(w_in[layer])
        proj, dt_raw = _inproj(xf, _row(norm1_w[layer]), w_main, w_dt)
        ro = _retention(proj, ret_tables, _row(ret_norm_w[layer]), bsz, seq)
        so = _ssd(proj, dt_raw, ssd_conv_w[layer].astype(F32), _row(ssd_conv_b[layer]),
                  _row(ssd_dt_bias[layer], LANES), _row(ssd_a_log[layer], LANES),
                  _row(jnp.repeat(ssd_d[layer], HEAD_DIM)), _row(ssd_norm_w[layer]), ssd_tables, bsz, seq)
        bo = _stickbreak(proj, sb_u, bsz, seq)
        xf = _outproj(xf, ro, so, bo, w_out, layer)
        j = layer // 2
        if layer % 2 == 0:
            xf = _ffn(xf, _row(norm2_w[layer]), ffn_w_gate, ffn_w_up, ffn_w_down, j)
        else:
            last = layer == depth - 1
            xf = _moe_layer(xf, norm2_w[layer], moe_router[j], moe_w_gate, moe_w_up, moe_w_down, j,
                            final_norm_w if last else None)
    if depth % 2 == 1:
        xf = _final_norm(xf, _row(final_norm_w))
    return xf.reshape(bsz, seq, D_MODEL)
```

```python
import functools
import math

import numpy as np
import jax
import jax.numpy as jnp
from jax import lax
from jax.experimental import pallas as pl
from jax.experimental.pallas import tpu as pltpu

F32 = jnp.float32
BF16 = jnp.bfloat16

D_MODEL = 1024
HEAD_DIM = 64
RET_HEADS = 4
RET_DIM = RET_HEADS * HEAD_DIM
SSD_HEADS = 8
SSD_INNER = SSD_HEADS * HEAD_DIM
SSD_GROUPS = 2
SSD_STATE = 64
SSD_CONV = 4
SSD_BC = SSD_GROUPS * SSD_STATE
SSD_CONV_DIM = SSD_INNER + 2 * SSD_BC
SB_HEADS = 4
SB_DIM = SB_HEADS * HEAD_DIM
D_MIX = RET_DIM + SSD_INNER + SB_DIM
CHUNK = 128
D_FF = 3584
N_EXPERTS = 8
ROPE_BASE = 10000.0
EPS = 1e-6

LANES = 128
SUBLANES = 8
D_PROJ = 3072
OFF_RQ, OFF_RK, OFF_RV, OFF_RG = 0, 256, 512, 768
OFF_SZ, OFF_SXBC = 1024, 1536
OFF_BQ, OFF_BK, OFF_BV = 2304, 2560, 2816

VMEM_LIMIT = 56 * 1024 * 1024


def _cparams(sem):
    return pltpu.CompilerParams(dimension_semantics=sem, vmem_limit_bytes=VMEM_LIMIT)


def _dot(a, b):
    return jnp.dot(a, b, preferred_element_type=F32)


def _dot_nt(a, b):
    return lax.dot_general(a, b, (((1,), (1,)), ((), ())), preferred_element_type=F32)


def _split2(x):
    hi = x.astype(BF16)
    lo = (x - hi.astype(F32)).astype(BF16)
    return hi, lo


def _split3(x):
    hi = x.astype(BF16)
    r = x - hi.astype(F32)
    mid = r.astype(BF16)
    lo = (r - mid.astype(F32)).astype(BF16)
    return hi, mid, lo


def _dot_x_const(x, c):
    hi, lo = _split2(x)
    return _dot(hi, c) + _dot(lo, c)


def _silu(x):
    return x * jax.nn.sigmoid(x)


def _softplus(x):
    e = jnp.exp2(jnp.abs(x) * (-math.log2(math.e)))
    return jnp.maximum(x, 0.0) + jnp.log(1.0 + e)


def _pair_heads_matmul(m_stack, x, n_heads):
    rows = m_stack.shape[0] // n_heads
    first_of_pair = lax.broadcasted_iota(jnp.int32, (1, LANES), 1) < HEAD_DIM
    parts = []
    for p in range(n_heads // 2):
        both = _dot(m_stack[2 * p * rows:(2 * p + 2) * rows], x[:, p * LANES:(p + 1) * LANES])
        parts.append(jnp.where(first_of_pair, both[:rows], both[rows:]))
    return jnp.concatenate(parts, axis=-1)


def _rms_h(x, w):
    ms = jnp.mean(x * x, axis=-1, keepdims=True)
    return x * lax.rsqrt(ms + EPS) * w


def _inproj_kernel(x_ref, nw_ref, w_ref, wdt_ref, o_ref, dt_ref):
    h = _rms_h(x_ref[...], nw_ref[...]).astype(BF16)
    step = 512
    for c in range(D_PROJ // step):
        o_ref[:, c * step:(c + 1) * step] = _dot(h, w_ref[:, c * step:(c + 1) * step]).astype(BF16)
    dt_ref[...] = _dot(h, wdt_ref[...])


def _inproj(x, nw, w, wdt, tm=512):
    t = x.shape[0]
    return pl.pallas_call(
        _inproj_kernel,
        grid=(t // tm,),
        in_specs=[
            pl.BlockSpec((tm, D_MODEL), lambda m: (m, 0)),
            pl.BlockSpec((1, D_MODEL), lambda m: (0, 0)),
            pl.BlockSpec((D_MODEL, D_PROJ), lambda m: (0, 0)),
            pl.BlockSpec((D_MODEL, LANES), lambda m: (0, 0)),
        ],
        out_specs=[
            pl.BlockSpec((tm, D_PROJ), lambda m: (m, 0)),
            pl.BlockSpec((tm, LANES), lambda m: (m, 0)),
        ],
        out_shape=[jax.ShapeDtypeStruct((t, D_PROJ), BF16), jax.ShapeDtypeStruct((t, LANES), F32)],
        compiler_params=_cparams(("parallel",)),
        name="inproj",
    )(x, nw, w, wdt)


def _retention_kernel(q_ref, k_ref, v_ref, g_ref, cos_ref, sin_ref, intra_ref, qdec_ref, kdec_ref,
                      cd_ref, avg_ref, nw_ref, o_ref, state_ref):
    c = pl.program_id(1)

    @pl.when(c == 0)
    def _():
        state_ref[...] = jnp.zeros_like(state_ref)

    cos = cos_ref[...]
    sin = sin_ref[...]

    def rot(t):
        t1, t2 = t[:, :LANES], t[:, LANES:]
        return jnp.concatenate([t1 * cos - t2 * sin, t1 * sin + t2 * cos], axis=-1)

    lane = lax.broadcasted_iota(jnp.int32, (1, RET_DIM), 1)
    head_qk = (lane % LANES) // (HEAD_DIM // 2)
    cd = cd_ref[...]
    avg = avg_ref[...]

    for b in range(q_ref.shape[0]):
        rq = rot(q_ref[b].astype(F32))
        rk = rot(k_ref[b].astype(F32))
        vb = v_ref[b]

        q_stack = jnp.concatenate([jnp.where(head_qk == h, rq, 0.0) for h in range(RET_HEADS)], axis=0)
        scores = _dot_nt(q_stack.astype(BF16), rk.astype(BF16)) * intra_ref[...]
        y = _pair_heads_matmul(scores.astype(BF16), vb, RET_HEADS)

        state = state_ref[b]
        y = y + _dot((rq * qdec_ref[...]).astype(BF16), state.astype(BF16))
        kd = (rk * kdec_ref[...]).astype(BF16)
        kv = lax.dot_general(kd, vb, (((0,), (0,)), ((), ())), preferred_element_type=F32)
        state_ref[b] = state * cd + jnp.where(cd != 0.0, kv, 0.0)

        mu = _dot_x_const(y, avg)
        d = y - mu
        var = _dot_x_const(d * d, avg)
        yn = d * lax.rsqrt(var + EPS)
        o_ref[b] = (yn * nw_ref[...] * _silu(g_ref[b].astype(F32))).astype(BF16)


def _retention_tables(seq):
    f = np.float32
    half = HEAD_DIM // 2
    inv = (1.0 / (f(ROPE_BASE) ** (np.arange(half, dtype=f) / f(half)))).astype(f)
    ang = np.arange(seq, dtype=f)[:, None] * inv[None, :]
    cos = np.tile(np.cos(ang).astype(f), (1, RET_HEADS))
    sin = np.tile(np.sin(ang).astype(f), (1, RET_HEADS))
    log_gamma = np.log(f(1.0) - f(2.0) ** (f(-5.0) - np.arange(RET_HEADS, dtype=f))).astype(f)
    idx = np.arange(CHUNK, dtype=f)
    diff = idx[:, None] - idx[None, :]
    scale = f(HEAD_DIM ** -0.5)
    intra = np.where(diff >= 0, np.exp(log_gamma[:, None, None] * np.maximum(diff, 0.0)), 0.0).astype(f)
    intra = (intra * scale).reshape(RET_HEADS * CHUNK, CHUNK)
    head_qk = (np.arange(RET_DIM) % LANES) // half
    head_v = np.arange(RET_DIM) // HEAD_DIM
    qdec = np.exp(log_gamma[head_qk][None, :] * (idx[:, None] + 1.0)).astype(f)
    kdec = (np.exp(log_gamma[head_qk][None, :] * (CHUNK - 1.0 - idx[:, None])) * scale).astype(f)
    chunk_decay = np.exp(log_gamma * f(CHUNK)).astype(f)
    same = head_qk[:, None] == head_v[None, :]
    cd = np.where(same, chunk_decay[head_qk][:, None], 0.0).astype(f)
    avg = np.where(head_v[:, None] == head_v[None, :], 1.0 / HEAD_DIM, 0.0).astype(f)
    return (jnp.asarray(cos), jnp.asarray(sin), jnp.asarray(intra), jnp.asarray(qdec), jnp.asarray(kdec),
            jnp.asarray(cd), jnp.asarray(avg, dtype=BF16))


MIXER_GROUP = 4


def _retention(proj, tables, nw, bsz, seq):
    nc = seq // CHUNK
    cos, sin, intra, qdec, kdec, cd, avg = tables
    w = RET_DIM
    grp = MIXER_GROUP if bsz % MIXER_GROUP == 0 else 1
    proj3 = proj.reshape(bsz, seq, D_PROJ)

    def col(j):
        return pl.BlockSpec((grp, CHUNK, w), lambda b, c: (b, c, j))

    def const(shape):
        return pl.BlockSpec(shape, lambda b, c: (0, 0))

    out = pl.pallas_call(
        _retention_kernel,
        grid=(bsz // grp, nc),
        in_specs=[
            col(OFF_RQ // w), col(OFF_RK // w), col(OFF_RV // w), col(OFF_RG // w),
            pl.BlockSpec((CHUNK, LANES), lambda b, c: (c, 0)),
            pl.BlockSpec((CHUNK, LANES), lambda b, c: (c, 0)),
            const((RET_HEADS * CHUNK, CHUNK)), const((CHUNK, w)), const((CHUNK, w)),
            const((w, w)), const((w, w)), const((1, w)),
        ],
        out_specs=pl.BlockSpec((grp, CHUNK, w), lambda b, c: (b, c, 0)),
        out_shape=jax.ShapeDtypeStruct((bsz, seq, w), BF16),
        scratch_shapes=[pltpu.VMEM((grp, w, w), F32)],
        compiler_params=_cparams(("parallel", "arbitrary")),
        name="retention",
    )(proj3, proj3, proj3, proj3, cos, sin, intra, qdec, kdec, cd, avg, nw)
    return out.reshape(bsz * seq, w)


def _ssd_kernel(z_ref, xbc_ref, dt_ref, cw_ref, cb_ref, dtb_ref, alog_ref, dskip_ref, nw_ref,
                tri_ref, exp_ref, o_ref, prev_ref, state_ref):
    c = pl.program_id(1)

    @pl.when(c == 0)
    def _():
        prev_ref[...] = jnp.zeros_like(prev_ref)
        state_ref[...] = jnp.zeros_like(state_ref)

    row = lax.broadcasted_iota(jnp.int32, (SUBLANES, 1), 0)
    tri = tri_ref[...]
    lane_bc = lax.broadcasted_iota(jnp.int32, (1, SSD_BC), 1)
    lower = (lax.broadcasted_iota(jnp.int32, (CHUNK, CHUNK), 0)
             >= lax.broadcasted_iota(jnp.int32, (CHUNK, CHUNK), 1))
    heads_per_group = SSD_HEADS // SSD_GROUPS
    lane_x = lax.broadcasted_iota(jnp.int32, (1, SSD_INNER), 1)
    row_g = lax.broadcasted_iota(jnp.int32, (SSD_BC, 1), 0) // SSD_STATE
    same_group = row_g == lane_x // (SSD_INNER // SSD_GROUPS)
    gw = SSD_INNER // SSD_GROUPS

    for b in range(z_ref.shape[0]):
        cur = xbc_ref[b].astype(F32)
        tail = prev_ref[b]
        acc = cur * cw_ref[SSD_CONV - 1:SSD_CONV, :] + cb_ref[...]
        for j in range(1, SSD_CONV):
            rolled = pltpu.roll(cur, j, 0)
            top = jnp.where(row >= j, rolled[:SUBLANES], pltpu.roll(tail, j, 0))
            shifted = jnp.concatenate([top, rolled[SUBLANES:]], axis=0)
            acc = acc + shifted * cw_ref[SSD_CONV - 1 - j:SSD_CONV - j, :]
        prev_ref[b] = cur[CHUNK - SUBLANES:]
        xa = _silu(acc)
        x = xa[:, :SSD_INNER]
        bm = xa[:, SSD_INNER:SSD_INNER + SSD_BC]
        cm = xa[:, SSD_INNER + SSD_BC:]

        dt = _softplus(dt_ref[b] + dtb_ref[...])
        a = dt * (-jnp.exp(alog_ref[...]))
        a_hi, a_mid, a_lo = _split3(a)
        a_cs = _dot(tri, a_hi) + _dot(tri, a_mid) + _dot(tri, a_lo)
        a_cs_t = a_cs.T
        stack = jnp.concatenate([dt, jnp.exp(a_cs), jnp.exp(a_cs[CHUNK - 1:CHUNK, :] - a_cs)], axis=0)
        stack_x = _dot_x_const(stack, exp_ref[...])
        dt_x = stack_x[:CHUNK]
        ea_x = stack_x[CHUNK:2 * CHUNK]
        dec_x = stack_x[2 * CHUNK:]

        xdt = x * dt_x
        xdt_b = xdt.astype(BF16)
        bm_b = bm.astype(BF16)
        cm_b = cm.astype(BF16)
        cb = [_dot_nt(jnp.where(lane_bc // SSD_STATE == g, cm, 0.0).astype(BF16), bm_b)
              for g in range(SSD_GROUPS)]

        m_list = []
        for h in range(SSD_HEADS):
            seg = a_cs[:, h:h + 1] - a_cs_t[h:h + 1, :]
            l_mat = jnp.exp(jnp.where(lower, seg, -jnp.inf))
            m_list.append((cb[h // heads_per_group] * l_mat).astype(BF16))
        y = _pair_heads_matmul(jnp.concatenate(m_list, axis=0), xdt_b, SSD_HEADS)

        state = state_ref[b]
        y = y + _dot(cm_b, state.astype(BF16)) * ea_x
        contrib = _dot(bm.T.astype(BF16), (xdt * dec_x).astype(BF16))
        state_ref[b] = state * ea_x[CHUNK - 1:CHUNK, :] + jnp.where(same_group, contrib, 0.0)

        y = y + dskip_ref[...] * x
        gy = y * _silu(z_ref[b].astype(F32))
        outs = []
        for g in range(SSD_GROUPS):
            part = gy[:, g * gw:(g + 1) * gw]
            outs.append(part * lax.rsqrt(jnp.mean(part * part, axis=-1, keepdims=True) + EPS))
        o_ref[b] = (jnp.concatenate(outs, axis=-1) * nw_ref[...]).astype(BF16)


def _ssd_tables():
    idx = np.arange(CHUNK)
    tri = (idx[:, None] >= idx[None, :]).astype(np.float32)
    expand = np.zeros((LANES, SSD_INNER), np.float32)
    for h in range(SSD_HEADS):
        expand[h, h * HEAD_DIM:(h + 1) * HEAD_DIM] = 1.0
    return jnp.asarray(tri, dtype=BF16), jnp.asarray(expand, dtype=BF16)


def _ssd(proj, dt_raw, conv_w, conv_b, dt_bias, a_log, d_skip, nw, tables, bsz, seq):
    nc = seq // CHUNK
    tri, expand = tables

    grp = MIXER_GROUP if bsz % MIXER_GROUP == 0 else 1
    proj3 = proj.reshape(bsz, seq, D_PROJ)
    dt3 = dt_raw.reshape(bsz, seq, LANES)

    def const(shape):
        return pl.BlockSpec(shape, lambda b, c: (0, 0))

    out = pl.pallas_call(
        _ssd_kernel,
        grid=(bsz // grp, nc),
        in_specs=[
            pl.BlockSpec((grp, CHUNK, SSD_INNER), lambda b, c: (b, c, OFF_SZ // SSD_INNER)),
            pl.BlockSpec((grp, CHUNK, SSD_CONV_DIM), lambda b, c: (b, c, OFF_SXBC // SSD_CONV_DIM)),
            pl.BlockSpec((grp, CHUNK, LANES), lambda b, c: (b, c, 0)),
            const((SSD_CONV, SSD_CONV_DIM)), const((1, SSD_CONV_DIM)), const((1, LANES)), const((1, LANES)),
            const((1, SSD_INNER)), const((1, SSD_INNER)), const((CHUNK, CHUNK)), const((LANES, SSD_INNER)),
        ],
        out_specs=pl.BlockSpec((grp, CHUNK, SSD_INNER), lambda b, c: (b, c, 0)),
        out_shape=jax.ShapeDtypeStruct((bsz, seq, SSD_INNER), BF16),
        scratch_shapes=[pltpu.VMEM((grp, SUBLANES, SSD_CONV_DIM), F32), pltpu.VMEM((grp, SSD_BC, SSD_INNER), F32)],
        compiler_params=_cparams(("parallel", "arbitrary")),
        name="ssd",
    )(proj3, proj3, dt3, conv_w, conv_b, dt_bias, a_log, d_skip, nw, tri, expand)
    return out.reshape(bsz * seq, SSD_INNER)


SB_BLOCK = 256
SB_GROUP = 4


def _stickbreak_kernel(q_ref, k_ref, v_ref, u_ref, o_ref, acc_ref):
    i = pl.program_id(1)
    blk = SB_BLOCK
    grp = q_ref.shape[0]
    lane = lax.broadcasted_iota(jnp.int32, (1, SB_DIM), 1)
    head = lane // HEAD_DIM
    q_heads = []
    for b in range(grp):
        q = q_ref[b] * jnp.asarray(HEAD_DIM ** -0.5, BF16)
        q_heads.append([jnp.where(head == h, q, jnp.zeros_like(q)) for h in range(SB_HEADS)])
    u = u_ref[...]
    strict_lower = (lax.broadcasted_iota(jnp.int32, (blk, blk), 1)
                    < lax.broadcasted_iota(jnp.int32, (blk, blk), 0))

    def block(j, carries, diagonal):
        start = pl.multiple_of(j * blk, blk)
        new = []
        for b in range(grp):
            kb = k_ref[b, pl.ds(start, blk), :]
            vb = v_ref[b, pl.ds(start, blk), :]
            for h in range(SB_HEADS):
                carry = carries[b * SB_HEADS + h]
                z = _dot_nt(q_heads[b][h], kb)
                sp = _softplus(z)
                if diagonal:
                    sp = jnp.where(strict_lower, sp, 0.0)
                cs = _dot(sp.astype(BF16), u)
                w = jnp.exp(z - cs - carry)
                if diagonal:
                    w = jnp.where(strict_lower, w, 0.0)
                pair = h // 2
                pv = _dot(w.astype(BF16), vb[:, pair * LANES:(pair + 1) * LANES])
                if diagonal:
                    acc_ref[b, h] = pv
                else:
                    acc_ref[b, h] += pv
                new.append(carry + jnp.sum(sp, axis=-1, keepdims=True))
        return tuple(new)

    zero = jnp.zeros((blk, 1), F32)
    carries = block(i, (zero,) * (grp * SB_HEADS), True)
    lax.fori_loop(0, i, lambda jj, c: block(i - 1 - jj, c, False), carries)
    first_of_pair = lax.broadcasted_iota(jnp.int32, (1, LANES), 1) < HEAD_DIM
    for b in range(grp):
        out = [jnp.where(first_of_pair, acc_ref[b, 2 * p], acc_ref[b, 2 * p + 1]) for p in range(SB_HEADS // 2)]
        o_ref[b] = jnp.concatenate(out, axis=-1).astype(BF16)


def _stickbreak_tables():
    idx = np.arange(SB_BLOCK)
    rev = (idx[:, None] >= idx[None, :]).astype(np.float32)
    return jnp.asarray(rev, dtype=BF16)


def _stickbreak(proj, u, bsz, seq):
    nq = seq // SB_BLOCK
    w = SB_DIM
    grp = SB_GROUP if bsz % SB_GROUP == 0 else 1
    proj3 = proj.reshape(bsz, seq, D_PROJ)
    out = pl.pallas_call(
        _stickbreak_kernel,
        grid=(bsz // grp, nq),
        in_specs=[
            pl.BlockSpec((grp, SB_BLOCK, w), lambda b, i: (b, i, OFF_BQ // w)),
            pl.BlockSpec((grp, seq, w), lambda b, i: (b, 0, OFF_BK // w)),
            pl.BlockSpec((grp, seq, w), lambda b, i: (b, 0, OFF_BV // w)),
            pl.BlockSpec((SB_BLOCK, SB_BLOCK), lambda b, i: (0, 0)),
        ],
        out_specs=pl.BlockSpec((grp, SB_BLOCK, w), lambda b, i: (b, i, 0)),
        out_shape=jax.ShapeDtypeStruct((bsz, seq, w), BF16),
        scratch_shapes=[pltpu.VMEM((grp, SB_HEADS, SB_BLOCK, LANES), F32)],
        compiler_params=_cparams(("parallel", "arbitrary")),
        name="stickbreak",
    )(proj3, proj3, proj3, u)
    return out.reshape(bsz * seq, w)


def _outproj_kernel(x_ref, ro_ref, so_ref, bo_ref, w_ref, o_ref, wb_ref):
    @pl.when(pl.program_id(0) == 0)
    def _():
        wb_ref[...] = w_ref[0].astype(BF16)

    acc = _dot(ro_ref[...], wb_ref[:RET_DIM, :])
    acc = acc + _dot(so_ref[...], wb_ref[RET_DIM:RET_DIM + SSD_INNER, :])
    acc = acc + _dot(bo_ref[...], wb_ref[RET_DIM + SSD_INNER:, :])
    o_ref[...] = x_ref[...] + acc


def _outproj(x, ro, so, bo, w, layer, tm=512):
    t = x.shape[0]
    return pl.pallas_call(
        _outproj_kernel,
        grid=(t // tm,),
        in_specs=[
            pl.BlockSpec((tm, D_MODEL), lambda m: (m, 0)),
            pl.BlockSpec((tm, RET_DIM), lambda m: (m, 0)),
            pl.BlockSpec((tm, SSD_INNER), lambda m: (m, 0)),
            pl.BlockSpec((tm, SB_DIM), lambda m: (m, 0)),
            pl.BlockSpec((1, D_MIX, D_MODEL), lambda m: (layer, 0, 0)),
        ],
        out_specs=pl.BlockSpec((tm, D_MODEL), lambda m: (m, 0)),
        out_shape=jax.ShapeDtypeStruct((t, D_MODEL), F32),
        scratch_shapes=[pltpu.VMEM((D_MIX, D_MODEL), BF16)],
        compiler_params=_cparams(("arbitrary",)),
        name="outproj",
    )(x, ro, so, bo, w)


def _ffn_kernel(x_ref, nw_ref, wg_ref, wu_ref, wd_ref, o_ref, h_ref, acc_ref):
    f = pl.program_id(1)

    @pl.when(f == 0)
    def _():
        h_ref[...] = _rms_h(x_ref[...], nw_ref[...]).astype(BF16)
        acc_ref[...] = jnp.zeros_like(acc_ref)

    h = h_ref[...]
    act = _silu(_dot(h, wg_ref[0].astype(BF16))) * _dot(h, wu_ref[0].astype(BF16))
    acc_ref[...] += _dot(act.astype(BF16), wd_ref[0].astype(BF16))

    @pl.when(f == pl.num_programs(1) - 1)
    def _():
        o_ref[...] = x_ref[...] + acc_ref[...]


def _ffn(x, nw, wg, wu, wd, layer, tm=1024, tf=512):
    t = x.shape[0]
    return pl.pallas_call(
        _ffn_kernel,
        grid=(t // tm, D_FF // tf),
        in_specs=[
            pl.BlockSpec((tm, D_MODEL), lambda m, f: (m, 0)),
            pl.BlockSpec((1, D_MODEL), lambda m, f: (0, 0)),
            pl.BlockSpec((1, D_MODEL, tf), lambda m, f: (layer, 0, f)),
            pl.BlockSpec((1, D_MODEL, tf), lambda m, f: (layer, 0, f)),
            pl.BlockSpec((1, tf, D_MODEL), lambda m, f: (layer, f, 0)),
        ],
        out_specs=pl.BlockSpec((tm, D_MODEL), lambda m, f: (m, 0)),
        out_shape=jax.ShapeDtypeStruct((t, D_MODEL), F32),
        scratch_shapes=[pltpu.VMEM((tm, D_MODEL), BF16), pltpu.VMEM((tm, D_MODEL), F32)],
        compiler_params=_cparams(("parallel", "arbitrary")),
        name="ffn",
    )(x, nw, wg, wu, wd)


def _router_kernel(x_ref, nw_ref, rw_ref, idx_ref, gate_ref):
    h = _rms_h(x_ref[...], nw_ref[...])
    h_hi, h_lo = _split2(h)
    w_hi, w_lo = _split2(rw_ref[...])
    logits = _dot(h_hi, w_hi) + _dot(h_hi, w_lo) + _dot(h_lo, w_hi)
    lane = lax.broadcasted_iota(jnp.int32, (1, LANES), 1)
    lg = jnp.where(lane < N_EXPERTS, logits, -jnp.inf)
    m1 = jnp.max(lg, axis=-1, keepdims=True)
    i1 = jnp.min(jnp.where(lg == m1, lane, LANES), axis=-1, keepdims=True)
    lg2 = jnp.where(lane == i1, -jnp.inf, lg)
    m2 = jnp.max(lg2, axis=-1, keepdims=True)
    i2 = jnp.min(jnp.where(lg2 == m2, lane, LANES), axis=-1, keepdims=True)
    e = jnp.exp(m2 - m1)
    g1 = 1.0 / (1.0 + e)
    g2 = e / (1.0 + e)
    idx_ref[...] = jnp.where(lane == 0, i1, jnp.where(lane == 1, i2, 0))
    gate_ref[...] = jnp.where(lane == 0, g1, jnp.where(lane == 1, g2, 0.0))


def _router(x, nw, rw, tm=512):
    t = x.shape[0]
    return pl.pallas_call(
        _router_kernel,
        grid=(t // tm,),
        in_specs=[
            pl.BlockSpec((tm, D_MODEL), lambda m: (m, 0)),
            pl.BlockSpec((1, D_MODEL), lambda m: (0, 0)),
            pl.BlockSpec((D_MODEL, LANES), lambda m: (0, 0)),
        ],
        out_specs=[
            pl.BlockSpec((tm, LANES), lambda m: (m, 0)),
            pl.BlockSpec((tm, LANES), lambda m: (m, 0)),
        ],
        out_shape=[jax.ShapeDtypeStruct((t, LANES), jnp.int32), jax.ShapeDtypeStruct((t, LANES), F32)],
        compiler_params=_cparams(("parallel",)),
        name="router",
    )(x, nw, rw)


MOE_TM = 1024
MOE_TF = 512
MOE_FF_BLOCKS = D_FF // MOE_TF
MOE_SHARE = -(-MOE_TM // MOE_FF_BLOCKS)
MOE_ROWS_BUF = -(-MOE_SHARE * MOE_FF_BLOCKS // SUBLANES) * SUBLANES


def _moe_plan(e2):
    t = e2.shape[0]
    p = 2 * t
    ef = e2.reshape(p)
    onehot = (ef[:, None] == jnp.arange(N_EXPERTS, dtype=jnp.int32)[None, :]).astype(jnp.int32)
    csum = jnp.cumsum(onehot, axis=0)
    counts = csum[-1]
    rank = jnp.sum((csum - onehot) * onehot, axis=1)
    padded = ((counts + MOE_TM - 1) // MOE_TM) * MOE_TM
    ends = jnp.cumsum(padded)
    starts = ends - padded
    dst = jnp.sum(onehot * starts[None, :], axis=1) + rank
    n_tiles = p // MOE_TM + N_EXPERTS
    tile_start = jnp.arange(n_tiles, dtype=jnp.int32) * MOE_TM
    tile_expert = jnp.sum((tile_start[:, None] >= ends[None, :]).astype(jnp.int32), axis=1)
    tile_expert = jnp.minimum(tile_expert, N_EXPERTS - 1)
    n_used = (ends[-1] // MOE_TM).reshape(1).astype(jnp.int32)
    src = jnp.zeros((n_tiles * MOE_TM,), jnp.int32).at[dst].set(jnp.arange(p, dtype=jnp.int32) // 2)
    return dst.reshape(t, 2), src, tile_expert.astype(jnp.int32), n_used


def _expert_kernel(te_ref, nu_ref, src_ref, src_next_ref, x_hbm, nw_ref, wg_ref, wu_ref, wd_ref, o_ref,
                   rows_ref, sem, h_ref, acc_ref):
    i = pl.program_id(0)
    f = pl.program_id(1)
    nf = pl.num_programs(1)
    n_used = nu_ref[0]
    slot = i % 2
    share = MOE_SHARE
    issued = MOE_SHARE * MOE_FF_BLOCKS

    def gather_copy(idx_ref, r, s):
        row = idx_ref[0, 0, jnp.minimum(r, MOE_TM - 1)]
        return pltpu.make_async_copy(x_hbm.at[pl.ds(row, 1)], rows_ref.at[s, pl.ds(r, 1)], sem.at[s])

    row_copy_priority = 1

    def wait_slot(s):
        pltpu.make_async_copy(x_hbm.at[pl.ds(0, MOE_TM)], rows_ref.at[s, pl.ds(0, MOE_TM)], sem.at[s]).wait()
        for r in range(MOE_TM, issued):
            pltpu.make_async_copy(x_hbm.at[pl.ds(0, 1)], rows_ref.at[s, pl.ds(r, 1)], sem.at[s]).wait()

    @pl.when(f == 0)
    def _():
        acc_ref[...] = jnp.zeros_like(acc_ref)

    @pl.when((f == 0) & (i == 0))
    def _():
        def body(r, carry):
            gather_copy(src_ref, r, 0).start(priority=row_copy_priority)
            return carry
        lax.fori_loop(0, issued, body, 0, unroll=8)

    @pl.when((f == 0) & (i < n_used))
    def _():
        wait_slot(slot)
        h_ref[...] = _rms_h(rows_ref[slot, :MOE_TM, :], nw_ref[...]).astype(BF16)

    @pl.when(i < n_used)
    def _():
        for k in range(share):
            gather_copy(src_next_ref, f * share + k, 1 - slot).start(priority=row_copy_priority)
        h = h_ref[...]
        act = _silu(_dot(h, wg_ref[0, 0].astype(BF16))) * _dot(h, wu_ref[0, 0].astype(BF16))
        acc_ref[...] += _dot(act.astype(BF16), wd_ref[0, 0].astype(BF16))

    @pl.when((i == n_used - 1) & (f == nf - 1))
    def _():
        wait_slot(1 - slot)

    @pl.when(f == nf - 1)
    def _():
        o_ref[...] = acc_ref[...].astype(BF16)


def _experts(x, nw, src, tile_expert, n_used, wg, wu, wd, layer):
    n_tiles = tile_expert.shape[0]
    tf = MOE_TF
    nf = MOE_FF_BLOCKS
    src2 = src.reshape(n_tiles, 1, MOE_TM)

    def fblock(i, f, nu):
        return jnp.where(i < nu[0], f, nf - 1)

    smem_row = functools.partial(pl.BlockSpec, (1, 1, MOE_TM), memory_space=pltpu.SMEM)
    grid_spec = pltpu.PrefetchScalarGridSpec(
        num_scalar_prefetch=2,
        grid=(n_tiles, nf),
        in_specs=[
            smem_row(lambda i, f, te, nu: (i, 0, 0)),
            smem_row(lambda i, f, te, nu: (jnp.minimum(i + 1, n_tiles - 1), 0, 0)),
            pl.BlockSpec(memory_space=pl.ANY),
            pl.BlockSpec((1, D_MODEL), lambda i, f, te, nu: (0, 0)),
            pl.BlockSpec((1, 1, D_MODEL, tf), lambda i, f, te, nu: (layer, te[i], 0, fblock(i, f, nu))),
            pl.BlockSpec((1, 1, D_MODEL, tf), lambda i, f, te, nu: (layer, te[i], 0, fblock(i, f, nu))),
            pl.BlockSpec((1, 1, tf, D_MODEL), lambda i, f, te, nu: (layer, te[i], fblock(i, f, nu), 0)),
        ],
        out_specs=pl.BlockSpec((MOE_TM, D_MODEL), lambda i, f, te, nu: (i, 0)),
        scratch_shapes=[
            pltpu.VMEM((2, MOE_ROWS_BUF, D_MODEL), F32),
            pltpu.SemaphoreType.DMA((2,)),
            pltpu.VMEM((MOE_TM, D_MODEL), BF16),
            pltpu.VMEM((MOE_TM, D_MODEL), F32),
        ],
    )
    return pl.pallas_call(
        _expert_kernel,
        grid_spec=grid_spec,
        out_shape=jax.ShapeDtypeStruct((n_tiles * MOE_TM, D_MODEL), BF16),
        compiler_params=_cparams(("arbitrary", "arbitrary")),
        name="experts",
    )(tile_expert, n_used, src2, src2, x, nw, wg, wu, wd)


def _combine_kernel(x_ref, y1_ref, y2_ref, gate_ref, nw_ref, o_ref, *, final_norm):
    g = gate_ref[...]
    out = x_ref[...] + g[:, 0:1] * y1_ref[...].astype(F32) + g[:, 1:2] * y2_ref[...].astype(F32)
    if final_norm:
        out = _rms_h(out, nw_ref[...])
    o_ref[...] = out


def _combine(x, y1, y2, gates, nw, final_norm, tm=1024):
    t = x.shape[0]
    return pl.pallas_call(
        functools.partial(_combine_kernel, final_norm=final_norm),
        grid=(t // tm,),
        in_specs=[
            pl.BlockSpec((tm, D_MODEL), lambda m: (m, 0)),
            pl.BlockSpec((tm, D_MODEL), lambda m: (m, 0)),
            pl.BlockSpec((tm, D_MODEL), lambda m: (m, 0)),
            pl.BlockSpec((tm, LANES), lambda m: (m, 0)),
            pl.BlockSpec((1, D_MODEL), lambda m: (0, 0)),
        ],
        out_specs=pl.BlockSpec((tm, D_MODEL), lambda m: (m, 0)),
        out_shape=jax.ShapeDtypeStruct((t, D_MODEL), F32),
        compiler_params=_cparams(("parallel",)),
        name="combine",
    )(x, y1, y2, gates, nw)


def _moe_layer(xf, norm_w, router_w, wg, wu, wd, layer, final_w=None):
    rw = jnp.pad(router_w.astype(F32), ((0, 0), (0, LANES - N_EXPERTS)))
    idx, gates = _router(xf, _row(norm_w), rw)
    dst, src, tile_expert, n_used = _moe_plan(idx[:, :2])
    ys = _experts(xf, _row(norm_w), src, tile_expert, n_used, wg, wu, wd, layer)
    y1 = ys.at[dst[:, 0]].get(mode="promise_in_bounds")
    y2 = ys.at[dst[:, 1]].get(mode="promise_in_bounds")
    nw = _row(final_w) if final_w is not None else jnp.ones((1, D_MODEL), F32)
    return _combine(xf, y1, y2, gates, nw, final_w is not None)


def _final_norm_kernel(x_ref, nw_ref, o_ref):
    o_ref[...] = _rms_h(x_ref[...], nw_ref[...])


def _final_norm(x, nw, tm=1024):
    t = x.shape[0]
    return pl.pallas_call(
        _final_norm_kernel,
        grid=(t // tm,),
        in_specs=[pl.BlockSpec((tm, D_MODEL), lambda m: (m, 0)), pl.BlockSpec((1, D_MODEL), lambda m: (0, 0))],
        out_specs=pl.BlockSpec((tm, D_MODEL), lambda m: (m, 0)),
        out_shape=jax.ShapeDtypeStruct((t, D_MODEL), F32),
        compiler_params=_cparams(("parallel",)),
        name="final_norm",
    )(x, nw)


def _halves_first(w):
    rows = w.shape[0]
    w = w.reshape(rows, RET_HEADS, 2, HEAD_DIM // 2)
    return jnp.swapaxes(w, 1, 2).reshape(rows, RET_DIM)


def _layout_w_in(w_in):
    dt0 = 4 * RET_DIM + SSD_INNER + SSD_CONV_DIM
    main = jnp.concatenate([
        _halves_first(w_in[:, :RET_DIM]), _halves_first(w_in[:, RET_DIM:2 * RET_DIM]),
        w_in[:, 2 * RET_DIM:dt0], w_in[:, dt0 + SSD_HEADS:],
    ], axis=1).astype(BF16)
    wdt = jnp.pad(w_in[:, dt0:dt0 + SSD_HEADS], ((0, 0), (0, LANES - SSD_HEADS))).astype(BF16)
    return main, wdt


def _row(v, width=None):
    v = v.reshape(1, -1).astype(F32)
    if width is not None and v.shape[1] < width:
        v = jnp.pad(v, ((0, 0), (0, width - v.shape[1])))
    return v


def kernel(x, norm1_w, w_in, ret_norm_w, ssd_conv_w, ssd_conv_b, ssd_dt_bias, ssd_a_log, ssd_d, ssd_norm_w, w_out, norm2_w, ffn_w_gate, ffn_w_up, ffn_w_down, moe_router, moe_w_gate, moe_w_up, moe_w_down, final_norm_w):
    bsz, seq, _ = x.shape
    depth = w_in.shape[0]
    ret_tables = _retention_tables(seq)
    ssd_tables = _ssd_tables()
    sb_u = _stickbreak_tables()

    xf = x.reshape(bsz * seq, D_MODEL)
    for layer in range(depth):
        w_main, w_dt = _layout_w_in(w_in[layer])
        proj, dt_raw = _inproj(xf, _row(norm1_w[layer]), w_main, w_dt)
        ro = _retention(proj, ret_tables, _row(ret_norm_w[layer]), bsz, seq)
        so = _ssd(proj, dt_raw, ssd_conv_w[layer].astype(F32), _row(ssd_conv_b[layer]),
                  _row(ssd_dt_bias[layer], LANES), _row(ssd_a_log[layer], LANES),
                  _row(jnp.repeat(ssd_d[layer], HEAD_DIM)), _row(ssd_norm_w[layer]), ssd_tables, bsz, seq)
        bo = _stickbreak(proj, sb_u, bsz, seq)
        xf = _outproj(xf, ro, so, bo, w_out, layer)
        j = layer // 2
        if layer % 2 == 0:
            xf = _ffn(xf, _row(norm2_w[layer]), ffn_w_gate, ffn_w_up, ffn_w_down, j)
        else:
            last = layer == depth - 1
            xf = _moe_layer(xf, norm2_w[layer], moe_router[j], moe_w_gate, moe_w_up, moe_w_down, j,
                            final_norm_w if last else None)
    if depth % 2 == 1:
        xf = _final_norm(xf, _row(final_norm_w))
    return xf.reshape(bsz, seq, D_MODEL)
```

```python
import functools
import math

import numpy as np
import jax
import jax.numpy as jnp
from jax import lax
from jax.experimental import pallas as pl
from jax.experimental.pallas import tpu as pltpu

F32 = jnp.float32
BF16 = jnp.bfloat16

D_MODEL = 1024
HEAD_DIM = 64
RET_HEADS = 4
RET_DIM = RET_HEADS * HEAD_DIM
SSD_HEADS = 8
SSD_INNER = SSD_HEADS * HEAD_DIM
SSD_GROUPS = 2
SSD_STATE = 64
SSD_CONV = 4
SSD_BC = SSD_GROUPS * SSD_STATE
SSD_CONV_DIM = SSD_INNER + 2 * SSD_BC
SB_HEADS = 4
SB_DIM = SB_HEADS * HEAD_DIM
D_MIX = RET_DIM + SSD_INNER + SB_DIM
CHUNK = 128
D_FF = 3584
N_EXPERTS = 8
ROPE_BASE = 10000.0
EPS = 1e-6

LANES = 128
SUBLANES = 8
D_PROJ = 3072
OFF_RQ, OFF_RK, OFF_RV, OFF_RG = 0, 256, 512, 768
OFF_SZ, OFF_SXBC = 1024, 1536
OFF_BQ, OFF_BK, OFF_BV = 2304, 2560, 2816

VMEM_LIMIT = 56 * 1024 * 1024


def _cparams(sem):
    return pltpu.CompilerParams(dimension_semantics=sem, vmem_limit_bytes=VMEM_LIMIT)


def _dot(a, b):
    return jnp.dot(a, b, preferred_element_type=F32)


def _dot_nt(a, b):
    return lax.dot_general(a, b, (((1,), (1,)), ((), ())), preferred_element_type=F32)


def _split2(x):
    hi = x.astype(BF16)
    lo = (x - hi.astype(F32)).astype(BF16)
    return hi, lo


def _split3(x):
    hi = x.astype(BF16)
    r = x - hi.astype(F32)
    mid = r.astype(BF16)
    lo = (r - mid.astype(F32)).astype(BF16)
    return hi, mid, lo


def _dot_x_const(x, c):
    hi, lo = _split2(x)
    return _dot(hi, c) + _dot(lo, c)


def _silu(x):
    return x * jax.nn.sigmoid(x)


def _softplus(x):
    e = jnp.exp2(jnp.abs(x) * (-math.log2(math.e)))
    return jnp.maximum(x, 0.0) + jnp.log(1.0 + e)


def _pair_heads_matmul(m_stack, x, n_heads):
    rows = m_stack.shape[0] // n_heads
    first_of_pair = lax.broadcasted_iota(jnp.int32, (1, LANES), 1) < HEAD_DIM
    parts = []
    for p in range(n_heads // 2):
        both = _dot(m_stack[2 * p * rows:(2 * p + 2) * rows], x[:, p * LANES:(p + 1) * LANES])
        parts.append(jnp.where(first_of_pair, both[:rows], both[rows:]))
    return jnp.concatenate(parts, axis=-1)


def _rms_h(x, w):
    ms = jnp.mean(x * x, axis=-1, keepdims=True)
    return x * lax.rsqrt(ms + EPS) * w


def _inproj_kernel(x_ref, nw_ref, w_ref, wdt_ref, o_ref, dt_ref):
    h = _rms_h(x_ref[...], nw_ref[...]).astype(BF16)
    step = 512
    for c in range(D_PROJ // step):
        o_ref[:, c * step:(c + 1) * step] = _dot(h, w_ref[:, c * step:(c + 1) * step]).astype(BF16)
    dt_ref[...] = _dot(h, wdt_ref[...])


def _inproj(x, nw, w, wdt, tm=512):
    t = x.shape[0]
    return pl.pallas_call(
        _inproj_kernel,
        grid=(t // tm,),
        in_specs=[
            pl.BlockSpec((tm, D_MODEL), lambda m: (m, 0)),
            pl.BlockSpec((1, D_MODEL), lambda m: (0, 0)),
            pl.BlockSpec((D_MODEL, D_PROJ), lambda m: (0, 0)),
            pl.BlockSpec((D_MODEL, LANES), lambda m: (0, 0)),
        ],
        out_specs=[
            pl.BlockSpec((tm, D_PROJ), lambda m: (m, 0)),
            pl.BlockSpec((tm, LANES), lambda m: (m, 0)),
        ],
        out_shape=[jax.ShapeDtypeStruct((t, D_PROJ), BF16), jax.ShapeDtypeStruct((t, LANES), F32)],
        compiler_params=_cparams(("parallel",)),
        name="inproj",
    )(x, nw, w, wdt)


def _retention_kernel(q_ref, k_ref, v_ref, g_ref, cos_ref, sin_ref, intra_ref, qdec_ref, kdec_ref,
                      cd_ref, avg_ref, nw_ref, o_ref, state_ref):
    c = pl.program_id(1)

    @pl.when(c == 0)
    def _():
        state_ref[...] = jnp.zeros_like(state_ref)

    cos = cos_ref[...]
    sin = sin_ref[...]

    def rot(t):
        t1, t2 = t[:, :LANES], t[:, LANES:]
        return jnp.concatenate([t1 * cos - t2 * sin, t1 * sin + t2 * cos], axis=-1)

    lane = lax.broadcasted_iota(jnp.int32, (1, RET_DIM), 1)
    head_qk = (lane % LANES) // (HEAD_DIM // 2)
    cd = cd_ref[...]
    avg = avg_ref[...]

    for b in range(q_ref.shape[0]):
        rq = rot(q_ref[b].astype(F32))
        rk = rot(k_ref[b].astype(F32))
        vb = v_ref[b]

        q_stack = jnp.concatenate([jnp.where(head_qk == h, rq, 0.0) for h in range(RET_HEADS)], axis=0)
        scores = _dot_nt(q_stack.astype(BF16), rk.astype(BF16)) * intra_ref[...]
        y = _pair_heads_matmul(scores.astype(BF16), vb, RET_HEADS)

        state = state_ref[b]
        y = y + _dot((rq * qdec_ref[...]).astype(BF16), state.astype(BF16))
        kd = (rk * kdec_ref[...]).astype(BF16)
        kv = lax.dot_general(kd, vb, (((0,), (0,)), ((), ())), preferred_element_type=F32)
        state_ref[b] = state * cd + jnp.where(cd != 0.0, kv, 0.0)

        mu = _dot_x_const(y, avg)
        d = y - mu
        var = _dot_x_const(d * d, avg)
        yn = d * lax.rsqrt(var + EPS)
        o_ref[b] = (yn * nw_ref[...] * _silu(g_ref[b].astype(F32))).astype(BF16)


def _retention_tables(seq):
    f = np.float32
    half = HEAD_DIM // 2
    inv = (1.0 / (f(ROPE_BASE) ** (np.arange(half, dtype=f) / f(half)))).astype(f)
    ang = np.arange(seq, dtype=f)[:, None] * inv[None, :]
    cos = np.tile(np.cos(ang).astype(f), (1, RET_HEADS))
    sin = np.tile(np.sin(ang).astype(f), (1, RET_HEADS))
    log_gamma = np.log(f(1.0) - f(2.0) ** (f(-5.0) - np.arange(RET_HEADS, dtype=f))).astype(f)
    idx = np.arange(CHUNK, dtype=f)
    diff = idx[:, None] - idx[None, :]
    scale = f(HEAD_DIM ** -0.5)
    intra = np.where(diff >= 0, np.exp(log_gamma[:, None, None] * np.maximum(diff, 0.0)), 0.0).astype(f)
    intra = (intra * scale).reshape(RET_HEADS * CHUNK, CHUNK)
    head_qk = (np.arange(RET_DIM) % LANES) // half
    head_v = np.arange(RET_DIM) // HEAD_DIM
    qdec = np.exp(log_gamma[head_qk][None, :] * (idx[:, None] + 1.0)).astype(f)
    kdec = (np.exp(log_gamma[head_qk][None, :] * (CHUNK - 1.0 - idx[:, None])) * scale).astype(f)
    chunk_decay = np.exp(log_gamma * f(CHUNK)).astype(f)
    same = head_qk[:, None] == head_v[None, :]
    cd = np.where(same, chunk_decay[head_qk][:, None], 0.0).astype(f)
    avg = np.where(head_v[:, None] == head_v[None, :], 1.0 / HEAD_DIM, 0.0).astype(f)
    return (jnp.asarray(cos), jnp.asarray(sin), jnp.asarray(intra), jnp.asarray(qdec), jnp.asarray(kdec),
            jnp.asarray(cd), jnp.asarray(avg, dtype=BF16))


MIXER_GROUP = 4


def _retention(proj, tables, nw, bsz, seq):
    nc = seq // CHUNK
    cos, sin, intra, qdec, kdec, cd, avg = tables
    w = RET_DIM
    grp = MIXER_GROUP if bsz % MIXER_GROUP == 0 else 1
    proj3 = proj.reshape(bsz, seq, D_PROJ)

    def col(j):
        return pl.BlockSpec((grp, CHUNK, w), lambda b, c: (b, c, j))

    def const(shape):
        return pl.BlockSpec(shape, lambda b, c: (0, 0))

    out = pl.pallas_call(
        _retention_kernel,
        grid=(bsz // grp, nc),
        in_specs=[
            col(OFF_RQ // w), col(OFF_RK // w), col(OFF_RV // w), col(OFF_RG // w),
            pl.BlockSpec((CHUNK, LANES), lambda b, c: (c, 0)),
            pl.BlockSpec((CHUNK, LANES), lambda b, c: (c, 0)),
            const((RET_HEADS * CHUNK, CHUNK)), const((CHUNK, w)), const((CHUNK, w)),
            const((w, w)), const((w, w)), const((1, w)),
        ],
        out_specs=pl.BlockSpec((grp, CHUNK, w), lambda b, c: (b, c, 0)),
        out_shape=jax.ShapeDtypeStruct((bsz, seq, w), BF16),
        scratch_shapes=[pltpu.VMEM((grp, w, w), F32)],
        compiler_params=_cparams(("parallel", "arbitrary")),
        name="retention",
    )(proj3, proj3, proj3, proj3, cos, sin, intra, qdec, kdec, cd, avg, nw)
    return out.reshape(bsz * seq, w)


def _ssd_kernel(z_ref, xbc_ref, dt_ref, cw_ref, cb_ref, dtb_ref, alog_ref, dskip_ref, nw_ref,
                tri_ref, exp_ref, o_ref, prev_ref, state_ref):
    c = pl.program_id(1)

    @pl.when(c == 0)
    def _():
        prev_ref[...] = jnp.zeros_like(prev_ref)
        state_ref[...] = jnp.zeros_like(state_ref)

    row = lax.broadcasted_iota(jnp.int32, (SUBLANES, 1), 0)
    tri = tri_ref[...]
    lane_bc = lax.broadcasted_iota(jnp.int32, (1, SSD_BC), 1)
    lower = (lax.broadcasted_iota(jnp.int32, (CHUNK, CHUNK), 0)
             >= lax.broadcasted_iota(jnp.int32, (CHUNK, CHUNK), 1))
    heads_per_group = SSD_HEADS // SSD_GROUPS
    lane_x = lax.broadcasted_iota(jnp.int32, (1, SSD_INNER), 1)
    row_g = lax.broadcasted_iota(jnp.int32, (SSD_BC, 1), 0) // SSD_STATE
    same_group = row_g == lane_x // (SSD_INNER // SSD_GROUPS)
    gw = SSD_INNER // SSD_GROUPS

    for b in range(z_ref.shape[0]):
        cur = xbc_ref[b].astype(F32)
        tail = prev_ref[b]
        acc = cur * cw_ref[SSD_CONV - 1:SSD_CONV, :] + cb_ref[...]
        for j in range(1, SSD_CONV):
            rolled = pltpu.roll(cur, j, 0)
            top = jnp.where(row >= j, rolled[:SUBLANES], pltpu.roll(tail, j, 0))
            shifted = jnp.concatenate([top, rolled[SUBLANES:]], axis=0)
            acc = acc + shifted * cw_ref[SSD_CONV - 1 - j:SSD_CONV - j, :]
        prev_ref[b] = cur[CHUNK - SUBLANES:]
        xa = _silu(acc)
        x = xa[:, :SSD_INNER]
        bm = xa[:, SSD_INNER:SSD_INNER + SSD_BC]
        cm = xa[:, SSD_INNER + SSD_BC:]

        dt = _softplus(dt_ref[b] + dtb_ref[...])
        a = dt * (-jnp.exp(alog_ref[...]))
        a_hi, a_mid, a_lo = _split3(a)
        a_cs = _dot(tri, a_hi) + _dot(tri, a_mid) + _dot(tri, a_lo)
        a_cs_t = a_cs.T
        stack = jnp.concatenate([dt, jnp.exp(a_cs), jnp.exp(a_cs[CHUNK - 1:CHUNK, :] - a_cs)], axis=0)
        stack_x = _dot_x_const(stack, exp_ref[...])
        dt_x = stack_x[:CHUNK]
        ea_x = stack_x[CHUNK:2 * CHUNK]
        dec_x = stack_x[2 * CHUNK:]

        xdt = x * dt_x
        xdt_b = xdt.astype(BF16)
        bm_b = bm.astype(BF16)
        cm_b = cm.astype(BF16)
        cb = [_dot_nt(jnp.where(lane_bc // SSD_STATE == g, cm, 0.0).astype(BF16), bm_b)
              for g in range(SSD_GROUPS)]

        m_list = []
        for h in range(SSD_HEADS):
            seg = a_cs[:, h:h + 1] - a_cs_t[h:h + 1, :]
            l_mat = jnp.exp(jnp.where(lower, seg, -jnp.inf))
            m_list.append((cb[h // heads_per_group] * l_mat).astype(BF16))
        y = _pair_heads_matmul(jnp.concatenate(m_list, axis=0), xdt_b, SSD_HEADS)

        state = state_ref[b]
        y = y + _dot(cm_b, state.astype(BF16)) * ea_x
        contrib = _dot(bm.T.astype(BF16), (xdt * dec_x).astype(BF16))
        state_ref[b] = state * ea_x[CHUNK - 1:CHUNK, :] + jnp.where(same_group, contrib, 0.0)

        y = y + dskip_ref[...] * x
        gy = y * _silu(z_ref[b].astype(F32))
        outs = []
        for g in range(SSD_GROUPS):
            part = gy[:, g * gw:(g + 1) * gw]
            outs.append(part * lax.rsqrt(jnp.mean(part * part, axis=-1, keepdims=True) + EPS))
        o_ref[b] = (jnp.concatenate(outs, axis=-1) * nw_ref[...]).astype(BF16)


def _ssd_tables():
    idx = np.arange(CHUNK)
    tri = (idx[:, None] >= idx[None, :]).astype(np.float32)
    expand = np.zeros((LANES, SSD_INNER), np.float32)
    for h in range(SSD_HEADS):
        expand[h, h * HEAD_DIM:(h + 1) * HEAD_DIM] = 1.0
    return jnp.asarray(tri, dtype=BF16), jnp.asarray(expand, dtype=BF16)


def _ssd(proj, dt_raw, conv_w, conv_b, dt_bias, a_log, d_skip, nw, tables, bsz, seq):
    nc = seq // CHUNK
    tri, expand = tables

    grp = MIXER_GROUP if bsz % MIXER_GROUP == 0 else 1
    proj3 = proj.reshape(bsz, seq, D_PROJ)
    dt3 = dt_raw.reshape(bsz, seq, LANES)

    def const(shape):
        return pl.BlockSpec(shape, lambda b, c: (0, 0))

    out = pl.pallas_call(
        _ssd_kernel,
        grid=(bsz // grp, nc),
        in_specs=[
            pl.BlockSpec((grp, CHUNK, SSD_INNER), lambda b, c: (b, c, OFF_SZ // SSD_INNER)),
            pl.BlockSpec((grp, CHUNK, SSD_CONV_DIM), lambda b, c: (b, c, OFF_SXBC // SSD_CONV_DIM)),
            pl.BlockSpec((grp, CHUNK, LANES), lambda b, c: (b, c, 0)),
            const((SSD_CONV, SSD_CONV_DIM)), const((1, SSD_CONV_DIM)), const((1, LANES)), const((1, LANES)),
            const((1, SSD_INNER)), const((1, SSD_INNER)), const((CHUNK, CHUNK)), const((LANES, SSD_INNER)),
        ],
        out_specs=pl.BlockSpec((grp, CHUNK, SSD_INNER), lambda b, c: (b, c, 0)),
        out_shape=jax.ShapeDtypeStruct((bsz, seq, SSD_INNER), BF16),
        scratch_shapes=[pltpu.VMEM((grp, SUBLANES, SSD_CONV_DIM), F32), pltpu.VMEM((grp, SSD_BC, SSD_INNER), F32)],
        compiler_params=_cparams(("parallel", "arbitrary")),
        name="ssd",
    )(proj3, proj3, dt3, conv_w, conv_b, dt_bias, a_log, d_skip, nw, tri, expand)
    return out.reshape(bsz * seq, SSD_INNER)


SB_BLOCK = 256
SB_GROUP = 4


def _stickbreak_kernel(q_ref, k_ref, v_ref, u_ref, o_ref, acc_ref):
    i = pl.program_id(1)
    blk = SB_BLOCK
    grp = q_ref.shape[0]
    lane = lax.broadcasted_iota(jnp.int32, (1, SB_DIM), 1)
    head = lane // HEAD_DIM
    rows = SB_HEADS * blk
    q_stack = []
    for b in range(grp):
        q = q_ref[b] * jnp.asarray(HEAD_DIM ** -0.5, BF16)
        q_stack.append(jnp.concatenate(
            [jnp.where(head == h, q, jnp.zeros_like(q)) for h in range(SB_HEADS)], axis=0))
    u = u_ref[...]
    strict_lower = (lax.broadcasted_iota(jnp.int32, (rows, blk), 1)
                    < lax.broadcasted_iota(jnp.int32, (rows, blk), 0) % blk)

    def block(j, carries, diagonal):
        start = pl.multiple_of(j * blk, blk)
        new = []
        for b in range(grp):
            kb = k_ref[b, pl.ds(start, blk), :]
            vb = v_ref[b, pl.ds(start, blk), :]
            z = _dot_nt(q_stack[b], kb)
            sp = _softplus(z)
            if diagonal:
                sp = jnp.where(strict_lower, sp, 0.0)
            cs = _dot(sp.astype(BF16), u)
            w = jnp.exp(z - cs - carries[b])
            if diagonal:
                w = jnp.where(strict_lower, w, 0.0)
            wb = w.astype(BF16)
            for pair in range(SB_HEADS // 2):
                pv = _dot(wb[pair * 2 * blk:(pair + 1) * 2 * blk], vb[:, pair * LANES:(pair + 1) * LANES])
                if diagonal:
                    acc_ref[b, pair] = pv
                else:
                    acc_ref[b, pair] += pv
            new.append(carries[b] + jnp.sum(sp, axis=-1, keepdims=True))
        return tuple(new)

    zero = jnp.zeros((rows, 1), F32)
    carries = block(i, (zero,) * grp, True)
    lax.fori_loop(0, i, lambda jj, c: block(i - 1 - jj, c, False), carries)
    first_of_pair = lax.broadcasted_iota(jnp.int32, (1, LANES), 1) < HEAD_DIM
    for b in range(grp):
        out = [jnp.where(first_of_pair, acc_ref[b, p, :blk], acc_ref[b, p, blk:]) for p in range(SB_HEADS // 2)]
        o_ref[b] = jnp.concatenate(out, axis=-1).astype(BF16)


def _stickbreak_tables():
    idx = np.arange(SB_BLOCK)
    rev = (idx[:, None] >= idx[None, :]).astype(np.float32)
    return jnp.asarray(rev, dtype=BF16)


def _stickbreak(proj, u, bsz, seq):
    nq = seq // SB_BLOCK
    w = SB_DIM
    grp = SB_GROUP if bsz % SB_GROUP == 0 else 1
    proj3 = proj.reshape(bsz, seq, D_PROJ)
    out = pl.pallas_call(
        _stickbreak_kernel,
        grid=(bsz // grp, nq),
        in_specs=[
            pl.BlockSpec((grp, SB_BLOCK, w), lambda b, i: (b, i, OFF_BQ // w)),
            pl.BlockSpec((grp, seq, w), lambda b, i: (b, 0, OFF_BK // w)),
            pl.BlockSpec((grp, seq, w), lambda b, i: (b, 0, OFF_BV // w)),
            pl.BlockSpec((SB_BLOCK, SB_BLOCK), lambda b, i: (0, 0)),
        ],
        out_specs=pl.BlockSpec((grp, SB_BLOCK, w), lambda b, i: (b, i, 0)),
        out_shape=jax.ShapeDtypeStruct((bsz, seq, w), BF16),
        scratch_shapes=[pltpu.VMEM((grp, SB_HEADS // 2, 2 * SB_BLOCK, LANES), F32)],
        compiler_params=_cparams(("parallel", "arbitrary")),
        name="stickbreak",
    )(proj3, proj3, proj3, u)
    return out.reshape(bsz * seq, w)


def _outproj_kernel(x_ref, ro_ref, so_ref, bo_ref, w_ref, o_ref, wb_ref):
    @pl.when(pl.program_id(0) == 0)
    def _():
        wb_ref[...] = w_ref[0].astype(BF16)

    acc = _dot(ro_ref[...], wb_ref[:RET_DIM, :])
    acc = acc + _dot(so_ref[...], wb_ref[RET_DIM:RET_DIM + SSD_INNER, :])
    acc = acc + _dot(bo_ref[...], wb_ref[RET_DIM + SSD_INNER:, :])
    o_ref[...] = x_ref[...] + acc


def _outproj(x, ro, so, bo, w, layer, tm=512):
    t = x.shape[0]
    return pl.pallas_call(
        _outproj_kernel,
        grid=(t // tm,),
        in_specs=[
            pl.BlockSpec((tm, D_MODEL), lambda m: (m, 0)),
            pl.BlockSpec((tm, RET_DIM), lambda m: (m, 0)),
            pl.BlockSpec((tm, SSD_INNER), lambda m: (m, 0)),
            pl.BlockSpec((tm, SB_DIM), lambda m: (m, 0)),
            pl.BlockSpec((1, D_MIX, D_MODEL), lambda m: (layer, 0, 0)),
        ],
        out_specs=pl.BlockSpec((tm, D_MODEL), lambda m: (m, 0)),
        out_shape=jax.ShapeDtypeStruct((t, D_MODEL), F32),
        scratch_shapes=[pltpu.VMEM((D_MIX, D_MODEL), BF16)],
        compiler_params=_cparams(("arbitrary",)),
        name="outproj",
    )(x, ro, so, bo, w)


def _ffn_kernel(x_ref, nw_ref, wg_ref, wu_ref, wd_ref, o_ref, h_ref, acc_ref):
    f = pl.program_id(1)

    @pl.when(f == 0)
    def _():
        h_ref[...] = _rms_h(x_ref[...], nw_ref[...]).astype(BF16)
        acc_ref[...] = jnp.zeros_like(acc_ref)

    h = h_ref[...]
    act = _silu(_dot(h, wg_ref[0].astype(BF16))) * _dot(h, wu_ref[0].astype(BF16))
    acc_ref[...] += _dot(act.astype(BF16), wd_ref[0].astype(BF16))

    @pl.when(f == pl.num_programs(1) - 1)
    def _():
        o_ref[...] = x_ref[...] + acc_ref[...]


def _ffn(x, nw, wg, wu, wd, layer, tm=1024, tf=512):
    t = x.shape[0]
    return pl.pallas_call(
        _ffn_kernel,
        grid=(t // tm, D_FF // tf),
        in_specs=[
            pl.BlockSpec((tm, D_MODEL), lambda m, f: (m, 0)),
            pl.BlockSpec((1, D_MODEL), lambda m, f: (0, 0)),
            pl.BlockSpec((1, D_MODEL, tf), lambda m, f: (layer, 0, f)),
            pl.BlockSpec((1, D_MODEL, tf), lambda m, f: (layer, 0, f)),
            pl.BlockSpec((1, tf, D_MODEL), lambda m, f: (layer, f, 0)),
        ],
        out_specs=pl.BlockSpec((tm, D_MODEL), lambda m, f: (m, 0)),
        out_shape=jax.ShapeDtypeStruct((t, D_MODEL), F32),
        scratch_shapes=[pltpu.VMEM((tm, D_MODEL), BF16), pltpu.VMEM((tm, D_MODEL), F32)],
        compiler_params=_cparams(("parallel", "arbitrary")),
        name="ffn",
    )(x, nw, wg, wu, wd)


def _router_kernel(x_ref, nw_ref, rw_ref, idx_ref, gate_ref):
    h = _rms_h(x_ref[...], nw_ref[...])
    h_hi, h_lo = _split2(h)
    w_hi, w_lo = _split2(rw_ref[...])
    logits = _dot(h_hi, w_hi) + _dot(h_hi, w_lo) + _dot(h_lo, w_hi)
    lane = lax.broadcasted_iota(jnp.int32, (1, LANES), 1)
    lg = jnp.where(lane < N_EXPERTS, logits, -jnp.inf)
    m1 = jnp.max(lg, axis=-1, keepdims=True)
    i1 = jnp.min(jnp.where(lg == m1, lane, LANES), axis=-1, keepdims=True)
    lg2 = jnp.where(lane == i1, -jnp.inf, lg)
    m2 = jnp.max(lg2, axis=-1, keepdims=True)
    i2 = jnp.min(jnp.where(lg2 == m2, lane, LANES), axis=-1, keepdims=True)
    e = jnp.exp(m2 - m1)
    g1 = 1.0 / (1.0 + e)
    g2 = e / (1.0 + e)
    idx_ref[...] = jnp.where(lane == 0, i1, jnp.where(lane == 1, i2, 0))
    gate_ref[...] = jnp.where(lane == 0, g1, jnp.where(lane == 1, g2, 0.0))


def _router(x, nw, rw, tm=512):
    t = x.shape[0]
    return pl.pallas_call(
        _router_kernel,
        grid=(t // tm,),
        in_specs=[
            pl.BlockSpec((tm, D_MODEL), lambda m: (m, 0)),
            pl.BlockSpec((1, D_MODEL), lambda m: (0, 0)),
            pl.BlockSpec((D_MODEL, LANES), lambda m: (0, 0)),
        ],
        out_specs=[
            pl.BlockSpec((tm, LANES), lambda m: (m, 0)),
            pl.BlockSpec((tm, LANES), lambda m: (m, 0)),
        ],
        out_shape=[jax.ShapeDtypeStruct((t, LANES), jnp.int32), jax.ShapeDtypeStruct((t, LANES), F32)],
        compiler_params=_cparams(("parallel",)),
        name="router",
    )(x, nw, rw)


MOE_TM = 1024
MOE_TF = 512
MOE_FF_BLOCKS = D_FF // MOE_TF
MOE_SHARE = -(-MOE_TM // MOE_FF_BLOCKS)
MOE_ROWS_BUF = -(-MOE_SHARE * MOE_FF_BLOCKS // SUBLANES) * SUBLANES


def _moe_plan(e2):
    t = e2.shape[0]
    p = 2 * t
    ef = e2.reshape(p)
    onehot = (ef[:, None] == jnp.arange(N_EXPERTS, dtype=jnp.int32)[None, :]).astype(jnp.int32)
    csum = jnp.cumsum(onehot, axis=0)
    counts = csum[-1]
    rank = jnp.sum((csum - onehot) * onehot, axis=1)
    padded = ((counts + MOE_TM - 1) // MOE_TM) * MOE_TM
    ends = jnp.cumsum(padded)
    starts = ends - padded
    dst = jnp.sum(onehot * starts[None, :], axis=1) + rank
    n_tiles = p // MOE_TM + N_EXPERTS
    tile_start = jnp.arange(n_tiles, dtype=jnp.int32) * MOE_TM
    tile_expert = jnp.sum((tile_start[:, None] >= ends[None, :]).astype(jnp.int32), axis=1)
    tile_expert = jnp.minimum(tile_expert, N_EXPERTS - 1)
    n_used = (ends[-1] // MOE_TM).reshape(1).astype(jnp.int32)
    src = jnp.zeros((n_tiles * MOE_TM,), jnp.int32).at[dst].set(
        jnp.arange(p, dtype=jnp.int32) // 2, unique_indices=True, mode="promise_in_bounds")
    return dst.reshape(t, 2), src, tile_expert.astype(jnp.int32), n_used


def _expert_kernel(te_ref, nu_ref, src_ref, src_next_ref, x_hbm, nw_ref, wg_ref, wu_ref, wd_ref, o_ref,
                   rows_ref, sem, h_ref, acc_ref):
    i = pl.program_id(0)
    f = pl.program_id(1)
    nf = pl.num_programs(1)
    n_used = nu_ref[0]
    slot = i % 2
    share = MOE_SHARE
    issued = MOE_SHARE * MOE_FF_BLOCKS

    def gather_copy(idx_ref, r, s):
        row = idx_ref[0, 0, jnp.minimum(r, MOE_TM - 1)]
        return pltpu.make_async_copy(x_hbm.at[pl.ds(row, 1)], rows_ref.at[s, pl.ds(r, 1)], sem.at[s])

    row_copy_priority = 1

    def wait_slot(s):
        pltpu.make_async_copy(x_hbm.at[pl.ds(0, MOE_TM)], rows_ref.at[s, pl.ds(0, MOE_TM)], sem.at[s]).wait()
        for r in range(MOE_TM, issued):
            pltpu.make_async_copy(x_hbm.at[pl.ds(0, 1)], rows_ref.at[s, pl.ds(r, 1)], sem.at[s]).wait()

    @pl.when(f == 0)
    def _():
        acc_ref[...] = jnp.zeros_like(acc_ref)

    @pl.when((f == 0) & (i == 0))
    def _():
        def body(r, carry):
            gather_copy(src_ref, r, 0).start(priority=row_copy_priority)
            return carry
        lax.fori_loop(0, issued, body, 0, unroll=8)

    @pl.when((f == 0) & (i < n_used))
    def _():
        wait_slot(slot)
        h_ref[...] = _rms_h(rows_ref[slot, :MOE_TM, :], nw_ref[...]).astype(BF16)

    @pl.when(i < n_used)
    def _():
        for k in range(share):
            gather_copy(src_next_ref, f * share + k, 1 - slot).start(priority=row_copy_priority)
        h = h_ref[...]
        act = _silu(_dot(h, wg_ref[0, 0].astype(BF16))) * _dot(h, wu_ref[0, 0].astype(BF16))
        acc_ref[...] += _dot(act.astype(BF16), wd_ref[0, 0].astype(BF16))

    @pl.when((i == n_used - 1) & (f == nf - 1))
    def _():
        wait_slot(1 - slot)

    @pl.when(f == nf - 1)
    def _():
        o_ref[...] = acc_ref[...].astype(BF16)


def _experts(x, nw, src, tile_expert, n_used, wg, wu, wd, layer):
    n_tiles = tile_expert.shape[0]
    tf = MOE_TF
    nf = MOE_FF_BLOCKS
    src2 = src.reshape(n_tiles, 1, MOE_TM)

    def fblock(i, f, nu):
        return jnp.where(i < nu[0], f, nf - 1)

    smem_row = functools.partial(pl.BlockSpec, (1, 1, MOE_TM), memory_space=pltpu.SMEM)
    grid_spec = pltpu.PrefetchScalarGridSpec(
        num_scalar_prefetch=2,
        grid=(n_tiles, nf),
        in_specs=[
            smem_row(lambda i, f, te, nu: (i, 0, 0)),
            smem_row(lambda i, f, te, nu: (jnp.minimum(i + 1, n_tiles - 1), 0, 0)),
            pl.BlockSpec(memory_space=pl.ANY),
            pl.BlockSpec((1, D_MODEL), lambda i, f, te, nu: (0, 0)),
            pl.BlockSpec((1, 1, D_MODEL, tf), lambda i, f, te, nu: (layer, te[i], 0, fblock(i, f, nu))),
            pl.BlockSpec((1, 1, D_MODEL, tf), lambda i, f, te, nu: (layer, te[i], 0, fblock(i, f, nu))),
            pl.BlockSpec((1, 1, tf, D_MODEL), lambda i, f, te, nu: (layer, te[i], fblock(i, f, nu), 0)),
        ],
        out_specs=pl.BlockSpec((MOE_TM, D_MODEL), lambda i, f, te, nu: (i, 0)),
        scratch_shapes=[
            pltpu.VMEM((2, MOE_ROWS_BUF, D_MODEL), F32),
            pltpu.SemaphoreType.DMA((2,)),
            pltpu.VMEM((MOE_TM, D_MODEL), BF16),
            pltpu.VMEM((MOE_TM, D_MODEL), F32),
        ],
    )
    return pl.pallas_call(
        _expert_kernel,
        grid_spec=grid_spec,
        out_shape=jax.ShapeDtypeStruct((n_tiles * MOE_TM, D_MODEL), BF16),
        compiler_params=_cparams(("arbitrary", "arbitrary")),
        name="experts",
    )(tile_expert, n_used, src2, src2, x, nw, wg, wu, wd)


def _combine_kernel(x_ref, y1_ref, y2_ref, gate_ref, nw_ref, o_ref, *, final_norm):
    g = gate_ref[...]
    out = x_ref[...] + g[:, 0:1] * y1_ref[...].astype(F32) + g[:, 1:2] * y2_ref[...].astype(F32)
    if final_norm:
        out = _rms_h(out, nw_ref[...])
    o_ref[...] = out


def _combine(x, y1, y2, gates, nw, final_norm, tm=1024):
    t = x.shape[0]
    return pl.pallas_call(
        functools.partial(_combine_kernel, final_norm=final_norm),
        grid=(t // tm,),
        in_specs=[
            pl.BlockSpec((tm, D_MODEL), lambda m: (m, 0)),
            pl.BlockSpec((tm, D_MODEL), lambda m: (m, 0)),
            pl.BlockSpec((tm, D_MODEL), lambda m: (m, 0)),
            pl.BlockSpec((tm, LANES), lambda m: (m, 0)),
            pl.BlockSpec((1, D_MODEL), lambda m: (0, 0)),
        ],
        out_specs=pl.BlockSpec((tm, D_MODEL), lambda m: (m, 0)),
        out_shape=jax.ShapeDtypeStruct((t, D_MODEL), F32),
        compiler_params=_cparams(("parallel",)),
        name="combine",
    )(x, y1, y2, gates, nw)


def _moe_layer(xf, norm_w, router_w, wg, wu, wd, layer, final_w=None):
    rw = jnp.pad(router_w.astype(F32), ((0, 0), (0, LANES - N_EXPERTS)))
    idx, gates = _router(xf, _row(norm_w), rw)
    dst, src, tile_expert, n_used = _moe_plan(idx[:, :2])
    ys = _experts(xf, _row(norm_w), src, tile_expert, n_used, wg, wu, wd, layer)
    y1 = ys.at[dst[:, 0]].get(mode="promise_in_bounds")
    y2 = ys.at[dst[:, 1]].get(mode="promise_in_bounds")
    nw = _row(final_w) if final_w is not None else jnp.ones((1, D_MODEL), F32)
    return _combine(xf, y1, y2, gates, nw, final_w is not None)


def _final_norm_kernel(x_ref, nw_ref, o_ref):
    o_ref[...] = _rms_h(x_ref[...], nw_ref[...])


def _final_norm(x, nw, tm=1024):
    t = x.shape[0]
    return pl.pallas_call(
        _final_norm_kernel,
        grid=(t // tm,),
        in_specs=[pl.BlockSpec((tm, D_MODEL), lambda m: (m, 0)), pl.BlockSpec((1, D_MODEL), lambda m: (0, 0))],
        out_specs=pl.BlockSpec((tm, D_MODEL), lambda m: (m, 0)),
        out_shape=jax.ShapeDtypeStruct((t, D_MODEL), F32),
        compiler_params=_cparams(("parallel",)),
        name="final_norm",
    )(x, nw)


def _halves_first(w):
    rows = w.shape[0]
    w = w.reshape(rows, RET_HEADS, 2, HEAD_DIM // 2)
    return jnp.swapaxes(w, 1, 2).reshape(rows, RET_DIM)


def _layout_w_in(w_in):
    dt0 = 4 * RET_DIM + SSD_INNER + SSD_CONV_DIM
    main = jnp.concatenate([
        _halves_first(w_in[:, :RET_DIM]), _halves_first(w_in[:, RET_DIM:2 * RET_DIM]),
        w_in[:, 2 * RET_DIM:dt0], w_in[:, dt0 + SSD_HEADS:],
    ], axis=1).astype(BF16)
    wdt = jnp.pad(w_in[:, dt0:dt0 + SSD_HEADS], ((0, 0), (0, LANES - SSD_HEADS))).astype(BF16)
    return main, wdt


def _row(v, width=None):
    v = v.reshape(1, -1).astype(F32)
    if width is not None and v.shape[1] < width:
        v = jnp.pad(v, ((0, 0), (0, width - v.shape[1])))
    return v


def kernel(x, norm1_w, w_in, ret_norm_w, ssd_conv_w, ssd_conv_b, ssd_dt_bias, ssd_a_log, ssd_d, ssd_norm_w, w_out, norm2_w, ffn_w_gate, ffn_w_up, ffn_w_down, moe_router, moe_w_gate, moe_w_up, moe_w_down, final_norm_w):
    bsz, seq, _ = x.shape
    depth = w_in.shape[0]
    ret_tables = _retention_tables(seq)
    ssd_tables = _ssd_tables()
    sb_u = _stickbreak_tables()

    xf = x.reshape(bsz * seq, D_MODEL)
    for layer in range(depth):
        w_main, w_dt = _layout_w_in(w_in[layer])
        proj, dt_raw = _inproj(xf, _row(norm1_w[layer]), w_main, w_dt)
        ro = _retention(proj, ret_tables, _row(ret_norm_w[layer]), bsz, seq)
        so = _ssd(proj, dt_raw, ssd_conv_w[layer].astype(F32), _row(ssd_conv_b[layer]),
                  _row(ssd_dt_bias[layer], LANES), _row(ssd_a_log[layer], LANES),
                  _row(jnp.repeat(ssd_d[layer], HEAD_DIM)), _row(ssd_norm_w[layer]), ssd_tables, bsz, seq)
        bo = _stickbreak(proj, sb_u, bsz, seq)
        xf = _outproj(xf, ro, so, bo, w_out, layer)
        j = layer // 2
        if layer % 2 == 0:
            xf = _ffn(xf, _row(norm2_w[layer]), ffn_w_gate, ffn_w_up, ffn_w_down, j)
        else:
            last = layer == depth - 1
            xf = _moe_layer(xf, norm2_w[layer], moe_router[j], moe_w_gate, moe_w_up, moe_w_down, j,
                            final_norm_w if last else None)
    if depth % 2 == 1:
        xf = _final_norm(xf, _row(final_norm_w))
    return xf.reshape(bsz, seq, D_MODEL)
```

```python
import functools
import math

import numpy as np
import jax
import jax.numpy as jnp
from jax import lax
from jax.experimental import pallas as pl
from jax.experimental.pallas import tpu as pltpu

F32 = jnp.float32
BF16 = jnp.bfloat16

D_MODEL = 1024
HEAD_DIM = 64
RET_HEADS = 4
RET_DIM = RET_HEADS * HEAD_DIM
SSD_HEADS = 8
SSD_INNER = SSD_HEADS * HEAD_DIM
SSD_GROUPS = 2
SSD_STATE = 64
SSD_CONV = 4
SSD_BC = SSD_GROUPS * SSD_STATE
SSD_CONV_DIM = SSD_INNER + 2 * SSD_BC
SB_HEADS = 4
SB_DIM = SB_HEADS * HEAD_DIM
D_MIX = RET_DIM + SSD_INNER + SB_DIM
CHUNK = 128
D_FF = 3584
N_EXPERTS = 8
ROPE_BASE = 10000.0
EPS = 1e-6

LANES = 128
SUBLANES = 8
D_PROJ = 3072
OFF_RQ, OFF_RK, OFF_RV, OFF_RG = 0, 256, 512, 768
OFF_SZ, OFF_SXBC = 1024, 1536
OFF_BQ, OFF_BK, OFF_BV = 2304, 2560, 2816

VMEM_LIMIT = 56 * 1024 * 1024


def _cparams(sem):
    return pltpu.CompilerParams(dimension_semantics=sem, vmem_limit_bytes=VMEM_LIMIT)


def _dot(a, b):
    return jnp.dot(a, b, preferred_element_type=F32)


def _dot_nt(a, b):
    return lax.dot_general(a, b, (((1,), (1,)), ((), ())), preferred_element_type=F32)


def _split2(x):
    hi = x.astype(BF16)
    lo = (x - hi.astype(F32)).astype(BF16)
    return hi, lo


def _split3(x):
    hi = x.astype(BF16)
    r = x - hi.astype(F32)
    mid = r.astype(BF16)
    lo = (r - mid.astype(F32)).astype(BF16)
    return hi, mid, lo


def _dot_x_const(x, c):
    hi, lo = _split2(x)
    return _dot(hi, c) + _dot(lo, c)


def _silu(x):
    return x * jax.nn.sigmoid(x)


def _softplus(x):
    e = jnp.exp2(jnp.abs(x) * (-math.log2(math.e)))
    return jnp.maximum(x, 0.0) + jnp.log(1.0 + e)


def _pair_heads_matmul(m_stack, x, n_heads):
    rows = m_stack.shape[0] // n_heads
    first_of_pair = lax.broadcasted_iota(jnp.int32, (1, LANES), 1) < HEAD_DIM
    parts = []
    for p in range(n_heads // 2):
        both = _dot(m_stack[2 * p * rows:(2 * p + 2) * rows], x[:, p * LANES:(p + 1) * LANES])
        parts.append(jnp.where(first_of_pair, both[:rows], both[rows:]))
    return jnp.concatenate(parts, axis=-1)


def _rms_h(x, w):
    ms = jnp.mean(x * x, axis=-1, keepdims=True)
    return x * lax.rsqrt(ms + EPS) * w


def _inproj_kernel(x_ref, nw_ref, w_ref, wdt_ref, o_ref, dt_ref):
    h = _rms_h(x_ref[...], nw_ref[...]).astype(BF16)
    step = 512
    for c in range(D_PROJ // step):
        o_ref[:, c * step:(c + 1) * step] = _dot(h, w_ref[:, c * step:(c + 1) * step]).astype(BF16)
    dt_ref[...] = _dot(h, wdt_ref[...])


def _inproj(x, nw, w, wdt, tm=1024):
    t = x.shape[0]
    return pl.pallas_call(
        _inproj_kernel,
        grid=(t // tm,),
        in_specs=[
            pl.BlockSpec((tm, D_MODEL), lambda m: (m, 0)),
            pl.BlockSpec((1, D_MODEL), lambda m: (0, 0)),
            pl.BlockSpec((D_MODEL, D_PROJ), lambda m: (0, 0)),
            pl.BlockSpec((D_MODEL, LANES), lambda m: (0, 0)),
        ],
        out_specs=[
            pl.BlockSpec((tm, D_PROJ), lambda m: (m, 0)),
            pl.BlockSpec((tm, LANES), lambda m: (m, 0)),
        ],
        out_shape=[jax.ShapeDtypeStruct((t, D_PROJ), BF16), jax.ShapeDtypeStruct((t, LANES), F32)],
        compiler_params=_cparams(("parallel",)),
        name="inproj",
    )(x, nw, w, wdt)


def _retention_kernel(q_ref, k_ref, v_ref, g_ref, cos_ref, sin_ref, intra_ref, qdec_ref, kdec_ref,
                      cd_ref, avg_ref, nw_ref, o_ref, state_ref):
    c = pl.program_id(1)

    @pl.when(c == 0)
    def _():
        state_ref[...] = jnp.zeros_like(state_ref)

    cos = cos_ref[...]
    sin = sin_ref[...]

    def rot(t):
        t1, t2 = t[:, :LANES], t[:, LANES:]
        return jnp.concatenate([t1 * cos - t2 * sin, t1 * sin + t2 * cos], axis=-1)

    lane = lax.broadcasted_iota(jnp.int32, (1, RET_DIM), 1)
    head_qk = (lane % LANES) // (HEAD_DIM // 2)
    cd = cd_ref[...]
    avg = avg_ref[...]

    for b in range(q_ref.shape[0]):
        rq = rot(q_ref[b].astype(F32))
        rk = rot(k_ref[b].astype(F32))
        vb = v_ref[b]

        q_stack = jnp.concatenate([jnp.where(head_qk == h, rq, 0.0) for h in range(RET_HEADS)], axis=0)
        scores = _dot_nt(q_stack.astype(BF16), rk.astype(BF16)) * intra_ref[...]
        y = _pair_heads_matmul(scores.astype(BF16), vb, RET_HEADS)

        state = state_ref[b]
        y = y + _dot((rq * qdec_ref[...]).astype(BF16), state.astype(BF16))
        kd = (rk * kdec_ref[...]).astype(BF16)
        kv = lax.dot_general(kd, vb, (((0,), (0,)), ((), ())), preferred_element_type=F32)
        state_ref[b] = state * cd + jnp.where(cd != 0.0, kv, 0.0)

        mu = _dot_x_const(y, avg)
        d = y - mu
        var = _dot_x_const(d * d, avg)
        yn = d * lax.rsqrt(var + EPS)
        o_ref[b] = (yn * nw_ref[...] * _silu(g_ref[b].astype(F32))).astype(BF16)


def _retention_tables(seq):
    f = np.float32
    half = HEAD_DIM // 2
    inv = (1.0 / (f(ROPE_BASE) ** (np.arange(half, dtype=f) / f(half)))).astype(f)
    ang = np.arange(seq, dtype=f)[:, None] * inv[None, :]
    cos = np.tile(np.cos(ang).astype(f), (1, RET_HEADS))
    sin = np.tile(np.sin(ang).astype(f), (1, RET_HEADS))
    log_gamma = np.log(f(1.0) - f(2.0) ** (f(-5.0) - np.arange(RET_HEADS, dtype=f))).astype(f)
    idx = np.arange(CHUNK, dtype=f)
    diff = idx[:, None] - idx[None, :]
    scale = f(HEAD_DIM ** -0.5)
    intra = np.where(diff >= 0, np.exp(log_gamma[:, None, None] * np.maximum(diff, 0.0)), 0.0).astype(f)
    intra = (intra * scale).reshape(RET_HEADS * CHUNK, CHUNK)
    head_qk = (np.arange(RET_DIM) % LANES) // half
    head_v = np.arange(RET_DIM) // HEAD_DIM
    qdec = np.exp(log_gamma[head_qk][None, :] * (idx[:, None] + 1.0)).astype(f)
    kdec = (np.exp(log_gamma[head_qk][None, :] * (CHUNK - 1.0 - idx[:, None])) * scale).astype(f)
    chunk_decay = np.exp(log_gamma * f(CHUNK)).astype(f)
    same = head_qk[:, None] == head_v[None, :]
    cd = np.where(same, chunk_decay[head_qk][:, None], 0.0).astype(f)
    avg = np.where(head_v[:, None] == head_v[None, :], 1.0 / HEAD_DIM, 0.0).astype(f)
    return (jnp.asarray(cos), jnp.asarray(sin), jnp.asarray(intra), jnp.asarray(qdec), jnp.asarray(kdec),
            jnp.asarray(cd), jnp.asarray(avg, dtype=BF16))


MIXER_GROUP = 4


def _retention(proj, tables, nw, bsz, seq):
    nc = seq // CHUNK
    cos, sin, intra, qdec, kdec, cd, avg = tables
    w = RET_DIM
    grp = MIXER_GROUP if bsz % MIXER_GROUP == 0 else 1
    proj3 = proj.reshape(bsz, seq, D_PROJ)

    def col(j):
        return pl.BlockSpec((grp, CHUNK, w), lambda b, c: (b, c, j))

    def const(shape):
        return pl.BlockSpec(shape, lambda b, c: (0, 0))

    out = pl.pallas_call(
        _retention_kernel,
        grid=(bsz // grp, nc),
        in_specs=[
            col(OFF_RQ // w), col(OFF_RK // w), col(OFF_RV // w), col(OFF_RG // w),
            pl.BlockSpec((CHUNK, LANES), lambda b, c: (c, 0)),
            pl.BlockSpec((CHUNK, LANES), lambda b, c: (c, 0)),
            const((RET_HEADS * CHUNK, CHUNK)), const((CHUNK, w)), const((CHUNK, w)),
            const((w, w)), const((w, w)), const((1, w)),
        ],
        out_specs=pl.BlockSpec((grp, CHUNK, w), lambda b, c: (b, c, 0)),
        out_shape=jax.ShapeDtypeStruct((bsz, seq, w), BF16),
        scratch_shapes=[pltpu.VMEM((grp, w, w), F32)],
        compiler_params=_cparams(("parallel", "arbitrary")),
        name="retention",
    )(proj3, proj3, proj3, proj3, cos, sin, intra, qdec, kdec, cd, avg, nw)
    return out.reshape(bsz * seq, w)


def _ssd_kernel(z_ref, xbc_ref, dt_ref, cw_ref, cb_ref, dtb_ref, alog_ref, dskip_ref, nw_ref,
                tri_ref, exp_ref, o_ref, prev_ref, state_ref):
    c = pl.program_id(1)

    @pl.when(c == 0)
    def _():
        prev_ref[...] = jnp.zeros_like(prev_ref)
        state_ref[...] = jnp.zeros_like(state_ref)

    row = lax.broadcasted_iota(jnp.int32, (SUBLANES, 1), 0)
    tri = tri_ref[...]
    lane_bc = lax.broadcasted_iota(jnp.int32, (1, SSD_BC), 1)
    lower = (lax.broadcasted_iota(jnp.int32, (CHUNK, CHUNK), 0)
             >= lax.broadcasted_iota(jnp.int32, (CHUNK, CHUNK), 1))
    heads_per_group = SSD_HEADS // SSD_GROUPS
    lane_x = lax.broadcasted_iota(jnp.int32, (1, SSD_INNER), 1)
    row_g = lax.broadcasted_iota(jnp.int32, (SSD_BC, 1), 0) // SSD_STATE
    same_group = row_g == lane_x // (SSD_INNER // SSD_GROUPS)
    gw = SSD_INNER // SSD_GROUPS

    for b in range(z_ref.shape[0]):
        cur = xbc_ref[b].astype(F32)
        tail = prev_ref[b]
        acc = cur * cw_ref[SSD_CONV - 1:SSD_CONV, :] + cb_ref[...]
        for j in range(1, SSD_CONV):
            rolled = pltpu.roll(cur, j, 0)
            top = jnp.where(row >= j, rolled[:SUBLANES], pltpu.roll(tail, j, 0))
            shifted = jnp.concatenate([top, rolled[SUBLANES:]], axis=0)
            acc = acc + shifted * cw_ref[SSD_CONV - 1 - j:SSD_CONV - j, :]
        prev_ref[b] = cur[CHUNK - SUBLANES:]
        xa = _silu(acc)
        x = xa[:, :SSD_INNER]
        bm = xa[:, SSD_INNER:SSD_INNER + SSD_BC]
        cm = xa[:, SSD_INNER + SSD_BC:]

        dt = _softplus(dt_ref[b] + dtb_ref[...])
        a = dt * (-jnp.exp(alog_ref[...]))
        a_hi, a_mid, a_lo = _split3(a)
        a_cs = _dot(tri, a_hi) + _dot(tri, a_mid) + _dot(tri, a_lo)
        a_cs_t = a_cs.T
        stack = jnp.concatenate([dt, jnp.exp(a_cs), jnp.exp(a_cs[CHUNK - 1:CHUNK, :] - a_cs)], axis=0)
        stack_x = _dot_x_const(stack, exp_ref[...])
        dt_x = stack_x[:CHUNK]
        ea_x = stack_x[CHUNK:2 * CHUNK]
        dec_x = stack_x[2 * CHUNK:]

        xdt = x * dt_x
        xdt_b = xdt.astype(BF16)
        bm_b = bm.astype(BF16)
        cm_b = cm.astype(BF16)
        cb = [_dot_nt(jnp.where(lane_bc // SSD_STATE == g, cm, 0.0).astype(BF16), bm_b)
              for g in range(SSD_GROUPS)]

        m_list = []
        for h in range(SSD_HEADS):
            seg = a_cs[:, h:h + 1] - a_cs_t[h:h + 1, :]
            l_mat = jnp.exp(jnp.where(lower, seg, -jnp.inf))
            m_list.append((cb[h // heads_per_group] * l_mat).astype(BF16))
        y = _pair_heads_matmul(jnp.concatenate(m_list, axis=0), xdt_b, SSD_HEADS)

        state = state_ref[b]
        y = y + _dot(cm_b, state.astype(BF16)) * ea_x
        contrib = _dot(bm.T.astype(BF16), (xdt * dec_x).astype(BF16))
        state_ref[b] = state * ea_x[CHUNK - 1:CHUNK, :] + jnp.where(same_group, contrib, 0.0)

        y = y + dskip_ref[...] * x
        gy = y * _silu(z_ref[b].astype(F32))
        outs = []
        for g in range(SSD_GROUPS):
            part = gy[:, g * gw:(g + 1) * gw]
            outs.append(part * lax.rsqrt(jnp.mean(part * part, axis=-1, keepdims=True) + EPS))
        o_ref[b] = (jnp.concatenate(outs, axis=-1) * nw_ref[...]).astype(BF16)


def _ssd_tables():
    idx = np.arange(CHUNK)
    tri = (idx[:, None] >= idx[None, :]).astype(np.float32)
    expand = np.zeros((LANES, SSD_INNER), np.float32)
    for h in range(SSD_HEADS):
        expand[h, h * HEAD_DIM:(h + 1) * HEAD_DIM] = 1.0
    return jnp.asarray(tri, dtype=BF16), jnp.asarray(expand, dtype=BF16)


def _ssd(proj, dt_raw, conv_w, conv_b, dt_bias, a_log, d_skip, nw, tables, bsz, seq):
    nc = seq // CHUNK
    tri, expand = tables

    grp = MIXER_GROUP if bsz % MIXER_GROUP == 0 else 1
    proj3 = proj.reshape(bsz, seq, D_PROJ)
    dt3 = dt_raw.reshape(bsz, seq, LANES)

    def const(shape):
        return pl.BlockSpec(shape, lambda b, c: (0, 0))

    out = pl.pallas_call(
        _ssd_kernel,
        grid=(bsz // grp, nc),
        in_specs=[
            pl.BlockSpec((grp, CHUNK, SSD_INNER), lambda b, c: (b, c, OFF_SZ // SSD_INNER)),
            pl.BlockSpec((grp, CHUNK, SSD_CONV_DIM), lambda b, c: (b, c, OFF_SXBC // SSD_CONV_DIM)),
            pl.BlockSpec((grp, CHUNK, LANES), lambda b, c: (b, c, 0)),
            const((SSD_CONV, SSD_CONV_DIM)), const((1, SSD_CONV_DIM)), const((1, LANES)), const((1, LANES)),
            const((1, SSD_INNER)), const((1, SSD_INNER)), const((CHUNK, CHUNK)), const((LANES, SSD_INNER)),
        ],
        out_specs=pl.BlockSpec((grp, CHUNK, SSD_INNER), lambda b, c: (b, c, 0)),
        out_shape=jax.ShapeDtypeStruct((bsz, seq, SSD_INNER), BF16),
        scratch_shapes=[pltpu.VMEM((grp, SUBLANES, SSD_CONV_DIM), F32), pltpu.VMEM((grp, SSD_BC, SSD_INNER), F32)],
        compiler_params=_cparams(("parallel", "arbitrary")),
        name="ssd",
    )(proj3, proj3, dt3, conv_w, conv_b, dt_bias, a_log, d_skip, nw, tri, expand)
    return out.reshape(bsz * seq, SSD_INNER)


SB_BLOCK = 256
SB_GROUP = 4


def _stickbreak_kernel(q_ref, k_ref, v_ref, u_ref, o_ref, acc_ref):
    i = pl.program_id(1)
    blk = SB_BLOCK
    grp = q_ref.shape[0]
    lane = lax.broadcasted_iota(jnp.int32, (1, SB_DIM), 1)
    head = lane // HEAD_DIM
    rows = SB_HEADS * blk
    q_stack = []
    for b in range(grp):
        q = q_ref[b] * jnp.asarray(HEAD_DIM ** -0.5, BF16)
        q_stack.append(jnp.concatenate(
            [jnp.where(head == h, q, jnp.zeros_like(q)) for h in range(SB_HEADS)], axis=0))
    u = u_ref[...]
    strict_lower = (lax.broadcasted_iota(jnp.int32, (rows, blk), 1)
                    < lax.broadcasted_iota(jnp.int32, (rows, blk), 0) % blk)

    def block(j, carries, diagonal):
        start = pl.multiple_of(j * blk, blk)
        new = []
        for b in range(grp):
            kb = k_ref[b, pl.ds(start, blk), :]
            vb = v_ref[b, pl.ds(start, blk), :]
            z = _dot_nt(q_stack[b], kb)
            sp = _softplus(z)
            if diagonal:
                sp = jnp.where(strict_lower, sp, 0.0)
            cs = _dot(sp.astype(BF16), u)
            w = jnp.exp(z - cs - carries[b])
            if diagonal:
                w = jnp.where(strict_lower, w, 0.0)
            wb = w.astype(BF16)
            for pair in range(SB_HEADS // 2):
                pv = _dot(wb[pair * 2 * blk:(pair + 1) * 2 * blk], vb[:, pair * LANES:(pair + 1) * LANES])
                if diagonal:
                    acc_ref[b, pair] = pv
                else:
                    acc_ref[b, pair] += pv
            new.append(carries[b] + jnp.sum(sp, axis=-1, keepdims=True))
        return tuple(new)

    zero = jnp.zeros((rows, 1), F32)
    carries = block(i, (zero,) * grp, True)
    lax.fori_loop(0, i, lambda jj, c: block(i - 1 - jj, c, False), carries)
    first_of_pair = lax.broadcasted_iota(jnp.int32, (1, LANES), 1) < HEAD_DIM
    for b in range(grp):
        out = [jnp.where(first_of_pair, acc_ref[b, p, :blk], acc_ref[b, p, blk:]) for p in range(SB_HEADS // 2)]
        o_ref[b] = jnp.concatenate(out, axis=-1).astype(BF16)


def _stickbreak_tables():
    idx = np.arange(SB_BLOCK)
    rev = (idx[:, None] >= idx[None, :]).astype(np.float32)
    return jnp.asarray(rev, dtype=BF16)


def _stickbreak(proj, u, bsz, seq):
    nq = seq // SB_BLOCK
    w = SB_DIM
    grp = SB_GROUP if bsz % SB_GROUP == 0 else 1
    proj3 = proj.reshape(bsz, seq, D_PROJ)
    out = pl.pallas_call(
        _stickbreak_kernel,
        grid=(bsz // grp, nq),
        in_specs=[
            pl.BlockSpec((grp, SB_BLOCK, w), lambda b, i: (b, i, OFF_BQ // w)),
            pl.BlockSpec((grp, seq, w), lambda b, i: (b, 0, OFF_BK // w)),
            pl.BlockSpec((grp, seq, w), lambda b, i: (b, 0, OFF_BV // w)),
            pl.BlockSpec((SB_BLOCK, SB_BLOCK), lambda b, i: (0, 0)),
        ],
        out_specs=pl.BlockSpec((grp, SB_BLOCK, w), lambda b, i: (b, i, 0)),
        out_shape=jax.ShapeDtypeStruct((bsz, seq, w), BF16),
        scratch_shapes=[pltpu.VMEM((grp, SB_HEADS // 2, 2 * SB_BLOCK, LANES), F32)],
        compiler_params=_cparams(("parallel", "arbitrary")),
        name="stickbreak",
    )(proj3, proj3, proj3, u)
    return out.reshape(bsz * seq, w)


def _outproj_kernel(x_ref, ro_ref, so_ref, bo_ref, w_ref, o_ref, wb_ref):
    @pl.when(pl.program_id(0) == 0)
    def _():
        wb_ref[...] = w_ref[0].astype(BF16)

    acc = _dot(ro_ref[...], wb_ref[:RET_DIM, :])
    acc = acc + _dot(so_ref[...], wb_ref[RET_DIM:RET_DIM + SSD_INNER, :])
    acc = acc + _dot(bo_ref[...], wb_ref[RET_DIM + SSD_INNER:, :])
    o_ref[...] = x_ref[...] + acc


def _outproj(x, ro, so, bo, w, layer, tm=512):
    t = x.shape[0]
    return pl.pallas_call(
        _outproj_kernel,
        grid=(t // tm,),
        in_specs=[
            pl.BlockSpec((tm, D_MODEL), lambda m: (m, 0)),
            pl.BlockSpec((tm, RET_DIM), lambda m: (m, 0)),
            pl.BlockSpec((tm, SSD_INNER), lambda m: (m, 0)),
            pl.BlockSpec((tm, SB_DIM), lambda m: (m, 0)),
            pl.BlockSpec((1, D_MIX, D_MODEL), lambda m: (layer, 0, 0)),
        ],
        out_specs=pl.BlockSpec((tm, D_MODEL), lambda m: (m, 0)),
        out_shape=jax.ShapeDtypeStruct((t, D_MODEL), F32),
        scratch_shapes=[pltpu.VMEM((D_MIX, D_MODEL), BF16)],
        compiler_params=_cparams(("arbitrary",)),
        name="outproj",
    )(x, ro, so, bo, w)


def _ffn_kernel(x_ref, nw_ref, wg_ref, wu_ref, wd_ref, o_ref, h_ref, acc_ref):
    f = pl.program_id(1)

    @pl.when(f == 0)
    def _():
        h_ref[...] = _rms_h(x_ref[...], nw_ref[...]).astype(BF16)
        acc_ref[...] = jnp.zeros_like(acc_ref)

    h = h_ref[...]
    act = _silu(_dot(h, wg_ref[0].astype(BF16))) * _dot(h, wu_ref[0].astype(BF16))
    acc_ref[...] += _dot(act.astype(BF16), wd_ref[0].astype(BF16))

    @pl.when(f == pl.num_programs(1) - 1)
    def _():
        o_ref[...] = x_ref[...] + acc_ref[...]


def _ffn(x, nw, wg, wu, wd, layer, tm=1024, tf=512):
    t = x.shape[0]
    return pl.pallas_call(
        _ffn_kernel,
        grid=(t // tm, D_FF // tf),
        in_specs=[
            pl.BlockSpec((tm, D_MODEL), lambda m, f: (m, 0)),
            pl.BlockSpec((1, D_MODEL), lambda m, f: (0, 0)),
            pl.BlockSpec((1, D_MODEL, tf), lambda m, f: (layer, 0, f)),
            pl.BlockSpec((1, D_MODEL, tf), lambda m, f: (layer, 0, f)),
            pl.BlockSpec((1, tf, D_MODEL), lambda m, f: (layer, f, 0)),
        ],
        out_specs=pl.BlockSpec((tm, D_MODEL), lambda m, f: (m, 0)),
        out_shape=jax.ShapeDtypeStruct((t, D_MODEL), F32),
        scratch_shapes=[pltpu.VMEM((tm, D_MODEL), BF16), pltpu.VMEM((tm, D_MODEL), F32)],
        compiler_params=_cparams(("parallel", "arbitrary")),
        name="ffn",
    )(x, nw, wg, wu, wd)


def _router_kernel(x_ref, nw_ref, rw_ref, idx_ref, gate_ref):
    h = _rms_h(x_ref[...], nw_ref[...])
    h_hi, h_lo = _split2(h)
    w_hi, w_lo = _split2(rw_ref[...])
    logits = _dot(h_hi, w_hi) + _dot(h_hi, w_lo) + _dot(h_lo, w_hi)
    lane = lax.broadcasted_iota(jnp.int32, (1, LANES), 1)
    lg = jnp.where(lane < N_EXPERTS, logits, -jnp.inf)
    m1 = jnp.max(lg, axis=-1, keepdims=True)
    i1 = jnp.min(jnp.where(lg == m1, lane, LANES), axis=-1, keepdims=True)
    lg2 = jnp.where(lane == i1, -jnp.inf, lg)
    m2 = jnp.max(lg2, axis=-1, keepdims=True)
    i2 = jnp.min(jnp.where(lg2 == m2, lane, LANES), axis=-1, keepdims=True)
    e = jnp.exp(m2 - m1)
    g1 = 1.0 / (1.0 + e)
    g2 = e / (1.0 + e)
    idx_ref[...] = jnp.where(lane == 0, i1, jnp.where(lane == 1, i2, 0))
    gate_ref[...] = jnp.where(lane == 0, g1, jnp.where(lane == 1, g2, 0.0))


def _router(x, nw, rw, tm=512):
    t = x.shape[0]
    return pl.pallas_call(
        _router_kernel,
        grid=(t // tm,),
        in_specs=[
            pl.BlockSpec((tm, D_MODEL), lambda m: (m, 0)),
            pl.BlockSpec((1, D_MODEL), lambda m: (0, 0)),
            pl.BlockSpec((D_MODEL, LANES), lambda m: (0, 0)),
        ],
        out_specs=[
            pl.BlockSpec((tm, LANES), lambda m: (m, 0)),
            pl.BlockSpec((tm, LANES), lambda m: (m, 0)),
        ],
        out_shape=[jax.ShapeDtypeStruct((t, LANES), jnp.int32), jax.ShapeDtypeStruct((t, LANES), F32)],
        compiler_params=_cparams(("parallel",)),
        name="router",
    )(x, nw, rw)


MOE_TM = 1024
MOE_TF = 512
MOE_FF_BLOCKS = D_FF // MOE_TF
MOE_SHARE = -(-MOE_TM // MOE_FF_BLOCKS)
MOE_ROWS_BUF = -(-MOE_SHARE * MOE_FF_BLOCKS // SUBLANES) * SUBLANES


def _moe_plan(e2):
    t = e2.shape[0]
    p = 2 * t
    ef = e2.reshape(p)
    onehot = (ef[:, None] == jnp.arange(N_EXPERTS, dtype=jnp.int32)[None, :]).astype(jnp.int32)
    csum = jnp.cumsum(onehot, axis=0)
    counts = csum[-1]
    rank = jnp.sum((csum - onehot) * onehot, axis=1)
    padded = ((counts + MOE_TM - 1) // MOE_TM) * MOE_TM
    ends = jnp.cumsum(padded)
    starts = ends - padded
    dst = jnp.sum(onehot * starts[None, :], axis=1) + rank
    n_tiles = p // MOE_TM + N_EXPERTS
    tile_start = jnp.arange(n_tiles, dtype=jnp.int32) * MOE_TM
    tile_expert = jnp.sum((tile_start[:, None] >= ends[None, :]).astype(jnp.int32), axis=1)
    tile_expert = jnp.minimum(tile_expert, N_EXPERTS - 1)
    n_used = (ends[-1] // MOE_TM).reshape(1).astype(jnp.int32)
    last_row = (starts + counts)[tile_expert]
    tile_valid = jnp.clip(last_row - tile_start, 0, MOE_TM).astype(jnp.int32)
    src = jnp.zeros((n_tiles * MOE_TM,), jnp.int32).at[dst].set(
        jnp.arange(p, dtype=jnp.int32) // 2, unique_indices=True, mode="promise_in_bounds")
    return dst.reshape(t, 2), src, tile_expert.astype(jnp.int32), tile_valid, n_used


MOE_SLOTS = 3


def _expert_kernel(te_ref, nu_ref, tv_ref, src_ref, src1_ref, src2_ref, x_hbm, nw_ref, wg_ref, wu_ref, wd_ref,
                   o_ref, rows_ref, sem, h_ref, acc_ref):
    i = pl.program_id(0)
    f = pl.program_id(1)
    nf = pl.num_programs(1)
    n_used = nu_ref[0]
    slot = i % MOE_SLOTS
    share = MOE_SHARE
    issued = MOE_SHARE * MOE_FF_BLOCKS

    def gather_copy(idx_ref, r, s):
        row = idx_ref[0, 0, jnp.minimum(r, MOE_TM - 1)]
        return pltpu.make_async_copy(x_hbm.at[pl.ds(row, 1)], rows_ref.at[s, pl.ds(r, 1)], sem.at[s])

    def wait_slot(s):
        pltpu.make_async_copy(x_hbm.at[pl.ds(0, MOE_TM)], rows_ref.at[s, pl.ds(0, MOE_TM)], sem.at[s]).wait()
        for r in range(MOE_TM, issued):
            pltpu.make_async_copy(x_hbm.at[pl.ds(0, 1)], rows_ref.at[s, pl.ds(r, 1)], sem.at[s]).wait()

    @pl.when(f == 0)
    def _():
        acc_ref[...] = jnp.zeros_like(acc_ref)

    @pl.when((f == 0) & (i == 0))
    def _():
        def body(r, carry):
            gather_copy(src_ref, r, 0).start()
            gather_copy(src1_ref, r, 1).start()
            return carry
        lax.fori_loop(0, issued, body, 0, unroll=8)

    @pl.when((f == 0) & (i < n_used))
    def _():
        wait_slot(slot)
        h_ref[...] = _rms_h(rows_ref[slot, :MOE_TM, :], nw_ref[...]).astype(BF16)

    def step(rows):
        for k in range(share):
            gather_copy(src2_ref, f * share + k, (i + 2) % MOE_SLOTS).start()
        h = h_ref[:rows, :]
        act = _silu(_dot(h, wg_ref[0, 0].astype(BF16))) * _dot(h, wu_ref[0, 0].astype(BF16))
        acc_ref[:rows, :] += _dot(act.astype(BF16), wd_ref[0, 0].astype(BF16))

    half = MOE_TM // 2
    valid = tv_ref[i]

    @pl.when((i < n_used) & (valid > half))
    def _():
        step(MOE_TM)

    @pl.when((i < n_used) & (valid <= half))
    def _():
        step(half)

    @pl.when((i == n_used - 1) & (f == nf - 1))
    def _():
        wait_slot((i + 1) % MOE_SLOTS)
        wait_slot((i + 2) % MOE_SLOTS)

    @pl.when(f == nf - 1)
    def _():
        o_ref[...] = acc_ref[...].astype(BF16)


def _experts(x, nw, src, tile_expert, tile_valid, n_used, wg, wu, wd, layer):
    n_tiles = tile_expert.shape[0]
    tf = MOE_TF
    nf = MOE_FF_BLOCKS
    src3 = src.reshape(n_tiles, 1, MOE_TM)

    def fblock(i, f, nu):
        return jnp.where(i < nu[0], f, nf - 1)

    def smem_row(ahead):
        return pl.BlockSpec((1, 1, MOE_TM), lambda i, f, te, nu, tv: (jnp.minimum(i + ahead, n_tiles - 1), 0, 0),
                            memory_space=pltpu.SMEM)

    grid_spec = pltpu.PrefetchScalarGridSpec(
        num_scalar_prefetch=3,
        grid=(n_tiles, nf),
        in_specs=[
            smem_row(0), smem_row(1), smem_row(2),
            pl.BlockSpec(memory_space=pl.ANY),
            pl.BlockSpec((1, D_MODEL), lambda i, f, te, nu, tv: (0, 0)),
            pl.BlockSpec((1, 1, D_MODEL, tf), lambda i, f, te, nu, tv: (layer, te[i], 0, fblock(i, f, nu))),
            pl.BlockSpec((1, 1, D_MODEL, tf), lambda i, f, te, nu, tv: (layer, te[i], 0, fblock(i, f, nu))),
            pl.BlockSpec((1, 1, tf, D_MODEL), lambda i, f, te, nu, tv: (layer, te[i], fblock(i, f, nu), 0)),
        ],
        out_specs=pl.BlockSpec((MOE_TM, D_MODEL), lambda i, f, te, nu, tv: (i, 0)),
        scratch_shapes=[
            pltpu.VMEM((MOE_SLOTS, MOE_ROWS_BUF, D_MODEL), F32),
            pltpu.SemaphoreType.DMA((MOE_SLOTS,)),
            pltpu.VMEM((MOE_TM, D_MODEL), BF16),
            pltpu.VMEM((MOE_TM, D_MODEL), F32),
        ],
    )
    return pl.pallas_call(
        _expert_kernel,
        grid_spec=grid_spec,
        out_shape=jax.ShapeDtypeStruct((n_tiles * MOE_TM, D_MODEL), BF16),
        compiler_params=_cparams(("arbitrary", "arbitrary")),
        name="experts",
    )(tile_expert, n_used, tile_valid, src3, src3, src3, x, nw, wg, wu, wd)


def _combine_kernel(x_ref, y1_ref, y2_ref, gate_ref, nw_ref, o_ref, *, final_norm):
    g = gate_ref[...]
    out = x_ref[...] + g[:, 0:1] * y1_ref[...].astype(F32) + g[:, 1:2] * y2_ref[...].astype(F32)
    if final_norm:
        out = _rms_h(out, nw_ref[...])
    o_ref[...] = out


def _combine(x, y1, y2, gates, nw, final_norm, tm=1024):
    t = x.shape[0]
    return pl.pallas_call(
        functools.partial(_combine_kernel, final_norm=final_norm),
        grid=(t // tm,),
        in_specs=[
            pl.BlockSpec((tm, D_MODEL), lambda m: (m, 0)),
            pl.BlockSpec((tm, D_MODEL), lambda m: (m, 0)),
            pl.BlockSpec((tm, D_MODEL), lambda m: (m, 0)),
            pl.BlockSpec((tm, LANES), lambda m: (m, 0)),
            pl.BlockSpec((1, D_MODEL), lambda m: (0, 0)),
        ],
        out_specs=pl.BlockSpec((tm, D_MODEL), lambda m: (m, 0)),
        out_shape=jax.ShapeDtypeStruct((t, D_MODEL), F32),
        compiler_params=_cparams(("parallel",)),
        name="combine",
    )(x, y1, y2, gates, nw)


def _moe_layer(xf, norm_w, router_w, wg, wu, wd, layer, final_w=None):
    rw = jnp.pad(router_w.astype(F32), ((0, 0), (0, LANES - N_EXPERTS)))
    idx, gates = _router(xf, _row(norm_w), rw)
    dst, src, tile_expert, tile_valid, n_used = _moe_plan(idx[:, :2])
    ys = _experts(xf, _row(norm_w), src, tile_expert, tile_valid, n_used, wg, wu, wd, layer)
    y1 = ys.at[dst[:, 0]].get(mode="promise_in_bounds")
    y2 = ys.at[dst[:, 1]].get(mode="promise_in_bounds")
    nw = _row(final_w) if final_w is not None else jnp.ones((1, D_MODEL), F32)
    return _combine(xf, y1, y2, gates, nw, final_w is not None)


def _final_norm_kernel(x_ref, nw_ref, o_ref):
    o_ref[...] = _rms_h(x_ref[...], nw_ref[...])


def _final_norm(x, nw, tm=1024):
    t = x.shape[0]
    return pl.pallas_call(
        _final_norm_kernel,
        grid=(t // tm,),
        in_specs=[pl.BlockSpec((tm, D_MODEL), lambda m: (m, 0)), pl.BlockSpec((1, D_MODEL), lambda m: (0, 0))],
        out_specs=pl.BlockSpec((tm, D_MODEL), lambda m: (m, 0)),
        out_shape=jax.ShapeDtypeStruct((t, D_MODEL), F32),
        compiler_params=_cparams(("parallel",)),
        name="final_norm",
    )(x, nw)


def _halves_first(w):
    rows = w.shape[0]
    w = w.reshape(rows, RET_HEADS, 2, HEAD_DIM // 2)
    return jnp.swapaxes(w, 1, 2).reshape(rows, RET_DIM)


def _layout_w_in(w_in):
    dt0 = 4 * RET_DIM + SSD_INNER + SSD_CONV_DIM
    main = jnp.concatenate([
        _halves_first(w_in[:, :RET_DIM]), _halves_first(w_in[:, RET_DIM:2 * RET_DIM]),
        w_in[:, 2 * RET_DIM:dt0], w_in[:, dt0 + SSD_HEADS:],
    ], axis=1).astype(BF16)
    wdt = jnp.pad(w_in[:, dt0:dt0 + SSD_HEADS], ((0, 0), (0, LANES - SSD_HEADS))).astype(BF16)
    return main, wdt


def _row(v, width=None):
    v = v.reshape(1, -1).astype(F32)
    if width is not None and v.shape[1] < width:
        v = jnp.pad(v, ((0, 0), (0, width - v.shape[1])))
    return v


def kernel(x, norm1_w, w_in, ret_norm_w, ssd_conv_w, ssd_conv_b, ssd_dt_bias, ssd_a_log, ssd_d, ssd_norm_w, w_out, norm2_w, ffn_w_gate, ffn_w_up, ffn_w_down, moe_router, moe_w_gate, moe_w_up, moe_w_down, final_norm_w):
    bsz, seq, _ = x.shape
    depth = w_in.shape[0]
    ret_tables = _retention_tables(seq)
    ssd_tables = _ssd_tables()
    sb_u = _stickbreak_tables()

    xf = x.reshape(bsz * seq, D_MODEL)
    for layer in range(depth):
        w_main, w_dt = _layout_w_in(w_in[layer])
        proj, dt_raw = _inproj(xf, _row(norm1_w[layer]), w_main, w_dt)
        ro = _retention(proj, ret_tables, _row(ret_norm_w[layer]), bsz, seq)
        so = _ssd(proj, dt_raw, ssd_conv_w[layer].astype(F32), _row(ssd_conv_b[layer]),
                  _row(ssd_dt_bias[layer], LANES), _row(ssd_a_log[layer], LANES),
                  _row(jnp.repeat(ssd_d[layer], HEAD_DIM)), _row(ssd_norm_w[layer]), ssd_tables, bsz, seq)
        bo = _stickbreak(proj, sb_u, bsz, seq)
        xf = _outproj(xf, ro, so, bo, w_out, layer)
        j = layer // 2
        if layer % 2 == 0:
            xf = _ffn(xf, _row(norm2_w[layer]), ffn_w_gate, ffn_w_up, ffn_w_down, j)
        else:
            last = layer == depth - 1
            xf = _moe_layer(xf, norm2_w[layer], moe_router[j], moe_w_gate, moe_w_up, moe_w_down, j,
                            final_norm_w if last else None)
    if depth % 2 == 1:
        xf = _final_norm(xf, _row(final_norm_w))
    return xf.reshape(bsz, seq, D_MODEL)
```

```python
import functools
import math

import numpy as np
import jax
import jax.numpy as jnp
from jax import lax
from jax.experimental import pallas as pl
from jax.experimental.pallas import tpu as pltpu

F32 = jnp.float32
BF16 = jnp.bfloat16

D_MODEL = 1024
HEAD_DIM = 64
RET_HEADS = 4
RET_DIM = RET_HEADS * HEAD_DIM
SSD_HEADS = 8
SSD_INNER = SSD_HEADS * HEAD_DIM
SSD_GROUPS = 2
SSD_STATE = 64
SSD_CONV = 4
SSD_BC = SSD_GROUPS * SSD_STATE
SSD_CONV_DIM = SSD_INNER + 2 * SSD_BC
SB_HEADS = 4
SB_DIM = SB_HEADS * HEAD_DIM
D_MIX = RET_DIM + SSD_INNER + SB_DIM
CHUNK = 128
D_FF = 3584
N_EXPERTS = 8
ROPE_BASE = 10000.0
EPS = 1e-6

LANES = 128
SUBLANES = 8
D_PROJ = 3072
OFF_RQ, OFF_RK, OFF_RV, OFF_RG = 0, 256, 512, 768
OFF_SZ, OFF_SXBC = 1024, 1536
OFF_BQ, OFF_BK, OFF_BV = 2304, 2560, 2816

VMEM_LIMIT = 56 * 1024 * 1024


def _cparams(sem):
    return pltpu.CompilerParams(dimension_semantics=sem, vmem_limit_bytes=VMEM_LIMIT)


def _dot(a, b):
    return jnp.dot(a, b, preferred_element_type=F32)


def _dot_nt(a, b):
    return lax.dot_general(a, b, (((1,), (1,)), ((), ())), preferred_element_type=F32)


def _split2(x):
    hi = x.astype(BF16)
    lo = (x - hi.astype(F32)).astype(BF16)
    return hi, lo


def _split3(x):
    hi = x.astype(BF16)
    r = x - hi.astype(F32)
    mid = r.astype(BF16)
    lo = (r - mid.astype(F32)).astype(BF16)
    return hi, mid, lo


def _dot_x_const(x, c):
    hi, lo = _split2(x)
    return _dot(hi, c) + _dot(lo, c)


def _silu(x):
    return x * jax.nn.sigmoid(x)


def _softplus(x):
    e = jnp.exp2(jnp.abs(x) * (-math.log2(math.e)))
    return jnp.maximum(x, 0.0) + jnp.log(1.0 + e)


def _pair_heads_matmul(m_stack, x, n_heads):
    rows = m_stack.shape[0] // n_heads
    first_of_pair = lax.broadcasted_iota(jnp.int32, (1, LANES), 1) < HEAD_DIM
    parts = []
    for p in range(n_heads // 2):
        both = _dot(m_stack[2 * p * rows:(2 * p + 2) * rows], x[:, p * LANES:(p + 1) * LANES])
        parts.append(jnp.where(first_of_pair, both[:rows], both[rows:]))
    return jnp.concatenate(parts, axis=-1)


def _rms_h(x, w):
    ms = jnp.mean(x * x, axis=-1, keepdims=True)
    return x * lax.rsqrt(ms + EPS) * w


def _inproj_kernel(x_ref, nw_ref, w_ref, wdt_ref, o_ref, dt_ref):
    h = _rms_h(x_ref[...], nw_ref[...]).astype(BF16)
    step = 512
    for c in range(D_PROJ // step):
        o_ref[:, c * step:(c + 1) * step] = _dot(h, w_ref[:, c * step:(c + 1) * step]).astype(BF16)
    dt_ref[...] = _dot(h, wdt_ref[...])


def _inproj(x, nw, w, wdt, tm=1024):
    t = x.shape[0]
    return pl.pallas_call(
        _inproj_kernel,
        grid=(t // tm,),
        in_specs=[
            pl.BlockSpec((tm, D_MODEL), lambda m: (m, 0)),
            pl.BlockSpec((1, D_MODEL), lambda m: (0, 0)),
            pl.BlockSpec((D_MODEL, D_PROJ), lambda m: (0, 0)),
            pl.BlockSpec((D_MODEL, LANES), lambda m: (0, 0)),
        ],
        out_specs=[
            pl.BlockSpec((tm, D_PROJ), lambda m: (m, 0)),
            pl.BlockSpec((tm, LANES), lambda m: (m, 0)),
        ],
        out_shape=[jax.ShapeDtypeStruct((t, D_PROJ), BF16), jax.ShapeDtypeStruct((t, LANES), F32)],
        compiler_params=_cparams(("parallel",)),
        name="inproj",
    )(x, nw, w, wdt)


def _retention_kernel(q_ref, k_ref, v_ref, g_ref, cos_ref, sin_ref, intra_ref, qdec_ref, kdec_ref,
                      cd_ref, avg_ref, nw_ref, o_ref, state_ref):
    c = pl.program_id(1)

    @pl.when(c == 0)
    def _():
        state_ref[...] = jnp.zeros_like(state_ref)

    cos = cos_ref[...]
    sin = sin_ref[...]

    def rot(t):
        t1, t2 = t[:, :LANES], t[:, LANES:]
        return jnp.concatenate([t1 * cos - t2 * sin, t1 * sin + t2 * cos], axis=-1)

    lane = lax.broadcasted_iota(jnp.int32, (1, RET_DIM), 1)
    head_qk = (lane % LANES) // (HEAD_DIM // 2)
    cd = cd_ref[...]
    avg = avg_ref[...]

    for b in range(q_ref.shape[0]):
        rq = rot(q_ref[b].astype(F32))
        rk = rot(k_ref[b].astype(F32))
        vb = v_ref[b]

        q_stack = jnp.concatenate([jnp.where(head_qk == h, rq, 0.0) for h in range(RET_HEADS)], axis=0)
        scores = _dot_nt(q_stack.astype(BF16), rk.astype(BF16)) * intra_ref[...]
        y = _pair_heads_matmul(scores.astype(BF16), vb, RET_HEADS)

        state = state_ref[b]
        y = y + _dot((rq * qdec_ref[...]).astype(BF16), state.astype(BF16))
        kd = (rk * kdec_ref[...]).astype(BF16)
        kv = lax.dot_general(kd, vb, (((0,), (0,)), ((), ())), preferred_element_type=F32)
        state_ref[b] = state * cd + jnp.where(cd != 0.0, kv, 0.0)

        mu = _dot_x_const(y, avg)
        d = y - mu
        var = _dot_x_const(d * d, avg)
        yn = d * lax.rsqrt(var + EPS)
        o_ref[b] = (yn * nw_ref[...] * _silu(g_ref[b].astype(F32))).astype(BF16)


def _retention_tables(seq):
    f = np.float32
    half = HEAD_DIM // 2
    inv = (1.0 / (f(ROPE_BASE) ** (np.arange(half, dtype=f) / f(half)))).astype(f)
    ang = np.arange(seq, dtype=f)[:, None] * inv[None, :]
    cos = np.tile(np.cos(ang).astype(f), (1, RET_HEADS))
    sin = np.tile(np.sin(ang).astype(f), (1, RET_HEADS))
    log_gamma = np.log(f(1.0) - f(2.0) ** (f(-5.0) - np.arange(RET_HEADS, dtype=f))).astype(f)
    idx = np.arange(CHUNK, dtype=f)
    diff = idx[:, None] - idx[None, :]
    scale = f(HEAD_DIM ** -0.5)
    intra = np.where(diff >= 0, np.exp(log_gamma[:, None, None] * np.maximum(diff, 0.0)), 0.0).astype(f)
    intra = (intra * scale).reshape(RET_HEADS * CHUNK, CHUNK)
    head_qk = (np.arange(RET_DIM) % LANES) // half
    head_v = np.arange(RET_DIM) // HEAD_DIM
    qdec = np.exp(log_gamma[head_qk][None, :] * (idx[:, None] + 1.0)).astype(f)
    kdec = (np.exp(log_gamma[head_qk][None, :] * (CHUNK - 1.0 - idx[:, None])) * scale).astype(f)
    chunk_decay = np.exp(log_gamma * f(CHUNK)).astype(f)
    same = head_qk[:, None] == head_v[None, :]
    cd = np.where(same, chunk_decay[head_qk][:, None], 0.0).astype(f)
    avg = np.where(head_v[:, None] == head_v[None, :], 1.0 / HEAD_DIM, 0.0).astype(f)
    return (jnp.asarray(cos), jnp.asarray(sin), jnp.asarray(intra), jnp.asarray(qdec), jnp.asarray(kdec),
            jnp.asarray(cd), jnp.asarray(avg, dtype=BF16))


MIXER_GROUP = 4


def _retention(proj, tables, nw, bsz, seq):
    nc = seq // CHUNK
    cos, sin, intra, qdec, kdec, cd, avg = tables
    w = RET_DIM
    grp = MIXER_GROUP if bsz % MIXER_GROUP == 0 else 1
    proj3 = proj.reshape(bsz, seq, D_PROJ)

    def col(j):
        return pl.BlockSpec((grp, CHUNK, w), lambda b, c: (b, c, j))

    def const(shape):
        return pl.BlockSpec(shape, lambda b, c: (0, 0))

    out = pl.pallas_call(
        _retention_kernel,
        grid=(bsz // grp, nc),
        in_specs=[
            col(OFF_RQ // w), col(OFF_RK // w), col(OFF_RV // w), col(OFF_RG // w),
            pl.BlockSpec((CHUNK, LANES), lambda b, c: (c, 0)),
            pl.BlockSpec((CHUNK, LANES), lambda b, c: (c, 0)),
            const((RET_HEADS * CHUNK, CHUNK)), const((CHUNK, w)), const((CHUNK, w)),
            const((w, w)), const((w, w)), const((1, w)),
        ],
        out_specs=pl.BlockSpec((grp, CHUNK, w), lambda b, c: (b, c, 0)),
        out_shape=jax.ShapeDtypeStruct((bsz, seq, w), BF16),
        scratch_shapes=[pltpu.VMEM((grp, w, w), F32)],
        compiler_params=_cparams(("parallel", "arbitrary")),
        name="retention",
    )(proj3, proj3, proj3, proj3, cos, sin, intra, qdec, kdec, cd, avg, nw)
    return out.reshape(bsz * seq, w)


def _ssd_kernel(z_ref, xbc_ref, dt_ref, cw_ref, cb_ref, dtb_ref, alog_ref, dskip_ref, nw_ref,
                tri_ref, exp_ref, o_ref, prev_ref, state_ref):
    c = pl.program_id(1)

    @pl.when(c == 0)
    def _():
        prev_ref[...] = jnp.zeros_like(prev_ref)
        state_ref[...] = jnp.zeros_like(state_ref)

    row = lax.broadcasted_iota(jnp.int32, (SUBLANES, 1), 0)
    tri = tri_ref[...]
    lane_bc = lax.broadcasted_iota(jnp.int32, (1, SSD_BC), 1)
    lower = (lax.broadcasted_iota(jnp.int32, (CHUNK, CHUNK), 0)
             >= lax.broadcasted_iota(jnp.int32, (CHUNK, CHUNK), 1))
    heads_per_group = SSD_HEADS // SSD_GROUPS
    lane_x = lax.broadcasted_iota(jnp.int32, (1, SSD_INNER), 1)
    row_g = lax.broadcasted_iota(jnp.int32, (SSD_BC, 1), 0) // SSD_STATE
    same_group = row_g == lane_x // (SSD_INNER // SSD_GROUPS)
    gw = SSD_INNER // SSD_GROUPS

    for b in range(z_ref.shape[0]):
        cur = xbc_ref[b].astype(F32)
        tail = prev_ref[b]
        acc = cur * cw_ref[SSD_CONV - 1:SSD_CONV, :] + cb_ref[...]
        for j in range(1, SSD_CONV):
            rolled = pltpu.roll(cur, j, 0)
            top = jnp.where(row >= j, rolled[:SUBLANES], pltpu.roll(tail, j, 0))
            shifted = jnp.concatenate([top, rolled[SUBLANES:]], axis=0)
            acc = acc + shifted * cw_ref[SSD_CONV - 1 - j:SSD_CONV - j, :]
        prev_ref[b] = cur[CHUNK - SUBLANES:]
        xa = _silu(acc)
        x = xa[:, :SSD_INNER]
        bm = xa[:, SSD_INNER:SSD_INNER + SSD_BC]
        cm = xa[:, SSD_INNER + SSD_BC:]

        dt = _softplus(dt_ref[b] + dtb_ref[...])
        a = dt * (-jnp.exp(alog_ref[...]))
        a_hi, a_mid, a_lo = _split3(a)
        a_cs = _dot(tri, a_hi) + _dot(tri, a_mid) + _dot(tri, a_lo)
        a_cs_t = a_cs.T
        stack = jnp.concatenate([dt, jnp.exp(a_cs), jnp.exp(a_cs[CHUNK - 1:CHUNK, :] - a_cs)], axis=0)
        stack_x = _dot_x_const(stack, exp_ref[...])
        dt_x = stack_x[:CHUNK]
        ea_x = stack_x[CHUNK:2 * CHUNK]
        dec_x = stack_x[2 * CHUNK:]

        xdt = x * dt_x
        xdt_b = xdt.astype(BF16)
        bm_b = bm.astype(BF16)
        cm_b = cm.astype(BF16)
        cb = [_dot_nt(jnp.where(lane_bc // SSD_STATE == g, cm, 0.0).astype(BF16), bm_b)
              for g in range(SSD_GROUPS)]

        m_list = []
        for h in range(SSD_HEADS):
            seg = a_cs[:, h:h + 1] - a_cs_t[h:h + 1, :]
            l_mat = jnp.exp(jnp.where(lower, seg, -jnp.inf))
            m_list.append((cb[h // heads_per_group] * l_mat).astype(BF16))
        y = _pair_heads_matmul(jnp.concatenate(m_list, axis=0), xdt_b, SSD_HEADS)

        state = state_ref[b]
        y = y + _dot(cm_b, state.astype(BF16)) * ea_x
        contrib = _dot(bm.T.astype(BF16), (xdt * dec_x).astype(BF16))
        state_ref[b] = state * ea_x[CHUNK - 1:CHUNK, :] + jnp.where(same_group, contrib, 0.0)

        y = y + dskip_ref[...] * x
        gy = y * _silu(z_ref[b].astype(F32))
        outs = []
        for g in range(SSD_GROUPS):
            part = gy[:, g * gw:(g + 1) * gw]
            outs.append(part * lax.rsqrt(jnp.mean(part * part, axis=-1, keepdims=True) + EPS))
        o_ref[b] = (jnp.concatenate(outs, axis=-1) * nw_ref[...]).astype(BF16)


def _ssd_tables():
    idx = np.arange(CHUNK)
    tri = (idx[:, None] >= idx[None, :]).astype(np.float32)
    expand = np.zeros((LANES, SSD_INNER), np.float32)
    for h in range(SSD_HEADS):
        expand[h, h * HEAD_DIM:(h + 1) * HEAD_DIM] = 1.0
    return jnp.asarray(tri, dtype=BF16), jnp.asarray(expand, dtype=BF16)


def _ssd(proj, dt_raw, conv_w, conv_b, dt_bias, a_log, d_skip, nw, tables, bsz, seq):
    nc = seq // CHUNK
    tri, expand = tables

    grp = MIXER_GROUP if bsz % MIXER_GROUP == 0 else 1
    proj3 = proj.reshape(bsz, seq, D_PROJ)
    dt3 = dt_raw.reshape(bsz, seq, LANES)

    def const(shape):
        return pl.BlockSpec(shape, lambda b, c: (0, 0))

    out = pl.pallas_call(
        _ssd_kernel,
        grid=(bsz // grp, nc),
        in_specs=[
            pl.BlockSpec((grp, CHUNK, SSD_INNER), lambda b, c: (b, c, OFF_SZ // SSD_INNER)),
            pl.BlockSpec((grp, CHUNK, SSD_CONV_DIM), lambda b, c: (b, c, OFF_SXBC // SSD_CONV_DIM)),
            pl.BlockSpec((grp, CHUNK, LANES), lambda b, c: (b, c, 0)),
            const((SSD_CONV, SSD_CONV_DIM)), const((1, SSD_CONV_DIM)), const((1, LANES)), const((1, LANES)),
            const((1, SSD_INNER)), const((1, SSD_INNER)), const((CHUNK, CHUNK)), const((LANES, SSD_INNER)),
        ],
        out_specs=pl.BlockSpec((grp, CHUNK, SSD_INNER), lambda b, c: (b, c, 0)),
        out_shape=jax.ShapeDtypeStruct((bsz, seq, SSD_INNER), BF16),
        scratch_shapes=[pltpu.VMEM((grp, SUBLANES, SSD_CONV_DIM), F32), pltpu.VMEM((grp, SSD_BC, SSD_INNER), F32)],
        compiler_params=_cparams(("parallel", "arbitrary")),
        name="ssd",
    )(proj3, proj3, dt3, conv_w, conv_b, dt_bias, a_log, d_skip, nw, tri, expand)
    return out.reshape(bsz * seq, SSD_INNER)


SB_BLOCK = 256
SB_GROUP = 4


def _stickbreak_kernel(q_ref, k_ref, v_ref, u_ref, o_ref, acc_ref):
    i = pl.program_id(1)
    blk = SB_BLOCK
    grp = q_ref.shape[0]
    lane = lax.broadcasted_iota(jnp.int32, (1, SB_DIM), 1)
    head = lane // HEAD_DIM
    rows = SB_HEADS * blk
    q_stack = []
    for b in range(grp):
        q = q_ref[b] * jnp.asarray(HEAD_DIM ** -0.5, BF16)
        q_stack.append(jnp.concatenate(
            [jnp.where(head == h, q, jnp.zeros_like(q)) for h in range(SB_HEADS)], axis=0))
    u = u_ref[...]
    strict_lower = (lax.broadcasted_iota(jnp.int32, (rows, blk), 1)
                    < lax.broadcasted_iota(jnp.int32, (rows, blk), 0) % blk)

    def block(j, carries, diagonal):
        start = pl.multiple_of(j * blk, blk)
        new = []
        for b in range(grp):
            kb = k_ref[b, pl.ds(start, blk), :]
            vb = v_ref[b, pl.ds(start, blk), :]
            z = _dot_nt(q_stack[b], kb)
            sp = _softplus(z)
            if diagonal:
                sp = jnp.where(strict_lower, sp, 0.0)
            cs = _dot(sp.astype(BF16), u)
            w = jnp.exp(z - cs - carries[b])
            if diagonal:
                w = jnp.where(strict_lower, w, 0.0)
            wb = w.astype(BF16)
            for pair in range(SB_HEADS // 2):
                pv = _dot(wb[pair * 2 * blk:(pair + 1) * 2 * blk], vb[:, pair * LANES:(pair + 1) * LANES])
                if diagonal:
                    acc_ref[b, pair] = pv
                else:
                    acc_ref[b, pair] += pv
            new.append(carries[b] + jnp.sum(sp, axis=-1, keepdims=True))
        return tuple(new)

    zero = jnp.zeros((rows, 1), F32)
    carries = block(i, (zero,) * grp, True)
    lax.fori_loop(0, i, lambda jj, c: block(i - 1 - jj, c, False), carries)
    first_of_pair = lax.broadcasted_iota(jnp.int32, (1, LANES), 1) < HEAD_DIM
    for b in range(grp):
        out = [jnp.where(first_of_pair, acc_ref[b, p, :blk], acc_ref[b, p, blk:]) for p in range(SB_HEADS // 2)]
        o_ref[b] = jnp.concatenate(out, axis=-1).astype(BF16)


def _stickbreak_tables():
    idx = np.arange(SB_BLOCK)
    rev = (idx[:, None] >= idx[None, :]).astype(np.float32)
    return jnp.asarray(rev, dtype=BF16)


def _stickbreak(proj, u, bsz, seq):
    nq = seq // SB_BLOCK
    w = SB_DIM
    grp = SB_GROUP if bsz % SB_GROUP == 0 else 1
    proj3 = proj.reshape(bsz, seq, D_PROJ)
    out = pl.pallas_call(
        _stickbreak_kernel,
        grid=(bsz // grp, nq),
        in_specs=[
            pl.BlockSpec((grp, SB_BLOCK, w), lambda b, i: (b, i, OFF_BQ // w)),
            pl.BlockSpec((grp, seq, w), lambda b, i: (b, 0, OFF_BK // w)),
            pl.BlockSpec((grp, seq, w), lambda b, i: (b, 0, OFF_BV // w)),
            pl.BlockSpec((SB_BLOCK, SB_BLOCK), lambda b, i: (0, 0)),
        ],
        out_specs=pl.BlockSpec((grp, SB_BLOCK, w), lambda b, i: (b, i, 0)),
        out_shape=jax.ShapeDtypeStruct((bsz, seq, w), BF16),
        scratch_shapes=[pltpu.VMEM((grp, SB_HEADS // 2, 2 * SB_BLOCK, LANES), F32)],
        compiler_params=_cparams(("parallel", "arbitrary")),
        name="stickbreak",
    )(proj3, proj3, proj3, u)
    return out.reshape(bsz * seq, w)


def _outproj_kernel(x_ref, ro_ref, so_ref, bo_ref, w_ref, o_ref, wb_ref):
    @pl.when(pl.program_id(0) == 0)
    def _():
        wb_ref[...] = w_ref[0].astype(BF16)

    acc = _dot(ro_ref[...], wb_ref[:RET_DIM, :])
    acc = acc + _dot(so_ref[...], wb_ref[RET_DIM:RET_DIM + SSD_INNER, :])
    acc = acc + _dot(bo_ref[...], wb_ref[RET_DIM + SSD_INNER:, :])
    o_ref[...] = x_ref[...] + acc


def _outproj(x, ro, so, bo, w, layer, tm=512):
    t = x.shape[0]
    return pl.pallas_call(
        _outproj_kernel,
        grid=(t // tm,),
        in_specs=[
            pl.BlockSpec((tm, D_MODEL), lambda m: (m, 0)),
            pl.BlockSpec((tm, RET_DIM), lambda m: (m, 0)),
            pl.BlockSpec((tm, SSD_INNER), lambda m: (m, 0)),
            pl.BlockSpec((tm, SB_DIM), lambda m: (m, 0)),
            pl.BlockSpec((1, D_MIX, D_MODEL), lambda m: (layer, 0, 0)),
        ],
        out_specs=pl.BlockSpec((tm, D_MODEL), lambda m: (m, 0)),
        out_shape=jax.ShapeDtypeStruct((t, D_MODEL), F32),
        scratch_shapes=[pltpu.VMEM((D_MIX, D_MODEL), BF16)],
        compiler_params=_cparams(("arbitrary",)),
        name="outproj",
    )(x, ro, so, bo, w)


def _ffn_kernel(x_ref, nw_ref, wg_ref, wu_ref, wd_ref, o_ref, h_ref, acc_ref):
    f = pl.program_id(1)

    @pl.when(f == 0)
    def _():
        h_ref[...] = _rms_h(x_ref[...], nw_ref[...]).astype(BF16)
        acc_ref[...] = jnp.zeros_like(acc_ref)

    h = h_ref[...]
    act = _silu(_dot(h, wg_ref[0].astype(BF16))) * _dot(h, wu_ref[0].astype(BF16))
    acc_ref[...] += _dot(act.astype(BF16), wd_ref[0].astype(BF16))

    @pl.when(f == pl.num_programs(1) - 1)
    def _():
        o_ref[...] = x_ref[...] + acc_ref[...]


def _ffn(x, nw, wg, wu, wd, layer, tm=1024, tf=512):
    t = x.shape[0]
    return pl.pallas_call(
        _ffn_kernel,
        grid=(t // tm, D_FF // tf),
        in_specs=[
            pl.BlockSpec((tm, D_MODEL), lambda m, f: (m, 0)),
            pl.BlockSpec((1, D_MODEL), lambda m, f: (0, 0)),
            pl.BlockSpec((1, D_MODEL, tf), lambda m, f: (layer, 0, f)),
            pl.BlockSpec((1, D_MODEL, tf), lambda m, f: (layer, 0, f)),
            pl.BlockSpec((1, tf, D_MODEL), lambda m, f: (layer, f, 0)),
        ],
        out_specs=pl.BlockSpec((tm, D_MODEL), lambda m, f: (m, 0)),
        out_shape=jax.ShapeDtypeStruct((t, D_MODEL), F32),
        scratch_shapes=[pltpu.VMEM((tm, D_MODEL), BF16), pltpu.VMEM((tm, D_MODEL), F32)],
        compiler_params=_cparams(("parallel", "arbitrary")),
        name="ffn",
    )(x, nw, wg, wu, wd)


def _router_kernel(x_ref, nw_ref, rw_ref, idx_ref, gate_ref):
    h = _rms_h(x_ref[...], nw_ref[...])
    h_hi, h_lo = _split2(h)
    w_hi, w_lo = _split2(rw_ref[...])
    logits = _dot(h_hi, w_hi) + _dot(h_hi, w_lo) + _dot(h_lo, w_hi)
    lane = lax.broadcasted_iota(jnp.int32, (1, LANES), 1)
    lg = jnp.where(lane < N_EXPERTS, logits, -jnp.inf)
    m1 = jnp.max(lg, axis=-1, keepdims=True)
    i1 = jnp.min(jnp.where(lg == m1, lane, LANES), axis=-1, keepdims=True)
    lg2 = jnp.where(lane == i1, -jnp.inf, lg)
    m2 = jnp.max(lg2, axis=-1, keepdims=True)
    i2 = jnp.min(jnp.where(lg2 == m2, lane, LANES), axis=-1, keepdims=True)
    e = jnp.exp(m2 - m1)
    g1 = 1.0 / (1.0 + e)
    g2 = e / (1.0 + e)
    idx_ref[...] = jnp.where(lane == 0, i1, jnp.where(lane == 1, i2, 0))
    gate_ref[...] = jnp.where(lane == 0, g1, jnp.where(lane == 1, g2, 0.0))


def _router(x, nw, rw, tm=512):
    t = x.shape[0]
    return pl.pallas_call(
        _router_kernel,
        grid=(t // tm,),
        in_specs=[
            pl.BlockSpec((tm, D_MODEL), lambda m: (m, 0)),
            pl.BlockSpec((1, D_MODEL), lambda m: (0, 0)),
            pl.BlockSpec((D_MODEL, LANES), lambda m: (0, 0)),
        ],
        out_specs=[
            pl.BlockSpec((tm, LANES), lambda m: (m, 0)),
            pl.BlockSpec((tm, LANES), lambda m: (m, 0)),
        ],
        out_shape=[jax.ShapeDtypeStruct((t, LANES), jnp.int32), jax.ShapeDtypeStruct((t, LANES), F32)],
        compiler_params=_cparams(("parallel",)),
        name="router",
    )(x, nw, rw)


MOE_TM = 1024
MOE_TF = 512
MOE_FF_BLOCKS = D_FF // MOE_TF
MOE_SHARE = -(-MOE_TM // MOE_FF_BLOCKS)
MOE_ROWS_BUF = -(-MOE_SHARE * MOE_FF_BLOCKS // SUBLANES) * SUBLANES


def _moe_plan(e2):
    t = e2.shape[0]
    p = 2 * t
    ef = e2.reshape(p)
    onehot = (ef[:, None] == jnp.arange(N_EXPERTS, dtype=jnp.int32)[None, :]).astype(jnp.int32)
    csum = jnp.cumsum(onehot, axis=0)
    counts = csum[-1]
    rank = jnp.sum((csum - onehot) * onehot, axis=1)
    padded = ((counts + MOE_TM - 1) // MOE_TM) * MOE_TM
    ends = jnp.cumsum(padded)
    starts = ends - padded
    dst = jnp.sum(onehot * starts[None, :], axis=1) + rank
    n_tiles = p // MOE_TM + N_EXPERTS
    tile_start = jnp.arange(n_tiles, dtype=jnp.int32) * MOE_TM
    tile_expert = jnp.sum((tile_start[:, None] >= ends[None, :]).astype(jnp.int32), axis=1)
    tile_expert = jnp.minimum(tile_expert, N_EXPERTS - 1)
    n_used = (ends[-1] // MOE_TM).reshape(1).astype(jnp.int32)
    last_row = (starts + counts)[tile_expert]
    tile_valid = jnp.clip(last_row - tile_start, 0, MOE_TM).astype(jnp.int32)
    tile_first = jnp.concatenate([jnp.ones((1,), jnp.int32),
                                  (tile_expert[1:] != tile_expert[:-1]).astype(jnp.int32)])
    src = jnp.zeros((n_tiles * MOE_TM,), jnp.int32).at[dst].set(
        jnp.arange(p, dtype=jnp.int32) // 2, unique_indices=True, mode="promise_in_bounds")
    return dst.reshape(t, 2), src, tile_expert.astype(jnp.int32), tile_valid, tile_first, n_used


MOE_SLOTS = 3


def _expert_kernel(te_ref, nu_ref, tv_ref, first_ref, src_ref, src1_ref, src2_ref, x_hbm, nw_ref,
                   wg_ref, wu_ref, wd_ref, o_ref, rows_ref, sem, h_ref, acc_ref, wg_cache, wu_cache):
    i = pl.program_id(0)
    f = pl.program_id(1)
    nf = pl.num_programs(1)
    n_used = nu_ref[0]
    slot = i % MOE_SLOTS
    share = MOE_SHARE
    issued = MOE_SHARE * MOE_FF_BLOCKS

    def gather_copy(idx_ref, r, s):
        row = idx_ref[0, 0, jnp.minimum(r, MOE_TM - 1)]
        return pltpu.make_async_copy(x_hbm.at[pl.ds(row, 1)], rows_ref.at[s, pl.ds(r, 1)], sem.at[s])

    def wait_slot(s):
        pltpu.make_async_copy(x_hbm.at[pl.ds(0, MOE_TM)], rows_ref.at[s, pl.ds(0, MOE_TM)], sem.at[s]).wait()
        for r in range(MOE_TM, issued):
            pltpu.make_async_copy(x_hbm.at[pl.ds(0, 1)], rows_ref.at[s, pl.ds(r, 1)], sem.at[s]).wait()

    @pl.when(f == 0)
    def _():
        acc_ref[...] = jnp.zeros_like(acc_ref)

    @pl.when((f == 0) & (i == 0))
    def _():
        def body(r, carry):
            gather_copy(src_ref, r, 0).start()
            gather_copy(src1_ref, r, 1).start()
            return carry
        lax.fori_loop(0, issued, body, 0, unroll=8)

    @pl.when((f == 0) & (i < n_used))
    def _():
        wait_slot(slot)
        h_ref[...] = _rms_h(rows_ref[slot, :MOE_TM, :], nw_ref[...]).astype(BF16)

    def step(rows):
        for k in range(share):
            gather_copy(src2_ref, f * share + k, (i + 2) % MOE_SLOTS).start()
        h = h_ref[:rows, :]
        act = _silu(_dot(h, wg_cache[f])) * _dot(h, wu_cache[f])
        acc_ref[:rows, :] += _dot(act.astype(BF16), wd_ref[0, 0].astype(BF16))

    @pl.when((i < n_used) & (first_ref[i] == 1))
    def _():
        wg_cache[f] = wg_ref[0, 0].astype(BF16)
        wu_cache[f] = wu_ref[0, 0].astype(BF16)

    half = MOE_TM // 2
    valid = tv_ref[i]

    @pl.when((i < n_used) & (valid > half))
    def _():
        step(MOE_TM)

    @pl.when((i < n_used) & (valid <= half))
    def _():
        step(half)

    @pl.when((i == n_used - 1) & (f == nf - 1))
    def _():
        wait_slot((i + 1) % MOE_SLOTS)
        wait_slot((i + 2) % MOE_SLOTS)

    @pl.when(f == nf - 1)
    def _():
        o_ref[...] = acc_ref[...].astype(BF16)


def _experts(x, nw, src, tile_expert, tile_valid, tile_first, n_used, wg, wu, wd, layer):
    n_tiles = tile_expert.shape[0]
    tf = MOE_TF
    nf = MOE_FF_BLOCKS
    src3 = src.reshape(n_tiles, 1, MOE_TM)

    def fblock(i, f, nu):
        return jnp.where(i < nu[0], f, nf - 1)

    def fblock_first(i, f, nu, first):
        return jnp.where((i < nu[0]) & (first[i] == 1), f, nf - 1)

    def smem_row(ahead):
        return pl.BlockSpec((1, 1, MOE_TM),
                            lambda i, f, te, nu, tv, first: (jnp.minimum(i + ahead, n_tiles - 1), 0, 0),
                            memory_space=pltpu.SMEM)

    def gate_up_spec():
        return pl.BlockSpec((1, 1, D_MODEL, tf),
                            lambda i, f, te, nu, tv, first: (layer, te[i], 0, fblock_first(i, f, nu, first)))

    grid_spec = pltpu.PrefetchScalarGridSpec(
        num_scalar_prefetch=4,
        grid=(n_tiles, nf),
        in_specs=[
            smem_row(0), smem_row(1), smem_row(2),
            pl.BlockSpec(memory_space=pl.ANY),
            pl.BlockSpec((1, D_MODEL), lambda i, f, te, nu, tv, first: (0, 0)),
            gate_up_spec(), gate_up_spec(),
            pl.BlockSpec((1, 1, tf, D_MODEL), lambda i, f, te, nu, tv, first: (layer, te[i], fblock(i, f, nu), 0)),
        ],
        out_specs=pl.BlockSpec((MOE_TM, D_MODEL), lambda i, f, te, nu, tv, first: (i, 0)),
        scratch_shapes=[
            pltpu.VMEM((MOE_SLOTS, MOE_ROWS_BUF, D_MODEL), F32),
            pltpu.SemaphoreType.DMA((MOE_SLOTS,)),
            pltpu.VMEM((MOE_TM, D_MODEL), BF16),
            pltpu.VMEM((MOE_TM, D_MODEL), F32),
            pltpu.VMEM((nf, D_MODEL, tf), BF16),
            pltpu.VMEM((nf, D_MODEL, tf), BF16),
        ],
    )
    return pl.pallas_call(
        _expert_kernel,
        grid_spec=grid_spec,
        out_shape=jax.ShapeDtypeStruct((n_tiles * MOE_TM, D_MODEL), BF16),
        compiler_params=_cparams(("arbitrary", "arbitrary")),
        name="experts",
    )(tile_expert, n_used, tile_valid, tile_first, src3, src3, src3, x, nw, wg, wu, wd)


def _combine_kernel(x_ref, y1_ref, y2_ref, gate_ref, nw_ref, o_ref, *, final_norm):
    g = gate_ref[...]
    out = x_ref[...] + g[:, 0:1] * y1_ref[...].astype(F32) + g[:, 1:2] * y2_ref[...].astype(F32)
    if final_norm:
        out = _rms_h(out, nw_ref[...])
    o_ref[...] = out


def _combine(x, y1, y2, gates, nw, final_norm, tm=1024):
    t = x.shape[0]
    return pl.pallas_call(
        functools.partial(_combine_kernel, final_norm=final_norm),
        grid=(t // tm,),
        in_specs=[
            pl.BlockSpec((tm, D_MODEL), lambda m: (m, 0)),
            pl.BlockSpec((tm, D_MODEL), lambda m: (m, 0)),
            pl.BlockSpec((tm, D_MODEL), lambda m: (m, 0)),
            pl.BlockSpec((tm, LANES), lambda m: (m, 0)),
            pl.BlockSpec((1, D_MODEL), lambda m: (0, 0)),
        ],
        out_specs=pl.BlockSpec((tm, D_MODEL), lambda m: (m, 0)),
        out_shape=jax.ShapeDtypeStruct((t, D_MODEL), F32),
        compiler_params=_cparams(("parallel",)),
        name="combine",
    )(x, y1, y2, gates, nw)


def _moe_layer(xf, norm_w, router_w, wg, wu, wd, layer, final_w=None):
    rw = jnp.pad(router_w.astype(F32), ((0, 0), (0, LANES - N_EXPERTS)))
    idx, gates = _router(xf, _row(norm_w), rw)
    dst, src, tile_expert, tile_valid, tile_first, n_used = _moe_plan(idx[:, :2])
    ys = _experts(xf, _row(norm_w), src, tile_expert, tile_valid, tile_first, n_used, wg, wu, wd, layer)
    y1 = ys.at[dst[:, 0]].get(mode="promise_in_bounds")
    y2 = ys.at[dst[:, 1]].get(mode="promise_in_bounds")
    nw = _row(final_w) if final_w is not None else jnp.ones((1, D_MODEL), F32)
    return _combine(xf, y1, y2, gates, nw, final_w is not None)


def _final_norm_kernel(x_ref, nw_ref, o_ref):
    o_ref[...] = _rms_h(x_ref[...], nw_ref[...])


def _final_norm(x, nw, tm=1024):
    t = x.shape[0]
    return pl.pallas_call(
        _final_norm_kernel,
        grid=(t // tm,),
        in_specs=[pl.BlockSpec((tm, D_MODEL), lambda m: (m, 0)), pl.BlockSpec((1, D_MODEL), lambda m: (0, 0))],
        out_specs=pl.BlockSpec((tm, D_MODEL), lambda m: (m, 0)),
        out_shape=jax.ShapeDtypeStruct((t, D_MODEL), F32),
        compiler_params=_cparams(("parallel",)),
        name="final_norm",
    )(x, nw)


def _halves_first(w):
    rows = w.shape[0]
    w = w.reshape(rows, RET_HEADS, 2, HEAD_DIM // 2)
    return jnp.swapaxes(w, 1, 2).reshape(rows, RET_DIM)


def _layout_w_in(w_in):
    dt0 = 4 * RET_DIM + SSD_INNER + SSD_CONV_DIM
    main = jnp.concatenate([
        _halves_first(w_in[:, :RET_DIM]), _halves_first(w_in[:, RET_DIM:2 * RET_DIM]),
        w_in[:, 2 * RET_DIM:dt0], w_in[:, dt0 + SSD_HEADS:],
    ], axis=1).astype(BF16)
    wdt = jnp.pad(w_in[:, dt0:dt0 + SSD_HEADS], ((0, 0), (0, LANES - SSD_HEADS))).astype(BF16)
    return main, wdt


def _row(v, width=None):
    v = v.reshape(1, -1).astype(F32)
    if width is not None and v.shape[1] < width:
        v = jnp.pad(v, ((0, 0), (0, width - v.shape[1])))
    return v


def kernel(x, norm1_w, w_in, ret_norm_w, ssd_conv_w, ssd_conv_b, ssd_dt_bias, ssd_a_log, ssd_d, ssd_norm_w, w_out, norm2_w, ffn_w_gate, ffn_w_up, ffn_w_down, moe_router, moe_w_gate, moe_w_up, moe_w_down, final_norm_w):
    bsz, seq, _ = x.shape
    depth = w_in.shape[0]
    ret_tables = _retention_tables(seq)
    ssd_tables = _ssd_tables()
    sb_u = _stickbreak_tables()

    xf = x.reshape(bsz * seq, D_MODEL)
    for layer in range(depth):
        w_main, w_dt = _layout_w_in(w_in[layer])
        proj, dt_raw = _inproj(xf, _row(norm1_w[layer]), w_main, w_dt)
        ro = _retention(proj, ret_tables, _row(ret_norm_w[layer]), bsz, seq)
        so = _ssd(proj, dt_raw, ssd_conv_w[layer].astype(F32), _row(ssd_conv_b[layer]),
                  _row(ssd_dt_bias[layer], LANES), _row(ssd_a_log[layer], LANES),
                  _row(jnp.repeat(ssd_d[layer], HEAD_DIM)), _row(ssd_norm_w[layer]), ssd_tables, bsz, seq)
        bo = _stickbreak(proj, sb_u, bsz, seq)
        xf = _outproj(xf, ro, so, bo, w_out, layer)
        j = layer // 2
        if layer % 2 == 0:
            xf = _ffn(xf, _row(norm2_w[layer]), ffn_w_gate, ffn_w_up, ffn_w_down, j)
        else:
            last = layer == depth - 1
            xf = _moe_layer(xf, norm2_w[layer], moe_router[j], moe_w_gate, moe_w_up, moe_w_down, j,
                            final_norm_w if last else None)
    if depth % 2 == 1:
        xf = _final_norm(xf, _row(final_norm_w))
    return xf.reshape(bsz, seq, D_MODEL)
```

```python
import functools
import math

import numpy as np
import jax
import jax.numpy as jnp
from jax import lax
from jax.experimental import pallas as pl
from jax.experimental.pallas import tpu as pltpu

F32 = jnp.float32
BF16 = jnp.bfloat16

D_MODEL = 1024
HEAD_DIM = 64
RET_HEADS = 4
RET_DIM = RET_HEADS * HEAD_DIM
SSD_HEADS = 8
SSD_INNER = SSD_HEADS * HEAD_DIM
SSD_GROUPS = 2
SSD_STATE = 64
SSD_CONV = 4
SSD_BC = SSD_GROUPS * SSD_STATE
SSD_CONV_DIM = SSD_INNER + 2 * SSD_BC
SB_HEADS = 4
SB_DIM = SB_HEADS * HEAD_DIM
D_MIX = RET_DIM + SSD_INNER + SB_DIM
CHUNK = 128
D_FF = 3584
N_EXPERTS = 8
ROPE_BASE = 10000.0
EPS = 1e-6

LANES = 128
SUBLANES = 8
D_PROJ = 3072
OFF_RQ, OFF_RK, OFF_RV, OFF_RG = 0, 256, 512, 768
OFF_SZ, OFF_SXBC = 1024, 1536
OFF_BQ, OFF_BK, OFF_BV = 2304, 2560, 2816

VMEM_LIMIT = 56 * 1024 * 1024


def _cparams(sem):
    return pltpu.CompilerParams(dimension_semantics=sem, vmem_limit_bytes=VMEM_LIMIT)


def _dot(a, b):
    return jnp.dot(a, b, preferred_element_type=F32)


def _dot_nt(a, b):
    return lax.dot_general(a, b, (((1,), (1,)), ((), ())), preferred_element_type=F32)


def _split2(x):
    hi = x.astype(BF16)
    lo = (x - hi.astype(F32)).astype(BF16)
    return hi, lo


def _split3(x):
    hi = x.astype(BF16)
    r = x - hi.astype(F32)
    mid = r.astype(BF16)
    lo = (r - mid.astype(F32)).astype(BF16)
    return hi, mid, lo


def _dot_x_const(x, c):
    hi, lo = _split2(x)
    return _dot(hi, c) + _dot(lo, c)


def _silu(x):
    return x * jax.nn.sigmoid(x)


def _softplus(x):
    e = jnp.exp2(jnp.abs(x) * (-math.log2(math.e)))
    return jnp.maximum(x, 0.0) + jnp.log(1.0 + e)


def _pair_heads_matmul(m_stack, x, n_heads):
    rows = m_stack.shape[0] // n_heads
    first_of_pair = lax.broadcasted_iota(jnp.int32, (1, LANES), 1) < HEAD_DIM
    parts = []
    for p in range(n_heads // 2):
        both = _dot(m_stack[2 * p * rows:(2 * p + 2) * rows], x[:, p * LANES:(p + 1) * LANES])
        parts.append(jnp.where(first_of_pair, both[:rows], both[rows:]))
    return jnp.concatenate(parts, axis=-1)


def _rms_h(x, w):
    ms = jnp.mean(x * x, axis=-1, keepdims=True)
    return x * lax.rsqrt(ms + EPS) * w


def _inproj_kernel(x_ref, nw_ref, w_ref, wdt_ref, o_ref, dt_ref):
    h = _rms_h(x_ref[...], nw_ref[...]).astype(BF16)
    step = 512
    for c in range(D_PROJ // step):
        o_ref[:, c * step:(c + 1) * step] = _dot(h, w_ref[:, c * step:(c + 1) * step]).astype(BF16)
    dt_ref[...] = _dot(h, wdt_ref[...])


def _inproj(x, nw, w, wdt, tm=1024):
    t = x.shape[0]
    return pl.pallas_call(
        _inproj_kernel,
        grid=(t // tm,),
        in_specs=[
            pl.BlockSpec((tm, D_MODEL), lambda m: (m, 0)),
            pl.BlockSpec((1, D_MODEL), lambda m: (0, 0)),
            pl.BlockSpec((D_MODEL, D_PROJ), lambda m: (0, 0)),
            pl.BlockSpec((D_MODEL, LANES), lambda m: (0, 0)),
        ],
        out_specs=[
            pl.BlockSpec((tm, D_PROJ), lambda m: (m, 0)),
            pl.BlockSpec((tm, LANES), lambda m: (m, 0)),
        ],
        out_shape=[jax.ShapeDtypeStruct((t, D_PROJ), BF16), jax.ShapeDtypeStruct((t, LANES), F32)],
        compiler_params=_cparams(("parallel",)),
        name="inproj",
    )(x, nw, w, wdt)


def _retention_kernel(q_ref, k_ref, v_ref, g_ref, cos_ref, sin_ref, intra_ref, qdec_ref, kdec_ref,
                      cd_ref, avg_ref, nw_ref, o_ref, state_ref):
    c = pl.program_id(1)

    @pl.when(c == 0)
    def _():
        state_ref[...] = jnp.zeros_like(state_ref)

    cos = cos_ref[...]
    sin = sin_ref[...]

    def rot(t):
        t1, t2 = t[:, :LANES], t[:, LANES:]
        return jnp.concatenate([t1 * cos - t2 * sin, t1 * sin + t2 * cos], axis=-1)

    lane = lax.broadcasted_iota(jnp.int32, (1, RET_DIM), 1)
    head_qk = (lane % LANES) // (HEAD_DIM // 2)
    cd = cd_ref[...]
    avg = avg_ref[...]

    for b in range(q_ref.shape[0]):
        rq = rot(q_ref[b].astype(F32))
        rk = rot(k_ref[b].astype(F32))
        vb = v_ref[b]

        q_stack = jnp.concatenate([jnp.where(head_qk == h, rq, 0.0) for h in range(RET_HEADS)], axis=0)
        scores = _dot_nt(q_stack.astype(BF16), rk.astype(BF16)) * intra_ref[...]
        y = _pair_heads_matmul(scores.astype(BF16), vb, RET_HEADS)

        state = state_ref[b]
        y = y + _dot((rq * qdec_ref[...]).astype(BF16), state.astype(BF16))
        kd = (rk * kdec_ref[...]).astype(BF16)
        kv = lax.dot_general(kd, vb, (((0,), (0,)), ((), ())), preferred_element_type=F32)
        state_ref[b] = state * cd + jnp.where(cd != 0.0, kv, 0.0)

        mu = _dot_x_const(y, avg)
        d = y - mu
        var = _dot_x_const(d * d, avg)
        yn = d * lax.rsqrt(var + EPS)
        o_ref[b] = (yn * nw_ref[...] * _silu(g_ref[b].astype(F32))).astype(BF16)


def _retention_tables(seq):
    f = np.float32
    half = HEAD_DIM // 2
    inv = (1.0 / (f(ROPE_BASE) ** (np.arange(half, dtype=f) / f(half)))).astype(f)
    ang = np.arange(seq, dtype=f)[:, None] * inv[None, :]
    cos = np.tile(np.cos(ang).astype(f), (1, RET_HEADS))
    sin = np.tile(np.sin(ang).astype(f), (1, RET_HEADS))
    log_gamma = np.log(f(1.0) - f(2.0) ** (f(-5.0) - np.arange(RET_HEADS, dtype=f))).astype(f)
    idx = np.arange(CHUNK, dtype=f)
    diff = idx[:, None] - idx[None, :]
    scale = f(HEAD_DIM ** -0.5)
    intra = np.where(diff >= 0, np.exp(log_gamma[:, None, None] * np.maximum(diff, 0.0)), 0.0).astype(f)
    intra = (intra * scale).reshape(RET_HEADS * CHUNK, CHUNK)
    head_qk = (np.arange(RET_DIM) % LANES) // half
    head_v = np.arange(RET_DIM) // HEAD_DIM
    qdec = np.exp(log_gamma[head_qk][None, :] * (idx[:, None] + 1.0)).astype(f)
    kdec = (np.exp(log_gamma[head_qk][None, :] * (CHUNK - 1.0 - idx[:, None])) * scale).astype(f)
    chunk_decay = np.exp(log_gamma * f(CHUNK)).astype(f)
    same = head_qk[:, None] == head_v[None, :]
    cd = np.where(same, chunk_decay[head_qk][:, None], 0.0).astype(f)
    avg = np.where(head_v[:, None] == head_v[None, :], 1.0 / HEAD_DIM, 0.0).astype(f)
    return (jnp.asarray(cos), jnp.asarray(sin), jnp.asarray(intra), jnp.asarray(qdec), jnp.asarray(kdec),
            jnp.asarray(cd), jnp.asarray(avg, dtype=BF16))


MIXER_GROUP = 4


def _retention(proj, tables, nw, bsz, seq):
    nc = seq // CHUNK
    cos, sin, intra, qdec, kdec, cd, avg = tables
    w = RET_DIM
    grp = MIXER_GROUP if bsz % MIXER_GROUP == 0 else 1
    proj3 = proj.reshape(bsz, seq, D_PROJ)

    def col(j):
        return pl.BlockSpec((grp, CHUNK, w), lambda b, c: (b, c, j))

    def const(shape):
        return pl.BlockSpec(shape, lambda b, c: (0, 0))

    out = pl.pallas_call(
        _retention_kernel,
        grid=(bsz // grp, nc),
        in_specs=[
            col(OFF_RQ // w), col(OFF_RK // w), col(OFF_RV // w), col(OFF_RG // w),
            pl.BlockSpec((CHUNK, LANES), lambda b, c: (c, 0)),
            pl.BlockSpec((CHUNK, LANES), lambda b, c: (c, 0)),
            const((RET_HEADS * CHUNK, CHUNK)), const((CHUNK, w)), const((CHUNK, w)),
            const((w, w)), const((w, w)), const((1, w)),
        ],
        out_specs=pl.BlockSpec((grp, CHUNK, w), lambda b, c: (b, c, 0)),
        out_shape=jax.ShapeDtypeStruct((bsz, seq, w), BF16),
        scratch_shapes=[pltpu.VMEM((grp, w, w), F32)],
        compiler_params=_cparams(("parallel", "arbitrary")),
        name="retention",
    )(proj3, proj3, proj3, proj3, cos, sin, intra, qdec, kdec, cd, avg, nw)
    return out.reshape(bsz * seq, w)


def _ssd_kernel(z_ref, xbc_ref, dt_ref, cw_ref, cb_ref, dtb_ref, alog_ref, dskip_ref, nw_ref,
                tri_ref, exp_ref, o_ref, prev_ref, state_ref):
    c = pl.program_id(1)

    @pl.when(c == 0)
    def _():
        prev_ref[...] = jnp.zeros_like(prev_ref)
        state_ref[...] = jnp.zeros_like(state_ref)

    row = lax.broadcasted_iota(jnp.int32, (SUBLANES, 1), 0)
    tri = tri_ref[...]
    lane_bc = lax.broadcasted_iota(jnp.int32, (1, SSD_BC), 1)
    lower = (lax.broadcasted_iota(jnp.int32, (CHUNK, CHUNK), 0)
             >= lax.broadcasted_iota(jnp.int32, (CHUNK, CHUNK), 1))
    heads_per_group = SSD_HEADS // SSD_GROUPS
    lane_x = lax.broadcasted_iota(jnp.int32, (1, SSD_INNER), 1)
    row_g = lax.broadcasted_iota(jnp.int32, (SSD_BC, 1), 0) // SSD_STATE
    same_group = row_g == lane_x // (SSD_INNER // SSD_GROUPS)
    gw = SSD_INNER // SSD_GROUPS

    for b in range(z_ref.shape[0]):
        cur = xbc_ref[b].astype(F32)
        tail = prev_ref[b]
        acc = cur * cw_ref[SSD_CONV - 1:SSD_CONV, :] + cb_ref[...]
        for j in range(1, SSD_CONV):
            rolled = pltpu.roll(cur, j, 0)
            top = jnp.where(row >= j, rolled[:SUBLANES], pltpu.roll(tail, j, 0))
            shifted = jnp.concatenate([top, rolled[SUBLANES:]], axis=0)
            acc = acc + shifted * cw_ref[SSD_CONV - 1 - j:SSD_CONV - j, :]
        prev_ref[b] = cur[CHUNK - SUBLANES:]
        xa = _silu(acc)
        x = xa[:, :SSD_INNER]
        bm = xa[:, SSD_INNER:SSD_INNER + SSD_BC]
        cm = xa[:, SSD_INNER + SSD_BC:]

        dt = _softplus(dt_ref[b] + dtb_ref[...])
        a = dt * (-jnp.exp(alog_ref[...]))
        a_hi, a_mid, a_lo = _split3(a)
        a_cs = _dot(tri, a_hi) + _dot(tri, a_mid) + _dot(tri, a_lo)
        a_cs_t = a_cs.T
        stack = jnp.concatenate([dt, jnp.exp(a_cs), jnp.exp(a_cs[CHUNK - 1:CHUNK, :] - a_cs)], axis=0)
        stack_x = _dot_x_const(stack, exp_ref[...])
        dt_x = stack_x[:CHUNK]
        ea_x = stack_x[CHUNK:2 * CHUNK]
        dec_x = stack_x[2 * CHUNK:]

        xdt = x * dt_x
        xdt_b = xdt.astype(BF16)
        bm_b = bm.astype(BF16)
        cm_b = cm.astype(BF16)
        cb = [_dot_nt(jnp.where(lane_bc // SSD_STATE == g, cm, 0.0).astype(BF16), bm_b)
              for g in range(SSD_GROUPS)]

        m_list = []
        for h in range(SSD_HEADS):
            seg = a_cs[:, h:h + 1] - a_cs_t[h:h + 1, :]
            l_mat = jnp.exp(jnp.where(lower, seg, -jnp.inf))
            m_list.append((cb[h // heads_per_group] * l_mat).astype(BF16))
        y = _pair_heads_matmul(jnp.concatenate(m_list, axis=0), xdt_b, SSD_HEADS)

        state = state_ref[b]
        y = y + _dot(cm_b, state.astype(BF16)) * ea_x
        contrib = _dot(bm.T.astype(BF16), (xdt * dec_x).astype(BF16))
        state_ref[b] = state * ea_x[CHUNK - 1:CHUNK, :] + jnp.where(same_group, contrib, 0.0)

        y = y + dskip_ref[...] * x
        gy = y * _silu(z_ref[b].astype(F32))
        outs = []
        for g in range(SSD_GROUPS):
            part = gy[:, g * gw:(g + 1) * gw]
            outs.append(part * lax.rsqrt(jnp.mean(part * part, axis=-1, keepdims=True) + EPS))
        o_ref[b] = (jnp.concatenate(outs, axis=-1) * nw_ref[...]).astype(BF16)


def _ssd_tables():
    idx = np.arange(CHUNK)
    tri = (idx[:, None] >= idx[None, :]).astype(np.float32)
    expand = np.zeros((LANES, SSD_INNER), np.float32)
    for h in range(SSD_HEADS):
        expand[h, h * HEAD_DIM:(h + 1) * HEAD_DIM] = 1.0
    return jnp.asarray(tri, dtype=BF16), jnp.asarray(expand, dtype=BF16)


def _ssd(proj, dt_raw, conv_w, conv_b, dt_bias, a_log, d_skip, nw, tables, bsz, seq):
    nc = seq // CHUNK
    tri, expand = tables

    grp = MIXER_GROUP if bsz % MIXER_GROUP == 0 else 1
    proj3 = proj.reshape(bsz, seq, D_PROJ)
    dt3 = dt_raw.reshape(bsz, seq, LANES)

    def const(shape):
        return pl.BlockSpec(shape, lambda b, c: (0, 0))

    out = pl.pallas_call(
        _ssd_kernel,
        grid=(bsz // grp, nc),
        in_specs=[
            pl.BlockSpec((grp, CHUNK, SSD_INNER), lambda b, c: (b, c, OFF_SZ // SSD_INNER)),
            pl.BlockSpec((grp, CHUNK, SSD_CONV_DIM), lambda b, c: (b, c, OFF_SXBC // SSD_CONV_DIM)),
            pl.BlockSpec((grp, CHUNK, LANES), lambda b, c: (b, c, 0)),
            const((SSD_CONV, SSD_CONV_DIM)), const((1, SSD_CONV_DIM)), const((1, LANES)), const((1, LANES)),
            const((1, SSD_INNER)), const((1, SSD_INNER)), const((CHUNK, CHUNK)), const((LANES, SSD_INNER)),
        ],
        out_specs=pl.BlockSpec((grp, CHUNK, SSD_INNER), lambda b, c: (b, c, 0)),
        out_shape=jax.ShapeDtypeStruct((bsz, seq, SSD_INNER), BF16),
        scratch_shapes=[pltpu.VMEM((grp, SUBLANES, SSD_CONV_DIM), F32), pltpu.VMEM((grp, SSD_BC, SSD_INNER), F32)],
        compiler_params=_cparams(("parallel", "arbitrary")),
        name="ssd",
    )(proj3, proj3, dt3, conv_w, conv_b, dt_bias, a_log, d_skip, nw, tri, expand)
    return out.reshape(bsz * seq, SSD_INNER)


SB_BLOCK = 256
SB_GROUP = 4


def _stickbreak_kernel(q_ref, k_ref, v_ref, u_ref, o_ref, acc_ref):
    i = pl.program_id(1)
    blk = SB_BLOCK
    grp = q_ref.shape[0]
    lane = lax.broadcasted_iota(jnp.int32, (1, SB_DIM), 1)
    head = lane // HEAD_DIM
    rows = SB_HEADS * blk
    q_stack = []
    for b in range(grp):
        q = q_ref[b] * jnp.asarray(HEAD_DIM ** -0.5, BF16)
        q_stack.append(jnp.concatenate(
            [jnp.where(head == h, q, jnp.zeros_like(q)) for h in range(SB_HEADS)], axis=0))
    u = u_ref[...]
    strict_lower = (lax.broadcasted_iota(jnp.int32, (rows, blk), 1)
                    < lax.broadcasted_iota(jnp.int32, (rows, blk), 0) % blk)

    def block(j, carries, diagonal):
        start = pl.multiple_of(j * blk, blk)
        new = []
        for b in range(grp):
            kb = k_ref[b, pl.ds(start, blk), :]
            vb = v_ref[b, pl.ds(start, blk), :]
            z = _dot_nt(q_stack[b], kb)
            sp = _softplus(z)
            if diagonal:
                sp = jnp.where(strict_lower, sp, 0.0)
            cs = _dot(sp.astype(BF16), u)
            w = jnp.exp(z - cs - carries[b])
            if diagonal:
                w = jnp.where(strict_lower, w, 0.0)
            wb = w.astype(BF16)
            for pair in range(SB_HEADS // 2):
                pv = _dot(wb[pair * 2 * blk:(pair + 1) * 2 * blk], vb[:, pair * LANES:(pair + 1) * LANES])
                if diagonal:
                    acc_ref[b, pair] = pv
                else:
                    acc_ref[b, pair] += pv
            new.append(carries[b] + jnp.sum(sp, axis=-1, keepdims=True))
        return tuple(new)

    zero = jnp.zeros((rows, 1), F32)
    carries = block(i, (zero,) * grp, True)
    lax.fori_loop(0, i, lambda jj, c: block(i - 1 - jj, c, False), carries)
    first_of_pair = lax.broadcasted_iota(jnp.int32, (1, LANES), 1) < HEAD_DIM
    for b in range(grp):
        out = [jnp.where(first_of_pair, acc_ref[b, p, :blk], acc_ref[b, p, blk:]) for p in range(SB_HEADS // 2)]
        o_ref[b] = jnp.concatenate(out, axis=-1).astype(BF16)


def _stickbreak_tables():
    idx = np.arange(SB_BLOCK)
    rev = (idx[:, None] >= idx[None, :]).astype(np.float32)
    return jnp.asarray(rev, dtype=BF16)


def _stickbreak(proj, u, bsz, seq):
    nq = seq // SB_BLOCK
    w = SB_DIM
    grp = SB_GROUP if bsz % SB_GROUP == 0 else 1
    proj3 = proj.reshape(bsz, seq, D_PROJ)
    out = pl.pallas_call(
        _stickbreak_kernel,
        grid=(bsz // grp, nq),
        in_specs=[
            pl.BlockSpec((grp, SB_BLOCK, w), lambda b, i: (b, i, OFF_BQ // w)),
            pl.BlockSpec((grp, seq, w), lambda b, i: (b, 0, OFF_BK // w)),
            pl.BlockSpec((grp, seq, w), lambda b, i: (b, 0, OFF_BV // w)),
            pl.BlockSpec((SB_BLOCK, SB_BLOCK), lambda b, i: (0, 0)),
        ],
        out_specs=pl.BlockSpec((grp, SB_BLOCK, w), lambda b, i: (b, i, 0)),
        out_shape=jax.ShapeDtypeStruct((bsz, seq, w), BF16),
        scratch_shapes=[pltpu.VMEM((grp, SB_HEADS // 2, 2 * SB_BLOCK, LANES), F32)],
        compiler_params=_cparams(("parallel", "arbitrary")),
        name="stickbreak",
    )(proj3, proj3, proj3, u)
    return out.reshape(bsz * seq, w)


def _outproj_kernel(x_ref, ro_ref, so_ref, bo_ref, w_ref, o_ref, wb_ref):
    @pl.when(pl.program_id(0) == 0)
    def _():
        wb_ref[...] = w_ref[0].astype(BF16)

    acc = _dot(ro_ref[...], wb_ref[:RET_DIM, :])
    acc = acc + _dot(so_ref[...], wb_ref[RET_DIM:RET_DIM + SSD_INNER, :])
    acc = acc + _dot(bo_ref[...], wb_ref[RET_DIM + SSD_INNER:, :])
    o_ref[...] = x_ref[...] + acc


def _outproj(x, ro, so, bo, w, layer, tm=512):
    t = x.shape[0]
    return pl.pallas_call(
        _outproj_kernel,
        grid=(t // tm,),
        in_specs=[
            pl.BlockSpec((tm, D_MODEL), lambda m: (m, 0)),
            pl.BlockSpec((tm, RET_DIM), lambda m: (m, 0)),
            pl.BlockSpec((tm, SSD_INNER), lambda m: (m, 0)),
            pl.BlockSpec((tm, SB_DIM), lambda m: (m, 0)),
            pl.BlockSpec((1, D_MIX, D_MODEL), lambda m: (layer, 0, 0)),
        ],
        out_specs=pl.BlockSpec((tm, D_MODEL), lambda m: (m, 0)),
        out_shape=jax.ShapeDtypeStruct((t, D_MODEL), F32),
        scratch_shapes=[pltpu.VMEM((D_MIX, D_MODEL), BF16)],
        compiler_params=_cparams(("arbitrary",)),
        name="outproj",
    )(x, ro, so, bo, w)


def _ffn_kernel(x_ref, nw_ref, wg_ref, wu_ref, wd_ref, o_ref, h_ref, acc_ref):
    f = pl.program_id(1)

    @pl.when(f == 0)
    def _():
        h_ref[...] = _rms_h(x_ref[...], nw_ref[...]).astype(BF16)
        acc_ref[...] = jnp.zeros_like(acc_ref)

    h = h_ref[...]
    act = _silu(_dot(h, wg_ref[0].astype(BF16))) * _dot(h, wu_ref[0].astype(BF16))
    acc_ref[...] += _dot(act.astype(BF16), wd_ref[0].astype(BF16))

    @pl.when(f == pl.num_programs(1) - 1)
    def _():
        o_ref[...] = x_ref[...] + acc_ref[...]


def _ffn(x, nw, wg, wu, wd, layer, tm=1024, tf=512):
    t = x.shape[0]
    return pl.pallas_call(
        _ffn_kernel,
        grid=(t // tm, D_FF // tf),
        in_specs=[
            pl.BlockSpec((tm, D_MODEL), lambda m, f: (m, 0)),
            pl.BlockSpec((1, D_MODEL), lambda m, f: (0, 0)),
            pl.BlockSpec((1, D_MODEL, tf), lambda m, f: (layer, 0, f)),
            pl.BlockSpec((1, D_MODEL, tf), lambda m, f: (layer, 0, f)),
            pl.BlockSpec((1, tf, D_MODEL), lambda m, f: (layer, f, 0)),
        ],
        out_specs=pl.BlockSpec((tm, D_MODEL), lambda m, f: (m, 0)),
        out_shape=jax.ShapeDtypeStruct((t, D_MODEL), F32),
        scratch_shapes=[pltpu.VMEM((tm, D_MODEL), BF16), pltpu.VMEM((tm, D_MODEL), F32)],
        compiler_params=_cparams(("parallel", "arbitrary")),
        name="ffn",
    )(x, nw, wg, wu, wd)


def _router_kernel(x_ref, nw_ref, rw_ref, idx_ref, gate_ref):
    h = _rms_h(x_ref[...], nw_ref[...])
    h_hi, h_lo = _split2(h)
    w_hi, w_lo = _split2(rw_ref[...])
    logits = _dot(h_hi, w_hi) + _dot(h_hi, w_lo) + _dot(h_lo, w_hi)
    lane = lax.broadcasted_iota(jnp.int32, (1, LANES), 1)
    lg = jnp.where(lane < N_EXPERTS, logits, -jnp.inf)
    m1 = jnp.max(lg, axis=-1, keepdims=True)
    i1 = jnp.min(jnp.where(lg == m1, lane, LANES), axis=-1, keepdims=True)
    lg2 = jnp.where(lane == i1, -jnp.inf, lg)
    m2 = jnp.max(lg2, axis=-1, keepdims=True)
    i2 = jnp.min(jnp.where(lg2 == m2, lane, LANES), axis=-1, keepdims=True)
    e = jnp.exp(m2 - m1)
    g1 = 1.0 / (1.0 + e)
    g2 = e / (1.0 + e)
    idx_ref[...] = jnp.where(lane == 0, i1, jnp.where(lane == 1, i2, 0))
    gate_ref[...] = jnp.where(lane == 0, g1, jnp.where(lane == 1, g2, 0.0))


def _router(x, nw, rw, tm=512):
    t = x.shape[0]
    return pl.pallas_call(
        _router_kernel,
        grid=(t // tm,),
        in_specs=[
            pl.BlockSpec((tm, D_MODEL), lambda m: (m, 0)),
            pl.BlockSpec((1, D_MODEL), lambda m: (0, 0)),
            pl.BlockSpec((D_MODEL, LANES), lambda m: (0, 0)),
        ],
        out_specs=[
            pl.BlockSpec((tm, LANES), lambda m: (m, 0)),
            pl.BlockSpec((tm, LANES), lambda m: (m, 0)),
        ],
        out_shape=[jax.ShapeDtypeStruct((t, LANES), jnp.int32), jax.ShapeDtypeStruct((t, LANES), F32)],
        compiler_params=_cparams(("parallel",)),
        name="router",
    )(x, nw, rw)


MOE_TM = 1024
MOE_TF = 512
MOE_FF_BLOCKS = D_FF // MOE_TF
MOE_SHARE = -(-MOE_TM // MOE_FF_BLOCKS)
MOE_ROWS_BUF = -(-MOE_SHARE * MOE_FF_BLOCKS // SUBLANES) * SUBLANES


def _moe_plan(e2):
    t = e2.shape[0]
    p = 2 * t
    ef = e2.reshape(p)
    onehot = (ef[:, None] == jnp.arange(N_EXPERTS, dtype=jnp.int32)[None, :]).astype(jnp.int32)
    csum = jnp.cumsum(onehot, axis=0)
    counts = csum[-1]
    rank = jnp.sum((csum - onehot) * onehot, axis=1)
    padded = ((counts + MOE_TM - 1) // MOE_TM) * MOE_TM
    ends = jnp.cumsum(padded)
    starts = ends - padded
    dst = jnp.sum(onehot * starts[None, :], axis=1) + rank
    n_tiles = p // MOE_TM + N_EXPERTS
    tile_start = jnp.arange(n_tiles, dtype=jnp.int32) * MOE_TM
    tile_expert = jnp.sum((tile_start[:, None] >= ends[None, :]).astype(jnp.int32), axis=1)
    tile_expert = jnp.minimum(tile_expert, N_EXPERTS - 1)
    n_used = (ends[-1] // MOE_TM).reshape(1).astype(jnp.int32)
    last_row = (starts + counts)[tile_expert]
    tile_valid = jnp.clip(last_row - tile_start, 0, MOE_TM).astype(jnp.int32)
    tile_first = jnp.concatenate([jnp.ones((1,), jnp.int32),
                                  (tile_expert[1:] != tile_expert[:-1]).astype(jnp.int32)])
    src = jnp.zeros((n_tiles * MOE_TM,), jnp.int32).at[dst].set(
        jnp.arange(p, dtype=jnp.int32) // 2, unique_indices=True, mode="promise_in_bounds")
    return dst.reshape(t, 2), src, tile_expert.astype(jnp.int32), tile_valid, tile_first, n_used


MOE_SLOTS = 3


def _expert_kernel(te_ref, nu_ref, tv_ref, first_ref, src_ref, src1_ref, src2_ref, x_hbm, nw_ref,
                   wg_ref, wu_ref, wd_ref, o_ref, rows_ref, sem, h_ref, acc_ref, wg_cache, wu_cache):
    i = pl.program_id(0)
    f = pl.program_id(1)
    nf = pl.num_programs(1)
    n_used = nu_ref[0]
    slot = i % MOE_SLOTS
    share = MOE_SHARE
    issued = MOE_SHARE * MOE_FF_BLOCKS

    def gather_copy(idx_ref, r, s):
        row = idx_ref[0, 0, jnp.minimum(r, MOE_TM - 1)]
        return pltpu.make_async_copy(x_hbm.at[pl.ds(row, 1)], rows_ref.at[s, pl.ds(r, 1)], sem.at[s])

    def wait_slot(s):
        pltpu.make_async_copy(x_hbm.at[pl.ds(0, MOE_TM)], rows_ref.at[s, pl.ds(0, MOE_TM)], sem.at[s]).wait()
        for r in range(MOE_TM, issued):
            pltpu.make_async_copy(x_hbm.at[pl.ds(0, 1)], rows_ref.at[s, pl.ds(r, 1)], sem.at[s]).wait()

    @pl.when(f == 0)
    def _():
        acc_ref[...] = jnp.zeros_like(acc_ref)

    @pl.when((f == 0) & (i == 0))
    def _():
        def body(r, carry):
            gather_copy(src_ref, r, 0).start()
            gather_copy(src1_ref, r, 1).start()
            return carry
        lax.fori_loop(0, issued, body, 0, unroll=8)

    @pl.when((f == 0) & (i < n_used))
    def _():
        wait_slot(slot)
        h_ref[...] = _rms_h(rows_ref[slot, :MOE_TM, :], nw_ref[...]).astype(BF16)

    def step(rows):
        for k in range(share):
            gather_copy(src2_ref, f * share + k, (i + 2) % MOE_SLOTS).start(priority=k % 2)
        h = h_ref[:rows, :]
        act = _silu(_dot(h, wg_cache[f])) * _dot(h, wu_cache[f])
        acc_ref[:rows, :] += _dot(act.astype(BF16), wd_ref[0, 0].astype(BF16))

    @pl.when((i < n_used) & (first_ref[i] == 1))
    def _():
        wg_cache[f] = wg_ref[0, 0].astype(BF16)
        wu_cache[f] = wu_ref[0, 0].astype(BF16)

    half = MOE_TM // 2
    valid = tv_ref[i]

    @pl.when((i < n_used) & (valid > half))
    def _():
        step(MOE_TM)

    @pl.when((i < n_used) & (valid <= half))
    def _():
        step(half)

    @pl.when((i == n_used - 1) & (f == nf - 1))
    def _():
        wait_slot((i + 1) % MOE_SLOTS)
        wait_slot((i + 2) % MOE_SLOTS)

    @pl.when(f == nf - 1)
    def _():
        o_ref[...] = acc_ref[...].astype(BF16)


def _experts(x, nw, src, tile_expert, tile_valid, tile_first, n_used, wg, wu, wd, layer):
    n_tiles = tile_expert.shape[0]
    tf = MOE_TF
    nf = MOE_FF_BLOCKS
    src3 = src.reshape(n_tiles, 1, MOE_TM)

    def fblock(i, f, nu):
        return jnp.where(i < nu[0], f, nf - 1)

    def fblock_first(i, f, nu, first):
        return jnp.where((i < nu[0]) & (first[i] == 1), f, nf - 1)

    def smem_row(ahead):
        return pl.BlockSpec((1, 1, MOE_TM),
                            lambda i, f, te, nu, tv, first: (jnp.minimum(i + ahead, n_tiles - 1), 0, 0),
                            memory_space=pltpu.SMEM)

    def gate_up_spec():
        return pl.BlockSpec((1, 1, D_MODEL, tf),
                            lambda i, f, te, nu, tv, first: (layer, te[i], 0, fblock_first(i, f, nu, first)))

    grid_spec = pltpu.PrefetchScalarGridSpec(
        num_scalar_prefetch=4,
        grid=(n_tiles, nf),
        in_specs=[
            smem_row(0), smem_row(1), smem_row(2),
            pl.BlockSpec(memory_space=pl.ANY),
            pl.BlockSpec((1, D_MODEL), lambda i, f, te, nu, tv, first: (0, 0)),
            gate_up_spec(), gate_up_spec(),
            pl.BlockSpec((1, 1, tf, D_MODEL), lambda i, f, te, nu, tv, first: (layer, te[i], fblock(i, f, nu), 0)),
        ],
        out_specs=pl.BlockSpec((MOE_TM, D_MODEL), lambda i, f, te, nu, tv, first: (i, 0)),
        scratch_shapes=[
            pltpu.VMEM((MOE_SLOTS, MOE_ROWS_BUF, D_MODEL), F32),
            pltpu.SemaphoreType.DMA((MOE_SLOTS,)),
            pltpu.VMEM((MOE_TM, D_MODEL), BF16),
            pltpu.VMEM((MOE_TM, D_MODEL), F32),
            pltpu.VMEM((nf, D_MODEL, tf), BF16),
            pltpu.VMEM((nf, D_MODEL, tf), BF16),
        ],
    )
    return pl.pallas_call(
        _expert_kernel,
        grid_spec=grid_spec,
        out_shape=jax.ShapeDtypeStruct((n_tiles * MOE_TM, D_MODEL), BF16),
        compiler_params=_cparams(("arbitrary", "arbitrary")),
        name="experts",
    )(tile_expert, n_used, tile_valid, tile_first, src3, src3, src3, x, nw, wg, wu, wd)


def _combine_kernel(x_ref, y1_ref, y2_ref, gate_ref, nw_ref, o_ref, *, final_norm):
    g = gate_ref[...]
    out = x_ref[...] + g[:, 0:1] * y1_ref[...].astype(F32) + g[:, 1:2] * y2_ref[...].astype(F32)
    if final_norm:
        out = _rms_h(out, nw_ref[...])
    o_ref[...] = out


def _combine(x, y1, y2, gates, nw, final_norm, tm=1024):
    t = x.shape[0]
    return pl.pallas_call(
        functools.partial(_combine_kernel, final_norm=final_norm),
        grid=(t // tm,),
        in_specs=[
            pl.BlockSpec((tm, D_MODEL), lambda m: (m, 0)),
            pl.BlockSpec((tm, D_MODEL), lambda m: (m, 0)),
            pl.BlockSpec((tm, D_MODEL), lambda m: (m, 0)),
            pl.BlockSpec((tm, LANES), lambda m: (m, 0)),
            pl.BlockSpec((1, D_MODEL), lambda m: (0, 0)),
        ],
        out_specs=pl.BlockSpec((tm, D_MODEL), lambda m: (m, 0)),
        out_shape=jax.ShapeDtypeStruct((t, D_MODEL), F32),
        compiler_params=_cparams(("parallel",)),
        name="combine",
    )(x, y1, y2, gates, nw)


def _moe_layer(xf, norm_w, router_w, wg, wu, wd, layer, final_w=None):
    rw = jnp.pad(router_w.astype(F32), ((0, 0), (0, LANES - N_EXPERTS)))
    idx, gates = _router(xf, _row(norm_w), rw)
    dst, src, tile_expert, tile_valid, tile_first, n_used = _moe_plan(idx[:, :2])
    ys = _experts(xf, _row(norm_w), src, tile_expert, tile_valid, tile_first, n_used, wg, wu, wd, layer)
    y1 = ys.at[dst[:, 0]].get(mode="promise_in_bounds")
    y2 = ys.at[dst[:, 1]].get(mode="promise_in_bounds")
    nw = _row(final_w) if final_w is not None else jnp.ones((1, D_MODEL), F32)
    return _combine(xf, y1, y2, gates, nw, final_w is not None)


def _final_norm_kernel(x_ref, nw_ref, o_ref):
    o_ref[...] = _rms_h(x_ref[...], nw_ref[...])


def _final_norm(x, nw, tm=1024):
    t = x.shape[0]
    return pl.pallas_call(
        _final_norm_kernel,
        grid=(t // tm,),
        in_specs=[pl.BlockSpec((tm, D_MODEL), lambda m: (m, 0)), pl.BlockSpec((1, D_MODEL), lambda m: (0, 0))],
        out_specs=pl.BlockSpec((tm, D_MODEL), lambda m: (m, 0)),
        out_shape=jax.ShapeDtypeStruct((t, D_MODEL), F32),
        compiler_params=_cparams(("parallel",)),
        name="final_norm",
    )(x, nw)


def _halves_first(w):
    rows = w.shape[0]
    w = w.reshape(rows, RET_HEADS, 2, HEAD_DIM // 2)
    return jnp.swapaxes(w, 1, 2).reshape(rows, RET_DIM)


def _layout_w_in(w_in):
    dt0 = 4 * RET_DIM + SSD_INNER + SSD_CONV_DIM
    main = jnp.concatenate([
        _halves_first(w_in[:, :RET_DIM]), _halves_first(w_in[:, RET_DIM:2 * RET_DIM]),
        w_in[:, 2 * RET_DIM:dt0], w_in[:, dt0 + SSD_HEADS:],
    ], axis=1).astype(BF16)
    wdt = jnp.pad(w_in[:, dt0:dt0 + SSD_HEADS], ((0, 0), (0, LANES - SSD_HEADS))).astype(BF16)
    return main, wdt


def _row(v, width=None):
    v = v.reshape(1, -1).astype(F32)
    if width is not None and v.shape[1] < width:
        v = jnp.pad(v, ((0, 0), (0, width - v.shape[1])))
    return v


def kernel(x, norm1_w, w_in, ret_norm_w, ssd_conv_w, ssd_conv_b, ssd_dt_bias, ssd_a_log, ssd_d, ssd_norm_w, w_out, norm2_w, ffn_w_gate, ffn_w_up, ffn_w_down, moe_router, moe_w_gate, moe_w_up, moe_w_down, final_norm_w):
    bsz, seq, _ = x.shape
    depth = w_in.shape[0]
    ret_tables = _retention_tables(seq)
    ssd_tables = _ssd_tables()
    sb_u = _stickbreak_tables()

    xf = x.reshape(bsz * seq, D_MODEL)
    for layer in range(depth):
        w_main, w_dt = _layout_w_in(w_in[layer])
        proj, dt_raw = _inproj(xf, _row(norm1_w[layer]), w_main, w_dt)
        ro = _retention(proj, ret_tables, _row(ret_norm_w[layer]), bsz, seq)
        so = _ssd(proj, dt_raw, ssd_conv_w[layer].astype(F32), _row(ssd_conv_b[layer]),
                  _row(ssd_dt_bias[layer], LANES), _row(ssd_a_log[layer], LANES),
                  _row(jnp.repeat(ssd_d[layer], HEAD_DIM)), _row(ssd_norm_w[layer]), ssd_tables, bsz, seq)
        bo = _stickbreak(proj, sb_u, bsz, seq)
        xf = _outproj(xf, ro, so, bo, w_out, layer)
        j = layer // 2
        if layer % 2 == 0:
            xf = _ffn(xf, _row(norm2_w[layer]), ffn_w_gate, ffn_w_up, ffn_w_down, j)
        else:
            last = layer == depth - 1
            xf = _moe_layer(xf, norm2_w[layer], moe_router[j], moe_w_gate, moe_w_up, moe_w_down, j,
                            final_norm_w if last else None)
    if depth % 2 == 1:
        xf = _final_norm(xf, _row(final_norm_w))
    return xf.reshape(bsz, seq, D_MODEL)
```

```python
import functools
import math

import numpy as np
import jax
import jax.numpy as jnp
from jax import lax
from jax.experimental import pallas as pl
from jax.experimental.pallas import tpu as pltpu

F32 = jnp.float32
BF16 = jnp.bfloat16

D_MODEL = 1024
HEAD_DIM = 64
RET_HEADS = 4
RET_DIM = RET_HEADS * HEAD_DIM
SSD_HEADS = 8
SSD_INNER = SSD_HEADS * HEAD_DIM
SSD_GROUPS = 2
SSD_STATE = 64
SSD_CONV = 4
SSD_BC = SSD_GROUPS * SSD_STATE
SSD_CONV_DIM = SSD_INNER + 2 * SSD_BC
SB_HEADS = 4
SB_DIM = SB_HEADS * HEAD_DIM
D_MIX = RET_DIM + SSD_INNER + SB_DIM
CHUNK = 128
D_FF = 3584
N_EXPERTS = 8
ROPE_BASE = 10000.0
EPS = 1e-6

LANES = 128
SUBLANES = 8
D_PROJ = 3072
OFF_RQ, OFF_RK, OFF_RV, OFF_RG = 0, 256, 512, 768
OFF_SZ, OFF_SXBC = 1024, 1536
OFF_BQ, OFF_BK, OFF_BV = 2304, 2560, 2816

VMEM_LIMIT = 56 * 1024 * 1024


def _cparams(sem):
    return pltpu.CompilerParams(dimension_semantics=sem, vmem_limit_bytes=VMEM_LIMIT)


def _dot(a, b):
    return jnp.dot(a, b, preferred_element_type=F32)


def _dot_nt(a, b):
    return lax.dot_general(a, b, (((1,), (1,)), ((), ())), preferred_element_type=F32)


def _split2(x):
    hi = x.astype(BF16)
    lo = (x - hi.astype(F32)).astype(BF16)
    return hi, lo


def _split3(x):
    hi = x.astype(BF16)
    r = x - hi.astype(F32)
    mid = r.astype(BF16)
    lo = (r - mid.astype(F32)).astype(BF16)
    return hi, mid, lo


def _dot_x_const(x, c):
    hi, lo = _split2(x)
    return _dot(hi, c) + _dot(lo, c)


def _silu(x):
    return x * jax.nn.sigmoid(x)


def _softplus(x):
    e = jnp.exp2(jnp.abs(x) * (-math.log2(math.e)))
    return jnp.maximum(x, 0.0) + jnp.log(1.0 + e)


def _pair_heads_matmul(m_stack, x, n_heads):
    rows = m_stack.shape[0] // n_heads
    first_of_pair = lax.broadcasted_iota(jnp.int32, (1, LANES), 1) < HEAD_DIM
    parts = []
    for p in range(n_heads // 2):
        both = _dot(m_stack[2 * p * rows:(2 * p + 2) * rows], x[:, p * LANES:(p + 1) * LANES])
        parts.append(jnp.where(first_of_pair, both[:rows], both[rows:]))
    return jnp.concatenate(parts, axis=-1)


def _rms_h(x, w):
    ms = jnp.mean(x * x, axis=-1, keepdims=True)
    return x * lax.rsqrt(ms + EPS) * w


def _inproj_kernel(x_ref, nw_ref, w_ref, wdt_ref, o_ref, dt_ref):
    h = _rms_h(x_ref[...], nw_ref[...]).astype(BF16)
    step = 512
    for c in range(D_PROJ // step):
        o_ref[:, c * step:(c + 1) * step] = _dot(h, w_ref[:, c * step:(c + 1) * step]).astype(BF16)
    dt_ref[...] = _dot(h, wdt_ref[...])


def _inproj(x, nw, w, wdt, tm=1024):
    t = x.shape[0]
    return pl.pallas_call(
        _inproj_kernel,
        grid=(t // tm,),
        in_specs=[
            pl.BlockSpec((tm, D_MODEL), lambda m: (m, 0)),
            pl.BlockSpec((1, D_MODEL), lambda m: (0, 0)),
            pl.BlockSpec((D_MODEL, D_PROJ), lambda m: (0, 0)),
            pl.BlockSpec((D_MODEL, LANES), lambda m: (0, 0)),
        ],
        out_specs=[
            pl.BlockSpec((tm, D_PROJ), lambda m: (m, 0)),
            pl.BlockSpec((tm, LANES), lambda m: (m, 0)),
        ],
        out_shape=[jax.ShapeDtypeStruct((t, D_PROJ), BF16), jax.ShapeDtypeStruct((t, LANES), F32)],
        compiler_params=_cparams(("parallel",)),
        name="inproj",
    )(x, nw, w, wdt)


def _retention_kernel(q_ref, k_ref, v_ref, g_ref, cos_ref, sin_ref, intra_ref, qdec_ref, kdec_ref,
                      cd_ref, avg_ref, nw_ref, o_ref, state_ref):
    c = pl.program_id(1)

    @pl.when(c == 0)
    def _():
        state_ref[...] = jnp.zeros_like(state_ref)

    cos = cos_ref[...]
    sin = sin_ref[...]

    def rot(t):
        t1, t2 = t[:, :LANES], t[:, LANES:]
        return jnp.concatenate([t1 * cos - t2 * sin, t1 * sin + t2 * cos], axis=-1)

    lane = lax.broadcasted_iota(jnp.int32, (1, RET_DIM), 1)
    head_qk = (lane % LANES) // (HEAD_DIM // 2)
    cd = cd_ref[...]
    avg = avg_ref[...]

    for b in range(q_ref.shape[0]):
        rq = rot(q_ref[b].astype(F32))
        rk = rot(k_ref[b].astype(F32))
        vb = v_ref[b]

        q_stack = jnp.concatenate([jnp.where(head_qk == h, rq, 0.0) for h in range(RET_HEADS)], axis=0)
        scores = _dot_nt(q_stack.astype(BF16), rk.astype(BF16)) * intra_ref[...]
        y = _pair_heads_matmul(scores.astype(BF16), vb, RET_HEADS)

        state = state_ref[b]
        y = y + _dot((rq * qdec_ref[...]).astype(BF16), state.astype(BF16))
        kd = (rk * kdec_ref[...]).astype(BF16)
        kv = lax.dot_general(kd, vb, (((0,), (0,)), ((), ())), preferred_element_type=F32)
        state_ref[b] = state * cd + jnp.where(cd != 0.0, kv, 0.0)

        mu = _dot_x_const(y, avg)
        d = y - mu
        var = _dot_x_const(d * d, avg)
        yn = d * lax.rsqrt(var + EPS)
        o_ref[b] = (yn * nw_ref[...] * _silu(g_ref[b].astype(F32))).astype(BF16)


def _retention_tables(seq):
    f = np.float32
    half = HEAD_DIM // 2
    inv = (1.0 / (f(ROPE_BASE) ** (np.arange(half, dtype=f) / f(half)))).astype(f)
    ang = np.arange(seq, dtype=f)[:, None] * inv[None, :]
    cos = np.tile(np.cos(ang).astype(f), (1, RET_HEADS))
    sin = np.tile(np.sin(ang).astype(f), (1, RET_HEADS))
    log_gamma = np.log(f(1.0) - f(2.0) ** (f(-5.0) - np.arange(RET_HEADS, dtype=f))).astype(f)
    idx = np.arange(CHUNK, dtype=f)
    diff = idx[:, None] - idx[None, :]
    scale = f(HEAD_DIM ** -0.5)
    intra = np.where(diff >= 0, np.exp(log_gamma[:, None, None] * np.maximum(diff, 0.0)), 0.0).astype(f)
    intra = (intra * scale).reshape(RET_HEADS * CHUNK, CHUNK)
    head_qk = (np.arange(RET_DIM) % LANES) // half
    head_v = np.arange(RET_DIM) // HEAD_DIM
    qdec = np.exp(log_gamma[head_qk][None, :] * (idx[:, None] + 1.0)).astype(f)
    kdec = (np.exp(log_gamma[head_qk][None, :] * (CHUNK - 1.0 - idx[:, None])) * scale).astype(f)
    chunk_decay = np.exp(log_gamma * f(CHUNK)).astype(f)
    same = head_qk[:, None] == head_v[None, :]
    cd = np.where(same, chunk_decay[head_qk][:, None], 0.0).astype(f)
    avg = np.where(head_v[:, None] == head_v[None, :], 1.0 / HEAD_DIM, 0.0).astype(f)
    return (jnp.asarray(cos), jnp.asarray(sin), jnp.asarray(intra), jnp.asarray(qdec), jnp.asarray(kdec),
            jnp.asarray(cd), jnp.asarray(avg, dtype=BF16))


MIXER_GROUP = 8


def _retention(proj, tables, nw, bsz, seq):
    nc = seq // CHUNK
    cos, sin, intra, qdec, kdec, cd, avg = tables
    w = RET_DIM
    grp = MIXER_GROUP if bsz % MIXER_GROUP == 0 else 1
    proj3 = proj.reshape(bsz, seq, D_PROJ)

    def col(j):
        return pl.BlockSpec((grp, CHUNK, w), lambda b, c: (b, c, j))

    def const(shape):
        return pl.BlockSpec(shape, lambda b, c: (0, 0))

    out = pl.pallas_call(
        _retention_kernel,
        grid=(bsz // grp, nc),
        in_specs=[
            col(OFF_RQ // w), col(OFF_RK // w), col(OFF_RV // w), col(OFF_RG // w),
            pl.BlockSpec((CHUNK, LANES), lambda b, c: (c, 0)),
            pl.BlockSpec((CHUNK, LANES), lambda b, c: (c, 0)),
            const((RET_HEADS * CHUNK, CHUNK)), const((CHUNK, w)), const((CHUNK, w)),
            const((w, w)), const((w, w)), const((1, w)),
        ],
        out_specs=pl.BlockSpec((grp, CHUNK, w), lambda b, c: (b, c, 0)),
        out_shape=jax.ShapeDtypeStruct((bsz, seq, w), BF16),
        scratch_shapes=[pltpu.VMEM((grp, w, w), F32)],
        compiler_params=_cparams(("parallel", "arbitrary")),
        name="retention",
    )(proj3, proj3, proj3, proj3, cos, sin, intra, qdec, kdec, cd, avg, nw)
    return out.reshape(bsz * seq, w)


def _ssd_kernel(z_ref, xbc_ref, dt_ref, cw_ref, cb_ref, dtb_ref, alog_ref, dskip_ref, nw_ref,
                tri_ref, exp_ref, o_ref, prev_ref, state_ref):
    c = pl.program_id(1)

    @pl.when(c == 0)
    def _():
        prev_ref[...] = jnp.zeros_like(prev_ref)
        state_ref[...] = jnp.zeros_like(state_ref)

    row = lax.broadcasted_iota(jnp.int32, (SUBLANES, 1), 0)
    tri = tri_ref[...]
    lane_bc = lax.broadcasted_iota(jnp.int32, (1, SSD_BC), 1)
    lower = (lax.broadcasted_iota(jnp.int32, (CHUNK, CHUNK), 0)
             >= lax.broadcasted_iota(jnp.int32, (CHUNK, CHUNK), 1))
    heads_per_group = SSD_HEADS // SSD_GROUPS
    lane_x = lax.broadcasted_iota(jnp.int32, (1, SSD_INNER), 1)
    row_g = lax.broadcasted_iota(jnp.int32, (SSD_BC, 1), 0) // SSD_STATE
    same_group = row_g == lane_x // (SSD_INNER // SSD_GROUPS)
    gw = SSD_INNER // SSD_GROUPS

    for b in range(z_ref.shape[0]):
        cur = xbc_ref[b].astype(F32)
        tail = prev_ref[b]
        acc = cur * cw_ref[SSD_CONV - 1:SSD_CONV, :] + cb_ref[...]
        for j in range(1, SSD_CONV):
            rolled = pltpu.roll(cur, j, 0)
            top = jnp.where(row >= j, rolled[:SUBLANES], pltpu.roll(tail, j, 0))
            shifted = jnp.concatenate([top, rolled[SUBLANES:]], axis=0)
            acc = acc + shifted * cw_ref[SSD_CONV - 1 - j:SSD_CONV - j, :]
        prev_ref[b] = cur[CHUNK - SUBLANES:]
        xa = _silu(acc)
        x = xa[:, :SSD_INNER]
        bm = xa[:, SSD_INNER:SSD_INNER + SSD_BC]
        cm = xa[:, SSD_INNER + SSD_BC:]

        dt = _softplus(dt_ref[b] + dtb_ref[...])
        a = dt * (-jnp.exp(alog_ref[...]))
        a_hi, a_mid, a_lo = _split3(a)
        a_cs = _dot(tri, a_hi) + _dot(tri, a_mid) + _dot(tri, a_lo)
        a_cs_t = a_cs.T
        stack = jnp.concatenate([dt, jnp.exp(a_cs), jnp.exp(a_cs[CHUNK - 1:CHUNK, :] - a_cs)], axis=0)
        stack_x = _dot_x_const(stack, exp_ref[...])
        dt_x = stack_x[:CHUNK]
        ea_x = stack_x[CHUNK:2 * CHUNK]
        dec_x = stack_x[2 * CHUNK:]

        xdt = x * dt_x
        xdt_b = xdt.astype(BF16)
        bm_b = bm.astype(BF16)
        cm_b = cm.astype(BF16)
        cb = [_dot_nt(jnp.where(lane_bc // SSD_STATE == g, cm, 0.0).astype(BF16), bm_b)
              for g in range(SSD_GROUPS)]

        m_list = []
        for h in range(SSD_HEADS):
            seg = a_cs[:, h:h + 1] - a_cs_t[h:h + 1, :]
            l_mat = jnp.exp(jnp.where(lower, seg, -jnp.inf))
            m_list.append((cb[h // heads_per_group] * l_mat).astype(BF16))
        y = _pair_heads_matmul(jnp.concatenate(m_list, axis=0), xdt_b, SSD_HEADS)

        state = state_ref[b]
        y = y + _dot(cm_b, state.astype(BF16)) * ea_x
        contrib = _dot(bm.T.astype(BF16), (xdt * dec_x).astype(BF16))
        state_ref[b] = state * ea_x[CHUNK - 1:CHUNK, :] + jnp.where(same_group, contrib, 0.0)

        y = y + dskip_ref[...] * x
        gy = y * _silu(z_ref[b].astype(F32))
        outs = []
        for g in range(SSD_GROUPS):
            part = gy[:, g * gw:(g + 1) * gw]
            outs.append(part * lax.rsqrt(jnp.mean(part * part, axis=-1, keepdims=True) + EPS))
        o_ref[b] = (jnp.concatenate(outs, axis=-1) * nw_ref[...]).astype(BF16)


def _ssd_tables():
    idx = np.arange(CHUNK)
    tri = (idx[:, None] >= idx[None, :]).astype(np.float32)
    expand = np.zeros((LANES, SSD_INNER), np.float32)
    for h in range(SSD_HEADS):
        expand[h, h * HEAD_DIM:(h + 1) * HEAD_DIM] = 1.0
    return jnp.asarray(tri, dtype=BF16), jnp.asarray(expand, dtype=BF16)


def _ssd(proj, dt_raw, conv_w, conv_b, dt_bias, a_log, d_skip, nw, tables, bsz, seq):
    nc = seq // CHUNK
    tri, expand = tables

    grp = MIXER_GROUP if bsz % MIXER_GROUP == 0 else 1
    proj3 = proj.reshape(bsz, seq, D_PROJ)
    dt3 = dt_raw.reshape(bsz, seq, LANES)

    def const(shape):
        return pl.BlockSpec(shape, lambda b, c: (0, 0))

    out = pl.pallas_call(
        _ssd_kernel,
        grid=(bsz // grp, nc),
        in_specs=[
            pl.BlockSpec((grp, CHUNK, SSD_INNER), lambda b, c: (b, c, OFF_SZ // SSD_INNER)),
            pl.BlockSpec((grp, CHUNK, SSD_CONV_DIM), lambda b, c: (b, c, OFF_SXBC // SSD_CONV_DIM)),
            pl.BlockSpec((grp, CHUNK, LANES), lambda b, c: (b, c, 0)),
            const((SSD_CONV, SSD_CONV_DIM)), const((1, SSD_CONV_DIM)), const((1, LANES)), const((1, LANES)),
            const((1, SSD_INNER)), const((1, SSD_INNER)), const((CHUNK, CHUNK)), const((LANES, SSD_INNER)),
        ],
        out_specs=pl.BlockSpec((grp, CHUNK, SSD_INNER), lambda b, c: (b, c, 0)),
        out_shape=jax.ShapeDtypeStruct((bsz, seq, SSD_INNER), BF16),
        scratch_shapes=[pltpu.VMEM((grp, SUBLANES, SSD_CONV_DIM), F32), pltpu.VMEM((grp, SSD_BC, SSD_INNER), F32)],
        compiler_params=_cparams(("parallel", "arbitrary")),
        name="ssd",
    )(proj3, proj3, dt3, conv_w, conv_b, dt_bias, a_log, d_skip, nw, tri, expand)
    return out.reshape(bsz * seq, SSD_INNER)


SB_BLOCK = 256
SB_GROUP = 4


def _stickbreak_kernel(q_ref, k_ref, v_ref, u_ref, o_ref, acc_ref):
    i = pl.program_id(1)
    blk = SB_BLOCK
    grp = q_ref.shape[0]
    lane = lax.broadcasted_iota(jnp.int32, (1, SB_DIM), 1)
    head = lane // HEAD_DIM
    rows = SB_HEADS * blk
    q_stack = []
    for b in range(grp):
        q = q_ref[b] * jnp.asarray(HEAD_DIM ** -0.5, BF16)
        q_stack.append(jnp.concatenate(
            [jnp.where(head == h, q, jnp.zeros_like(q)) for h in range(SB_HEADS)], axis=0))
    u = u_ref[...]
    strict_lower = (lax.broadcasted_iota(jnp.int32, (rows, blk), 1)
                    < lax.broadcasted_iota(jnp.int32, (rows, blk), 0) % blk)

    def block(j, carries, diagonal):
        start = pl.multiple_of(j * blk, blk)
        new = []
        for b in range(grp):
            kb = k_ref[b, pl.ds(start, blk), :]
            vb = v_ref[b, pl.ds(start, blk), :]
            z = _dot_nt(q_stack[b], kb)
            sp = _softplus(z)
            if diagonal:
                sp = jnp.where(strict_lower, sp, 0.0)
            cs = _dot(sp.astype(BF16), u)
            w = jnp.exp(z - cs - carries[b])
            if diagonal:
                w = jnp.where(strict_lower, w, 0.0)
            wb = w.astype(BF16)
            for pair in range(SB_HEADS // 2):
                pv = _dot(wb[pair * 2 * blk:(pair + 1) * 2 * blk], vb[:, pair * LANES:(pair + 1) * LANES])
                if diagonal:
                    acc_ref[b, pair] = pv
                else:
                    acc_ref[b, pair] += pv
            new.append(carries[b] + jnp.sum(sp, axis=-1, keepdims=True))
        return tuple(new)

    zero = jnp.zeros((rows, 1), F32)
    carries = block(i, (zero,) * grp, True)
    lax.fori_loop(0, i, lambda jj, c: block(i - 1 - jj, c, False), carries)
    first_of_pair = lax.broadcasted_iota(jnp.int32, (1, LANES), 1) < HEAD_DIM
    for b in range(grp):
        out = [jnp.where(first_of_pair, acc_ref[b, p, :blk], acc_ref[b, p, blk:]) for p in range(SB_HEADS // 2)]
        o_ref[b] = jnp.concatenate(out, axis=-1).astype(BF16)


def _stickbreak_tables():
    idx = np.arange(SB_BLOCK)
    rev = (idx[:, None] >= idx[None, :]).astype(np.float32)
    return jnp.asarray(rev, dtype=BF16)


def _stickbreak(proj, u, bsz, seq):
    nq = seq // SB_BLOCK
    w = SB_DIM
    grp = SB_GROUP if bsz % SB_GROUP == 0 else 1
    proj3 = proj.reshape(bsz, seq, D_PROJ)
    out = pl.pallas_call(
        _stickbreak_kernel,
        grid=(bsz // grp, nq),
        in_specs=[
            pl.BlockSpec((grp, SB_BLOCK, w), lambda b, i: (b, i, OFF_BQ // w)),
            pl.BlockSpec((grp, seq, w), lambda b, i: (b, 0, OFF_BK // w)),
            pl.BlockSpec((grp, seq, w), lambda b, i: (b, 0, OFF_BV // w)),
            pl.BlockSpec((SB_BLOCK, SB_BLOCK), lambda b, i: (0, 0)),
        ],
        out_specs=pl.BlockSpec((grp, SB_BLOCK, w), lambda b, i: (b, i, 0)),
        out_shape=jax.ShapeDtypeStruct((bsz, seq, w), BF16),
        scratch_shapes=[pltpu.VMEM((grp, SB_HEADS // 2, 2 * SB_BLOCK, LANES), F32)],
        compiler_params=_cparams(("parallel", "arbitrary")),
        name="stickbreak",
    )(proj3, proj3, proj3, u)
    return out.reshape(bsz * seq, w)


def _outproj_kernel(x_ref, ro_ref, so_ref, bo_ref, w_ref, o_ref, wb_ref):
    @pl.when(pl.program_id(0) == 0)
    def _():
        wb_ref[...] = w_ref[0].astype(BF16)

    acc = _dot(ro_ref[...], wb_ref[:RET_DIM, :])
    acc = acc + _dot(so_ref[...], wb_ref[RET_DIM:RET_DIM + SSD_INNER, :])
    acc = acc + _dot(bo_ref[...], wb_ref[RET_DIM + SSD_INNER:, :])
    o_ref[...] = x_ref[...] + acc


def _outproj(x, ro, so, bo, w, layer, tm=512):
    t = x.shape[0]
    return pl.pallas_call(
        _outproj_kernel,
        grid=(t // tm,),
        in_specs=[
            pl.BlockSpec((tm, D_MODEL), lambda m: (m, 0)),
            pl.BlockSpec((tm, RET_DIM), lambda m: (m, 0)),
            pl.BlockSpec((tm, SSD_INNER), lambda m: (m, 0)),
            pl.BlockSpec((tm, SB_DIM), lambda m: (m, 0)),
            pl.BlockSpec((1, D_MIX, D_MODEL), lambda m: (layer, 0, 0)),
        ],
        out_specs=pl.BlockSpec((tm, D_MODEL), lambda m: (m, 0)),
        out_shape=jax.ShapeDtypeStruct((t, D_MODEL), F32),
        scratch_shapes=[pltpu.VMEM((D_MIX, D_MODEL), BF16)],
        compiler_params=_cparams(("arbitrary",)),
        name="outproj",
    )(x, ro, so, bo, w)


def _ffn_kernel(x_ref, nw_ref, wg_ref, wu_ref, wd_ref, o_ref, h_ref, acc_ref):
    f = pl.program_id(1)

    @pl.when(f == 0)
    def _():
        h_ref[...] = _rms_h(x_ref[...], nw_ref[...]).astype(BF16)
        acc_ref[...] = jnp.zeros_like(acc_ref)

    h = h_ref[...]
    act = _silu(_dot(h, wg_ref[0].astype(BF16))) * _dot(h, wu_ref[0].astype(BF16))
    acc_ref[...] += _dot(act.astype(BF16), wd_ref[0].astype(BF16))

    @pl.when(f == pl.num_programs(1) - 1)
    def _():
        o_ref[...] = x_ref[...] + acc_ref[...]


def _ffn(x, nw, wg, wu, wd, layer, tm=1024, tf=512):
    t = x.shape[0]
    return pl.pallas_call(
        _ffn_kernel,
        grid=(t // tm, D_FF // tf),
        in_specs=[
            pl.BlockSpec((tm, D_MODEL), lambda m, f: (m, 0)),
            pl.BlockSpec((1, D_MODEL), lambda m, f: (0, 0)),
            pl.BlockSpec((1, D_MODEL, tf), lambda m, f: (layer, 0, f)),
            pl.BlockSpec((1, D_MODEL, tf), lambda m, f: (layer, 0, f)),
            pl.BlockSpec((1, tf, D_MODEL), lambda m, f: (layer, f, 0)),
        ],
        out_specs=pl.BlockSpec((tm, D_MODEL), lambda m, f: (m, 0)),
        out_shape=jax.ShapeDtypeStruct((t, D_MODEL), F32),
        scratch_shapes=[pltpu.VMEM((tm, D_MODEL), BF16), pltpu.VMEM((tm, D_MODEL), F32)],
        compiler_params=_cparams(("parallel", "arbitrary")),
        name="ffn",
    )(x, nw, wg, wu, wd)


def _router_kernel(x_ref, nw_ref, rw_ref, idx_ref, gate_ref):
    h = _rms_h(x_ref[...], nw_ref[...])
    h_hi, h_lo = _split2(h)
    w_hi, w_lo = _split2(rw_ref[...])
    logits = _dot(h_hi, w_hi) + _dot(h_hi, w_lo) + _dot(h_lo, w_hi)
    lane = lax.broadcasted_iota(jnp.int32, (1, LANES), 1)
    lg = jnp.where(lane < N_EXPERTS, logits, -jnp.inf)
    m1 = jnp.max(lg, axis=-1, keepdims=True)
    i1 = jnp.min(jnp.where(lg == m1, lane, LANES), axis=-1, keepdims=True)
    lg2 = jnp.where(lane == i1, -jnp.inf, lg)
    m2 = jnp.max(lg2, axis=-1, keepdims=True)
    i2 = jnp.min(jnp.where(lg2 == m2, lane, LANES), axis=-1, keepdims=True)
    e = jnp.exp(m2 - m1)
    g1 = 1.0 / (1.0 + e)
    g2 = e / (1.0 + e)
    idx_ref[...] = jnp.where(lane == 0, i1, jnp.where(lane == 1, i2, 0))
    gate_ref[...] = jnp.where(lane == 0, g1, jnp.where(lane == 1, g2, 0.0))


def _router(x, nw, rw, tm=512):
    t = x.shape[0]
    return pl.pallas_call(
        _router_kernel,
        grid=(t // tm,),
        in_specs=[
            pl.BlockSpec((tm, D_MODEL), lambda m: (m, 0)),
            pl.BlockSpec((1, D_MODEL), lambda m: (0, 0)),
            pl.BlockSpec((D_MODEL, LANES), lambda m: (0, 0)),
        ],
        out_specs=[
            pl.BlockSpec((tm, LANES), lambda m: (m, 0)),
            pl.BlockSpec((tm, LANES), lambda m: (m, 0)),
        ],
        out_shape=[jax.ShapeDtypeStruct((t, LANES), jnp.int32), jax.ShapeDtypeStruct((t, LANES), F32)],
        compiler_params=_cparams(("parallel",)),
        name="router",
    )(x, nw, rw)


MOE_TM = 1024
MOE_TF = 512
MOE_FF_BLOCKS = D_FF // MOE_TF
MOE_SHARE = -(-MOE_TM // MOE_FF_BLOCKS)
MOE_ROWS_BUF = -(-MOE_SHARE * MOE_FF_BLOCKS // SUBLANES) * SUBLANES


def _moe_plan(e2):
    t = e2.shape[0]
    p = 2 * t
    ef = e2.reshape(p)
    onehot = (ef[:, None] == jnp.arange(N_EXPERTS, dtype=jnp.int32)[None, :]).astype(jnp.int32)
    csum = jnp.cumsum(onehot, axis=0)
    counts = csum[-1]
    rank = jnp.sum((csum - onehot) * onehot, axis=1)
    padded = ((counts + MOE_TM - 1) // MOE_TM) * MOE_TM
    ends = jnp.cumsum(padded)
    starts = ends - padded
    dst = jnp.sum(onehot * starts[None, :], axis=1) + rank
    n_tiles = p // MOE_TM + N_EXPERTS
    tile_start = jnp.arange(n_tiles, dtype=jnp.int32) * MOE_TM
    tile_expert = jnp.sum((tile_start[:, None] >= ends[None, :]).astype(jnp.int32), axis=1)
    tile_expert = jnp.minimum(tile_expert, N_EXPERTS - 1)
    n_used = (ends[-1] // MOE_TM).reshape(1).astype(jnp.int32)
    last_row = (starts + counts)[tile_expert]
    tile_valid = jnp.clip(last_row - tile_start, 0, MOE_TM).astype(jnp.int32)
    tile_first = jnp.concatenate([jnp.ones((1,), jnp.int32),
                                  (tile_expert[1:] != tile_expert[:-1]).astype(jnp.int32)])
    src = jnp.zeros((n_tiles * MOE_TM,), jnp.int32).at[dst].set(
        jnp.arange(p, dtype=jnp.int32) // 2, unique_indices=True, mode="promise_in_bounds")
    return dst.reshape(t, 2), src, tile_expert.astype(jnp.int32), tile_valid, tile_first, n_used


MOE_SLOTS = 3


def _expert_kernel(te_ref, nu_ref, tv_ref, first_ref, src_ref, src1_ref, src2_ref, x_hbm, nw_ref,
                   wg_ref, wu_ref, wd_ref, o_ref, rows_ref, sem, h_ref, acc_ref, wg_cache, wu_cache):
    i = pl.program_id(0)
    f = pl.program_id(1)
    nf = pl.num_programs(1)
    n_used = nu_ref[0]
    slot = i % MOE_SLOTS
    share = MOE_SHARE
    issued = MOE_SHARE * MOE_FF_BLOCKS

    def gather_copy(idx_ref, r, s):
        row = idx_ref[0, 0, jnp.minimum(r, MOE_TM - 1)]
        return pltpu.make_async_copy(x_hbm.at[pl.ds(row, 1)], rows_ref.at[s, pl.ds(r, 1)], sem.at[s])

    def wait_slot(s):
        pltpu.make_async_copy(x_hbm.at[pl.ds(0, MOE_TM)], rows_ref.at[s, pl.ds(0, MOE_TM)], sem.at[s]).wait()
        for r in range(MOE_TM, issued):
            pltpu.make_async_copy(x_hbm.at[pl.ds(0, 1)], rows_ref.at[s, pl.ds(r, 1)], sem.at[s]).wait()

    @pl.when(f == 0)
    def _():
        acc_ref[...] = jnp.zeros_like(acc_ref)

    @pl.when((f == 0) & (i == 0))
    def _():
        def body(r, carry):
            gather_copy(src_ref, r, 0).start()
            gather_copy(src1_ref, r, 1).start()
            return carry
        lax.fori_loop(0, issued, body, 0, unroll=8)

    @pl.when((f == 0) & (i < n_used))
    def _():
        wait_slot(slot)
        h_ref[...] = _rms_h(rows_ref[slot, :MOE_TM, :], nw_ref[...]).astype(BF16)

    def step(rows):
        for k in range(share):
            gather_copy(src2_ref, f * share + k, (i + 2) % MOE_SLOTS).start(priority=k % 2)
        h = h_ref[:rows, :]
        act = _silu(_dot(h, wg_cache[f])) * _dot(h, wu_cache[f])
        acc_ref[:rows, :] += _dot(act.astype(BF16), wd_ref[0, 0].astype(BF16))

    @pl.when((i < n_used) & (first_ref[i] == 1))
    def _():
        wg_cache[f] = wg_ref[0, 0].astype(BF16)
        wu_cache[f] = wu_ref[0, 0].astype(BF16)

    half = MOE_TM // 2
    valid = tv_ref[i]

    @pl.when((i < n_used) & (valid > half))
    def _():
        step(MOE_TM)

    @pl.when((i < n_used) & (valid <= half))
    def _():
        step(half)

    @pl.when((i == n_used - 1) & (f == nf - 1))
    def _():
        wait_slot((i + 1) % MOE_SLOTS)
        wait_slot((i + 2) % MOE_SLOTS)

    @pl.when(f == nf - 1)
    def _():
        o_ref[...] = acc_ref[...].astype(BF16)


def _experts(x, nw, src, tile_expert, tile_valid, tile_first, n_used, wg, wu, wd, layer):
    n_tiles = tile_expert.shape[0]
    tf = MOE_TF
    nf = MOE_FF_BLOCKS
    src3 = src.reshape(n_tiles, 1, MOE_TM)

    def fblock(i, f, nu):
        return jnp.where(i < nu[0], f, nf - 1)

    def fblock_first(i, f, nu, first):
        return jnp.where((i < nu[0]) & (first[i] == 1), f, nf - 1)

    def smem_row(ahead):
        return pl.BlockSpec((1, 1, MOE_TM),
                            lambda i, f, te, nu, tv, first: (jnp.minimum(i + ahead, n_tiles - 1), 0, 0),
                            memory_space=pltpu.SMEM)

    def gate_up_spec():
        return pl.BlockSpec((1, 1, D_MODEL, tf),
                            lambda i, f, te, nu, tv, first: (layer, te[i], 0, fblock_first(i, f, nu, first)))

    grid_spec = pltpu.PrefetchScalarGridSpec(
        num_scalar_prefetch=4,
        grid=(n_tiles, nf),
        in_specs=[
            smem_row(0), smem_row(1), smem_row(2),
            pl.BlockSpec(memory_space=pl.ANY),
            pl.BlockSpec((1, D_MODEL), lambda i, f, te, nu, tv, first: (0, 0)),
            gate_up_spec(), gate_up_spec(),
            pl.BlockSpec((1, 1, tf, D_MODEL), lambda i, f, te, nu, tv, first: (layer, te[i], fblock(i, f, nu), 0)),
        ],
        out_specs=pl.BlockSpec((MOE_TM, D_MODEL), lambda i, f, te, nu, tv, first: (i, 0)),
        scratch_shapes=[
            pltpu.VMEM((MOE_SLOTS, MOE_ROWS_BUF, D_MODEL), F32),
            pltpu.SemaphoreType.DMA((MOE_SLOTS,)),
            pltpu.VMEM((MOE_TM, D_MODEL), BF16),
            pltpu.VMEM((MOE_TM, D_MODEL), F32),
            pltpu.VMEM((nf, D_MODEL, tf), BF16),
            pltpu.VMEM((nf, D_MODEL, tf), BF16),
        ],
    )
    return pl.pallas_call(
        _expert_kernel,
        grid_spec=grid_spec,
        out_shape=jax.ShapeDtypeStruct((n_tiles * MOE_TM, D_MODEL), BF16),
        compiler_params=_cparams(("arbitrary", "arbitrary")),
        name="experts",
    )(tile_expert, n_used, tile_valid, tile_first, src3, src3, src3, x, nw, wg, wu, wd)


def _combine_kernel(x_ref, y1_ref, y2_ref, gate_ref, nw_ref, o_ref, *, final_norm):
    g = gate_ref[...]
    out = x_ref[...] + g[:, 0:1] * y1_ref[...].astype(F32) + g[:, 1:2] * y2_ref[...].astype(F32)
    if final_norm:
        out = _rms_h(out, nw_ref[...])
    o_ref[...] = out


def _combine(x, y1, y2, gates, nw, final_norm, tm=1024):
    t = x.shape[0]
    return pl.pallas_call(
        functools.partial(_combine_kernel, final_norm=final_norm),
        grid=(t // tm,),
        in_specs=[
            pl.BlockSpec((tm, D_MODEL), lambda m: (m, 0)),
            pl.BlockSpec((tm, D_MODEL), lambda m: (m, 0)),
            pl.BlockSpec((tm, D_MODEL), lambda m: (m, 0)),
            pl.BlockSpec((tm, LANES), lambda m: (m, 0)),
            pl.BlockSpec((1, D_MODEL), lambda m: (0, 0)),
        ],
        out_specs=pl.BlockSpec((tm, D_MODEL), lambda m: (m, 0)),
        out_shape=jax.ShapeDtypeStruct((t, D_MODEL), F32),
        compiler_params=_cparams(("parallel",)),
        name="combine",
    )(x, y1, y2, gates, nw)


def _moe_layer(xf, norm_w, router_w, wg, wu, wd, layer, final_w=None):
    rw = jnp.pad(router_w.astype(F32), ((0, 0), (0, LANES - N_EXPERTS)))
    idx, gates = _router(xf, _row(norm_w), rw)
    dst, src, tile_expert, tile_valid, tile_first, n_used = _moe_plan(idx[:, :2])
    ys = _experts(xf, _row(norm_w), src, tile_expert, tile_valid, tile_first, n_used, wg, wu, wd, layer)
    y1 = ys.at[dst[:, 0]].get(mode="promise_in_bounds")
    y2 = ys.at[dst[:, 1]].get(mode="promise_in_bounds")
    nw = _row(final_w) if final_w is not None else jnp.ones((1, D_MODEL), F32)
    return _combine(xf, y1, y2, gates, nw, final_w is not None)


def _final_norm_kernel(x_ref, nw_ref, o_ref):
    o_ref[...] = _rms_h(x_ref[...], nw_ref[...])


def _final_norm(x, nw, tm=1024):
    t = x.shape[0]
    return pl.pallas_call(
        _final_norm_kernel,
        grid=(t // tm,),
        in_specs=[pl.BlockSpec((tm, D_MODEL), lambda m: (m, 0)), pl.BlockSpec((1, D_MODEL), lambda m: (0, 0))],
        out_specs=pl.BlockSpec((tm, D_MODEL), lambda m: (m, 0)),
        out_shape=jax.ShapeDtypeStruct((t, D_MODEL), F32),
        compiler_params=_cparams(("parallel",)),
        name="final_norm",
    )(x, nw)


def _halves_first(w):
    rows = w.shape[0]
    w = w.reshape(rows, RET_HEADS, 2, HEAD_DIM // 2)
    return jnp.swapaxes(w, 1, 2).reshape(rows, RET_DIM)


def _layout_w_in(w_in):
    dt0 = 4 * RET_DIM + SSD_INNER + SSD_CONV_DIM
    main = jnp.concatenate([
        _halves_first(w_in[:, :RET_DIM]), _halves_first(w_in[:, RET_DIM:2 * RET_DIM]),
        w_in[:, 2 * RET_DIM:dt0], w_in[:, dt0 + SSD_HEADS:],
    ], axis=1).astype(BF16)
    wdt = jnp.pad(w_in[:, dt0:dt0 + SSD_HEADS], ((0, 0), (0, LANES - SSD_HEADS))).astype(BF16)
    return main, wdt


def _row(v, width=None):
    v = v.reshape(1, -1).astype(F32)
    if width is not None and v.shape[1] < width:
        v = jnp.pad(v, ((0, 0), (0, width - v.shape[1])))
    return v


def kernel(x, norm1_w, w_in, ret_norm_w, ssd_conv_w, ssd_conv_b, ssd_dt_bias, ssd_a_log, ssd_d, ssd_norm_w, w_out, norm2_w, ffn_w_gate, ffn_w_up, ffn_w_down, moe_router, moe_w_gate, moe_w_up, moe_w_down, final_norm_w):
    bsz, seq, _ = x.shape
    depth = w_in.shape[0]
    ret_tables = _retention_tables(seq)
    ssd_tables = _ssd_tables()
    sb_u = _stickbreak_tables()

    xf = x.reshape(bsz * seq, D_MODEL)
    for layer in range(depth):
        w_main, w_dt = _layout_w_in(w_in[layer])
        proj, dt_raw = _inproj(xf, _row(norm1_w[layer]), w_main, w_dt)
        ro = _retention(proj, ret_tables, _row(ret_norm_w[layer]), bsz, seq)
        so = _ssd(proj, dt_raw, ssd_conv_w[layer].astype(F32), _row(ssd_conv_b[layer]),
                  _row(ssd_dt_bias[layer], LANES), _row(ssd_a_log[layer], LANES),
                  _row(jnp.repeat(ssd_d[layer], HEAD_DIM)), _row(ssd_norm_w[layer]), ssd_tables, bsz, seq)
        bo = _stickbreak(proj, sb_u, bsz, seq)
        xf = _outproj(xf, ro, so, bo, w_out, layer)
        j = layer // 2
        if layer % 2 == 0:
            xf = _ffn(xf, _row(norm2_w[layer]), ffn_w_gate, ffn_w_up, ffn_w_down, j)
        else:
            last = layer == depth - 1
            xf = _moe_layer(xf, norm2_w[layer], moe_router[j], moe_w_gate, moe_w_up, moe_w_down, j,
                            final_norm_w if last else None)
    if depth % 2 == 1:
        xf = _final_norm(xf, _row(final_norm_w))
    return xf.reshape(bsz, seq, D_MODEL)
```

```python
import functools
import math

import numpy as np
import jax
import jax.numpy as jnp
from jax import lax
from jax.experimental import pallas as pl
from jax.experimental.pallas import tpu as pltpu

F32 = jnp.float32
BF16 = jnp.bfloat16

D_MODEL = 1024
HEAD_DIM = 64
RET_HEADS = 4
RET_DIM = RET_HEADS * HEAD_DIM
SSD_HEADS = 8
SSD_INNER = SSD_HEADS * HEAD_DIM
SSD_GROUPS = 2
SSD_STATE = 64
SSD_CONV = 4
SSD_BC = SSD_GROUPS * SSD_STATE
SSD_CONV_DIM = SSD_INNER + 2 * SSD_BC
SB_HEADS = 4
SB_DIM = SB_HEADS * HEAD_DIM
D_MIX = RET_DIM + SSD_INNER + SB_DIM
CHUNK = 128
D_FF = 3584
N_EXPERTS = 8
ROPE_BASE = 10000.0
EPS = 1e-6

LANES = 128
SUBLANES = 8
D_PROJ = 3072
OFF_RQ, OFF_RK, OFF_RV, OFF_RG = 0, 256, 512, 768
OFF_SZ, OFF_SXBC = 1024, 1536
OFF_BQ, OFF_BK, OFF_BV = 2304, 2560, 2816

VMEM_LIMIT = 56 * 1024 * 1024


def _cparams(sem):
    return pltpu.CompilerParams(dimension_semantics=sem, vmem_limit_bytes=VMEM_LIMIT)


def _dot(a, b):
    return jnp.dot(a, b, preferred_element_type=F32)


def _dot_nt(a, b):
    return lax.dot_general(a, b, (((1,), (1,)), ((), ())), preferred_element_type=F32)


def _split2(x):
    hi = x.astype(BF16)
    lo = (x - hi.astype(F32)).astype(BF16)
    return hi, lo


def _split3(x):
    hi = x.astype(BF16)
    r = x - hi.astype(F32)
    mid = r.astype(BF16)
    lo = (r - mid.astype(F32)).astype(BF16)
    return hi, mid, lo


def _dot_x_const(x, c):
    hi, lo = _split2(x)
    return _dot(hi, c) + _dot(lo, c)


def _silu(x):
    return x * jax.nn.sigmoid(x)


def _softplus(x):
    e = jnp.exp2(jnp.abs(x) * (-math.log2(math.e)))
    return jnp.maximum(x, 0.0) + jnp.log(1.0 + e)


def _pair_heads_matmul(m_stack, x, n_heads):
    rows = m_stack.shape[0] // n_heads
    first_of_pair = lax.broadcasted_iota(jnp.int32, (1, LANES), 1) < HEAD_DIM
    parts = []
    for p in range(n_heads // 2):
        both = _dot(m_stack[2 * p * rows:(2 * p + 2) * rows], x[:, p * LANES:(p + 1) * LANES])
        parts.append(jnp.where(first_of_pair, both[:rows], both[rows:]))
    return jnp.concatenate(parts, axis=-1)


def _rms_h(x, w):
    ms = jnp.mean(x * x, axis=-1, keepdims=True)
    return x * lax.rsqrt(ms + EPS) * w


def _inproj_kernel(x_ref, nw_ref, w_ref, wdt_ref, o_ref, dt_ref):
    h = _rms_h(x_ref[...], nw_ref[...]).astype(BF16)
    step = 512
    for c in range(D_PROJ // step):
        o_ref[:, c * step:(c + 1) * step] = _dot(h, w_ref[:, c * step:(c + 1) * step]).astype(BF16)
    dt_ref[...] = _dot(h, wdt_ref[...])


def _inproj(x, nw, w, wdt, tm=1024):
    t = x.shape[0]
    return pl.pallas_call(
        _inproj_kernel,
        grid=(t // tm,),
        in_specs=[
            pl.BlockSpec((tm, D_MODEL), lambda m: (m, 0)),
            pl.BlockSpec((1, D_MODEL), lambda m: (0, 0)),
            pl.BlockSpec((D_MODEL, D_PROJ), lambda m: (0, 0)),
            pl.BlockSpec((D_MODEL, LANES), lambda m: (0, 0)),
        ],
        out_specs=[
            pl.BlockSpec((tm, D_PROJ), lambda m: (m, 0)),
            pl.BlockSpec((tm, LANES), lambda m: (m, 0)),
        ],
        out_shape=[jax.ShapeDtypeStruct((t, D_PROJ), BF16), jax.ShapeDtypeStruct((t, LANES), F32)],
        compiler_params=_cparams(("parallel",)),
        name="inproj",
    )(x, nw, w, wdt)


def _retention_kernel(q_ref, k_ref, v_ref, g_ref, cos_ref, sin_ref, intra_ref, qdec_ref, kdec_ref,
                      cd_ref, avg_ref, nw_ref, o_ref, state_ref):
    c = pl.program_id(1)

    @pl.when(c == 0)
    def _():
        state_ref[...] = jnp.zeros_like(state_ref)

    cos = cos_ref[...]
    sin = sin_ref[...]

    def rot(t):
        t1, t2 = t[:, :LANES], t[:, LANES:]
        return jnp.concatenate([t1 * cos - t2 * sin, t1 * sin + t2 * cos], axis=-1)

    lane = lax.broadcasted_iota(jnp.int32, (1, RET_DIM), 1)
    head_qk = (lane % LANES) // (HEAD_DIM // 2)
    cd = cd_ref[...]
    avg = avg_ref[...]

    for b in range(q_ref.shape[0]):
        rq = rot(q_ref[b].astype(F32))
        rk = rot(k_ref[b].astype(F32))
        vb = v_ref[b]

        q_stack = jnp.concatenate([jnp.where(head_qk == h, rq, 0.0) for h in range(RET_HEADS)], axis=0)
        scores = _dot_nt(q_stack.astype(BF16), rk.astype(BF16)) * intra_ref[...]
        y = _pair_heads_matmul(scores.astype(BF16), vb, RET_HEADS)

        state = state_ref[b]
        y = y + _dot((rq * qdec_ref[...]).astype(BF16), state.astype(BF16))
        kd = (rk * kdec_ref[...]).astype(BF16)
        kv = lax.dot_general(kd, vb, (((0,), (0,)), ((), ())), preferred_element_type=F32)
        state_ref[b] = state * cd + jnp.where(cd != 0.0, kv, 0.0)

        mu = _dot_x_const(y, avg)
        d = y - mu
        var = _dot_x_const(d * d, avg)
        yn = d * lax.rsqrt(var + EPS)
        o_ref[b] = (yn * nw_ref[...] * _silu(g_ref[b].astype(F32))).astype(BF16)


def _retention_tables(seq):
    f = np.float32
    half = HEAD_DIM // 2
    inv = (1.0 / (f(ROPE_BASE) ** (np.arange(half, dtype=f) / f(half)))).astype(f)
    ang = np.arange(seq, dtype=f)[:, None] * inv[None, :]
    cos = np.tile(np.cos(ang).astype(f), (1, RET_HEADS))
    sin = np.tile(np.sin(ang).astype(f), (1, RET_HEADS))
    log_gamma = np.log(f(1.0) - f(2.0) ** (f(-5.0) - np.arange(RET_HEADS, dtype=f))).astype(f)
    idx = np.arange(CHUNK, dtype=f)
    diff = idx[:, None] - idx[None, :]
    scale = f(HEAD_DIM ** -0.5)
    intra = np.where(diff >= 0, np.exp(log_gamma[:, None, None] * np.maximum(diff, 0.0)), 0.0).astype(f)
    intra = (intra * scale).reshape(RET_HEADS * CHUNK, CHUNK)
    head_qk = (np.arange(RET_DIM) % LANES) // half
    head_v = np.arange(RET_DIM) // HEAD_DIM
    qdec = np.exp(log_gamma[head_qk][None, :] * (idx[:, None] + 1.0)).astype(f)
    kdec = (np.exp(log_gamma[head_qk][None, :] * (CHUNK - 1.0 - idx[:, None])) * scale).astype(f)
    chunk_decay = np.exp(log_gamma * f(CHUNK)).astype(f)
    same = head_qk[:, None] == head_v[None, :]
    cd = np.where(same, chunk_decay[head_qk][:, None], 0.0).astype(f)
    avg = np.where(head_v[:, None] == head_v[None, :], 1.0 / HEAD_DIM, 0.0).astype(f)
    return (jnp.asarray(cos), jnp.asarray(sin), jnp.asarray(intra), jnp.asarray(qdec), jnp.asarray(kdec),
            jnp.asarray(cd), jnp.asarray(avg, dtype=BF16))


MIXER_GROUP = 8


def _retention(proj, tables, nw, bsz, seq):
    nc = seq // CHUNK
    cos, sin, intra, qdec, kdec, cd, avg = tables
    w = RET_DIM
    grp = MIXER_GROUP if bsz % MIXER_GROUP == 0 else 1
    proj3 = proj.reshape(bsz, seq, D_PROJ)

    def col(j):
        return pl.BlockSpec((grp, CHUNK, w), lambda b, c: (b, c, j))

    def const(shape):
        return pl.BlockSpec(shape, lambda b, c: (0, 0))

    out = pl.pallas_call(
        _retention_kernel,
        grid=(bsz // grp, nc),
        in_specs=[
            col(OFF_RQ // w), col(OFF_RK // w), col(OFF_RV // w), col(OFF_RG // w),
            pl.BlockSpec((CHUNK, LANES), lambda b, c: (c, 0)),
            pl.BlockSpec((CHUNK, LANES), lambda b, c: (c, 0)),
            const((RET_HEADS * CHUNK, CHUNK)), const((CHUNK, w)), const((CHUNK, w)),
            const((w, w)), const((w, w)), const((1, w)),
        ],
        out_specs=pl.BlockSpec((grp, CHUNK, w), lambda b, c: (b, c, 0)),
        out_shape=jax.ShapeDtypeStruct((bsz, seq, w), BF16),
        scratch_shapes=[pltpu.VMEM((grp, w, w), F32)],
        compiler_params=_cparams(("parallel", "arbitrary")),
        name="retention",
    )(proj3, proj3, proj3, proj3, cos, sin, intra, qdec, kdec, cd, avg, nw)
    return out.reshape(bsz * seq, w)


def _ssd_kernel(z_ref, xbc_ref, dt_ref, cw_ref, cb_ref, dtb_ref, alog_ref, dskip_ref, nw_ref,
                tri_ref, exp_ref, o_ref, prev_ref, state_ref):
    c = pl.program_id(1)

    @pl.when(c == 0)
    def _():
        prev_ref[...] = jnp.zeros_like(prev_ref)
        state_ref[...] = jnp.zeros_like(state_ref)

    row = lax.broadcasted_iota(jnp.int32, (SUBLANES, 1), 0)
    tri = tri_ref[...]
    lane_bc = lax.broadcasted_iota(jnp.int32, (1, SSD_BC), 1)
    lower = (lax.broadcasted_iota(jnp.int32, (CHUNK, CHUNK), 0)
             >= lax.broadcasted_iota(jnp.int32, (CHUNK, CHUNK), 1))
    heads_per_group = SSD_HEADS // SSD_GROUPS
    lane_x = lax.broadcasted_iota(jnp.int32, (1, SSD_INNER), 1)
    row_g = lax.broadcasted_iota(jnp.int32, (SSD_BC, 1), 0) // SSD_STATE
    same_group = row_g == lane_x // (SSD_INNER // SSD_GROUPS)
    gw = SSD_INNER // SSD_GROUPS

    for b in range(z_ref.shape[0]):
        cur = xbc_ref[b].astype(F32)
        tail = prev_ref[b]
        acc = cur * cw_ref[SSD_CONV - 1:SSD_CONV, :] + cb_ref[...]
        for j in range(1, SSD_CONV):
            rolled = pltpu.roll(cur, j, 0)
            top = jnp.where(row >= j, rolled[:SUBLANES], pltpu.roll(tail, j, 0))
            shifted = jnp.concatenate([top, rolled[SUBLANES:]], axis=0)
            acc = acc + shifted * cw_ref[SSD_CONV - 1 - j:SSD_CONV - j, :]
        prev_ref[b] = cur[CHUNK - SUBLANES:]
        xa = _silu(acc)
        x = xa[:, :SSD_INNER]
        bm = xa[:, SSD_INNER:SSD_INNER + SSD_BC]
        cm = xa[:, SSD_INNER + SSD_BC:]

        dt = _softplus(dt_ref[b] + dtb_ref[...])
        a = dt * (-jnp.exp(alog_ref[...]))
        a_hi, a_mid, a_lo = _split3(a)
        a_cs = _dot(tri, a_hi) + _dot(tri, a_mid) + _dot(tri, a_lo)
        a_cs_t = a_cs.T
        stack = jnp.concatenate([dt, jnp.exp(a_cs), jnp.exp(a_cs[CHUNK - 1:CHUNK, :] - a_cs)], axis=0)
        stack_x = _dot_x_const(stack, exp_ref[...])
        dt_x = stack_x[:CHUNK]
        ea_x = stack_x[CHUNK:2 * CHUNK]
        dec_x = stack_x[2 * CHUNK:]

        xdt = x * dt_x
        xdt_b = xdt.astype(BF16)
        bm_b = bm.astype(BF16)
        cm_b = cm.astype(BF16)
        cb = [_dot_nt(jnp.where(lane_bc // SSD_STATE == g, cm, 0.0).astype(BF16), bm_b)
              for g in range(SSD_GROUPS)]

        m_list = []
        for h in range(SSD_HEADS):
            seg = a_cs[:, h:h + 1] - a_cs_t[h:h + 1, :]
            l_mat = jnp.exp(jnp.where(lower, seg, -jnp.inf))
            m_list.append((cb[h // heads_per_group] * l_mat).astype(BF16))
        y = _pair_heads_matmul(jnp.concatenate(m_list, axis=0), xdt_b, SSD_HEADS)

        state = state_ref[b]
        y = y + _dot(cm_b, state.astype(BF16)) * ea_x
        contrib = _dot(bm.T.astype(BF16), (xdt * dec_x).astype(BF16))
        state_ref[b] = state * ea_x[CHUNK - 1:CHUNK, :] + jnp.where(same_group, contrib, 0.0)

        y = y + dskip_ref[...] * x
        gy = y * _silu(z_ref[b].astype(F32))
        outs = []
        for g in range(SSD_GROUPS):
            part = gy[:, g * gw:(g + 1) * gw]
            outs.append(part * lax.rsqrt(jnp.mean(part * part, axis=-1, keepdims=True) + EPS))
        o_ref[b] = (jnp.concatenate(outs, axis=-1) * nw_ref[...]).astype(BF16)


def _ssd_tables():
    idx = np.arange(CHUNK)
    tri = (idx[:, None] >= idx[None, :]).astype(np.float32)
    expand = np.zeros((LANES, SSD_INNER), np.float32)
    for h in range(SSD_HEADS):
        expand[h, h * HEAD_DIM:(h + 1) * HEAD_DIM] = 1.0
    return jnp.asarray(tri, dtype=BF16), jnp.asarray(expand, dtype=BF16)


def _ssd(proj, dt_raw, conv_w, conv_b, dt_bias, a_log, d_skip, nw, tables, bsz, seq):
    nc = seq // CHUNK
    tri, expand = tables

    grp = MIXER_GROUP if bsz % MIXER_GROUP == 0 else 1
    proj3 = proj.reshape(bsz, seq, D_PROJ)
    dt3 = dt_raw.reshape(bsz, seq, LANES)

    def const(shape):
        return pl.BlockSpec(shape, lambda b, c: (0, 0))

    out = pl.pallas_call(
        _ssd_kernel,
        grid=(bsz // grp, nc),
        in_specs=[
            pl.BlockSpec((grp, CHUNK, SSD_INNER), lambda b, c: (b, c, OFF_SZ // SSD_INNER)),
            pl.BlockSpec((grp, CHUNK, SSD_CONV_DIM), lambda b, c: (b, c, OFF_SXBC // SSD_CONV_DIM)),
            pl.BlockSpec((grp, CHUNK, LANES), lambda b, c: (b, c, 0)),
            const((SSD_CONV, SSD_CONV_DIM)), const((1, SSD_CONV_DIM)), const((1, LANES)), const((1, LANES)),
            const((1, SSD_INNER)), const((1, SSD_INNER)), const((CHUNK, CHUNK)), const((LANES, SSD_INNER)),
        ],
        out_specs=pl.BlockSpec((grp, CHUNK, SSD_INNER), lambda b, c: (b, c, 0)),
        out_shape=jax.ShapeDtypeStruct((bsz, seq, SSD_INNER), BF16),
        scratch_shapes=[pltpu.VMEM((grp, SUBLANES, SSD_CONV_DIM), F32), pltpu.VMEM((grp, SSD_BC, SSD_INNER), F32)],
        compiler_params=_cparams(("parallel", "arbitrary")),
        name="ssd",
    )(proj3, proj3, dt3, conv_w, conv_b, dt_bias, a_log, d_skip, nw, tri, expand)
    return out.reshape(bsz * seq, SSD_INNER)


SB_BLOCK = 256
SB_GROUP = 8


def _stickbreak_kernel(q_ref, k_ref, v_ref, u_ref, o_ref, acc_ref):
    i = pl.program_id(1)
    blk = SB_BLOCK
    grp = q_ref.shape[0]
    lane = lax.broadcasted_iota(jnp.int32, (1, SB_DIM), 1)
    head = lane // HEAD_DIM
    rows = SB_HEADS * blk
    q_stack = []
    for b in range(grp):
        q = q_ref[b] * jnp.asarray(HEAD_DIM ** -0.5, BF16)
        q_stack.append(jnp.concatenate(
            [jnp.where(head == h, q, jnp.zeros_like(q)) for h in range(SB_HEADS)], axis=0))
    u = u_ref[...]
    strict_lower = (lax.broadcasted_iota(jnp.int32, (rows, blk), 1)
                    < lax.broadcasted_iota(jnp.int32, (rows, blk), 0) % blk)

    def block(j, carries, diagonal):
        start = pl.multiple_of(j * blk, blk)
        new = []
        for b in range(grp):
            kb = k_ref[b, pl.ds(start, blk), :]
            vb = v_ref[b, pl.ds(start, blk), :]
            z = _dot_nt(q_stack[b], kb)
            sp = _softplus(z)
            if diagonal:
                sp = jnp.where(strict_lower, sp, 0.0)
            cs = _dot(sp.astype(BF16), u)
            w = jnp.exp(z - cs - carries[b])
            if diagonal:
                w = jnp.where(strict_lower, w, 0.0)
            wb = w.astype(BF16)
            for pair in range(SB_HEADS // 2):
                pv = _dot(wb[pair * 2 * blk:(pair + 1) * 2 * blk], vb[:, pair * LANES:(pair + 1) * LANES])
                if diagonal:
                    acc_ref[b, pair] = pv
                else:
                    acc_ref[b, pair] += pv
            new.append(carries[b] + jnp.sum(sp, axis=-1, keepdims=True))
        return tuple(new)

    zero = jnp.zeros((rows, 1), F32)
    carries = block(i, (zero,) * grp, True)
    lax.fori_loop(0, i, lambda jj, c: block(i - 1 - jj, c, False), carries)
    first_of_pair = lax.broadcasted_iota(jnp.int32, (1, LANES), 1) < HEAD_DIM
    for b in range(grp):
        out = [jnp.where(first_of_pair, acc_ref[b, p, :blk], acc_ref[b, p, blk:]) for p in range(SB_HEADS // 2)]
        o_ref[b] = jnp.concatenate(out, axis=-1).astype(BF16)


def _stickbreak_tables():
    idx = np.arange(SB_BLOCK)
    rev = (idx[:, None] >= idx[None, :]).astype(np.float32)
    return jnp.asarray(rev, dtype=BF16)


def _stickbreak(proj, u, bsz, seq):
    nq = seq // SB_BLOCK
    w = SB_DIM
    grp = SB_GROUP if bsz % SB_GROUP == 0 else 1
    proj3 = proj.reshape(bsz, seq, D_PROJ)
    out = pl.pallas_call(
        _stickbreak_kernel,
        grid=(bsz // grp, nq),
        in_specs=[
            pl.BlockSpec((grp, SB_BLOCK, w), lambda b, i: (b, i, OFF_BQ // w)),
            pl.BlockSpec((grp, seq, w), lambda b, i: (b, 0, OFF_BK // w)),
            pl.BlockSpec((grp, seq, w), lambda b, i: (b, 0, OFF_BV // w)),
            pl.BlockSpec((SB_BLOCK, SB_BLOCK), lambda b, i: (0, 0)),
        ],
        out_specs=pl.BlockSpec((grp, SB_BLOCK, w), lambda b, i: (b, i, 0)),
        out_shape=jax.ShapeDtypeStruct((bsz, seq, w), BF16),
        scratch_shapes=[pltpu.VMEM((grp, SB_HEADS // 2, 2 * SB_BLOCK, LANES), F32)],
        compiler_params=_cparams(("parallel", "arbitrary")),
        name="stickbreak",
    )(proj3, proj3, proj3, u)
    return out.reshape(bsz * seq, w)


def _outproj_kernel(x_ref, ro_ref, so_ref, bo_ref, w_ref, o_ref, wb_ref):
    @pl.when(pl.program_id(0) == 0)
    def _():
        wb_ref[...] = w_ref[0].astype(BF16)

    acc = _dot(ro_ref[...], wb_ref[:RET_DIM, :])
    acc = acc + _dot(so_ref[...], wb_ref[RET_DIM:RET_DIM + SSD_INNER, :])
    acc = acc + _dot(bo_ref[...], wb_ref[RET_DIM + SSD_INNER:, :])
    o_ref[...] = x_ref[...] + acc


def _outproj(x, ro, so, bo, w, layer, tm=512):
    t = x.shape[0]
    return pl.pallas_call(
        _outproj_kernel,
        grid=(t // tm,),
        in_specs=[
            pl.BlockSpec((tm, D_MODEL), lambda m: (m, 0)),
            pl.BlockSpec((tm, RET_DIM), lambda m: (m, 0)),
            pl.BlockSpec((tm, SSD_INNER), lambda m: (m, 0)),
            pl.BlockSpec((tm, SB_DIM), lambda m: (m, 0)),
            pl.BlockSpec((1, D_MIX, D_MODEL), lambda m: (layer, 0, 0)),
        ],
        out_specs=pl.BlockSpec((tm, D_MODEL), lambda m: (m, 0)),
        out_shape=jax.ShapeDtypeStruct((t, D_MODEL), F32),
        scratch_shapes=[pltpu.VMEM((D_MIX, D_MODEL), BF16)],
        compiler_params=_cparams(("arbitrary",)),
        name="outproj",
    )(x, ro, so, bo, w)


def _ffn_kernel(x_ref, nw_ref, wg_ref, wu_ref, wd_ref, o_ref, h_ref, acc_ref):
    f = pl.program_id(1)

    @pl.when(f == 0)
    def _():
        h_ref[...] = _rms_h(x_ref[...], nw_ref[...]).astype(BF16)
        acc_ref[...] = jnp.zeros_like(acc_ref)

    h = h_ref[...]
    act = _silu(_dot(h, wg_ref[0].astype(BF16))) * _dot(h, wu_ref[0].astype(BF16))
    acc_ref[...] += _dot(act.astype(BF16), wd_ref[0].astype(BF16))

    @pl.when(f == pl.num_programs(1) - 1)
    def _():
        o_ref[...] = x_ref[...] + acc_ref[...]


def _ffn(x, nw, wg, wu, wd, layer, tm=1024, tf=512):
    t = x.shape[0]
    return pl.pallas_call(
        _ffn_kernel,
        grid=(t // tm, D_FF // tf),
        in_specs=[
            pl.BlockSpec((tm, D_MODEL), lambda m, f: (m, 0)),
            pl.BlockSpec((1, D_MODEL), lambda m, f: (0, 0)),
            pl.BlockSpec((1, D_MODEL, tf), lambda m, f: (layer, 0, f)),
            pl.BlockSpec((1, D_MODEL, tf), lambda m, f: (layer, 0, f)),
            pl.BlockSpec((1, tf, D_MODEL), lambda m, f: (layer, f, 0)),
        ],
        out_specs=pl.BlockSpec((tm, D_MODEL), lambda m, f: (m, 0)),
        out_shape=jax.ShapeDtypeStruct((t, D_MODEL), F32),
        scratch_shapes=[pltpu.VMEM((tm, D_MODEL), BF16), pltpu.VMEM((tm, D_MODEL), F32)],
        compiler_params=_cparams(("parallel", "arbitrary")),
        name="ffn",
    )(x, nw, wg, wu, wd)


def _router_kernel(x_ref, nw_ref, rw_ref, idx_ref, gate_ref):
    h = _rms_h(x_ref[...], nw_ref[...])
    h_hi, h_lo = _split2(h)
    w_hi, w_lo = _split2(rw_ref[...])
    logits = _dot(h_hi, w_hi) + _dot(h_hi, w_lo) + _dot(h_lo, w_hi)
    lane = lax.broadcasted_iota(jnp.int32, (1, LANES), 1)
    lg = jnp.where(lane < N_EXPERTS, logits, -jnp.inf)
    m1 = jnp.max(lg, axis=-1, keepdims=True)
    i1 = jnp.min(jnp.where(lg == m1, lane, LANES), axis=-1, keepdims=True)
    lg2 = jnp.where(lane == i1, -jnp.inf, lg)
    m2 = jnp.max(lg2, axis=-1, keepdims=True)
    i2 = jnp.min(jnp.where(lg2 == m2, lane, LANES), axis=-1, keepdims=True)
    e = jnp.exp(m2 - m1)
    g1 = 1.0 / (1.0 + e)
    g2 = e / (1.0 + e)
    idx_ref[...] = jnp.where(lane == 0, i1, jnp.where(lane == 1, i2, 0))
    gate_ref[...] = jnp.where(lane == 0, g1, jnp.where(lane == 1, g2, 0.0))


def _router(x, nw, rw, tm=512):
    t = x.shape[0]
    return pl.pallas_call(
        _router_kernel,
        grid=(t // tm,),
        in_specs=[
            pl.BlockSpec((tm, D_MODEL), lambda m: (m, 0)),
            pl.BlockSpec((1, D_MODEL), lambda m: (0, 0)),
            pl.BlockSpec((D_MODEL, LANES), lambda m: (0, 0)),
        ],
        out_specs=[
            pl.BlockSpec((tm, LANES), lambda m: (m, 0)),
            pl.BlockSpec((tm, LANES), lambda m: (m, 0)),
        ],
        out_shape=[jax.ShapeDtypeStruct((t, LANES), jnp.int32), jax.ShapeDtypeStruct((t, LANES), F32)],
        compiler_params=_cparams(("parallel",)),
        name="router",
    )(x, nw, rw)


MOE_TM = 1024
MOE_TF = 512
MOE_FF_BLOCKS = D_FF // MOE_TF
MOE_SHARE = -(-MOE_TM // MOE_FF_BLOCKS)
MOE_ROWS_BUF = -(-MOE_SHARE * MOE_FF_BLOCKS // SUBLANES) * SUBLANES


def _moe_plan(e2):
    t = e2.shape[0]
    p = 2 * t
    ef = e2.reshape(p)
    onehot = (ef[:, None] == jnp.arange(N_EXPERTS, dtype=jnp.int32)[None, :]).astype(jnp.int32)
    csum = jnp.cumsum(onehot, axis=0)
    counts = csum[-1]
    rank = jnp.sum((csum - onehot) * onehot, axis=1)
    padded = ((counts + MOE_TM - 1) // MOE_TM) * MOE_TM
    ends = jnp.cumsum(padded)
    starts = ends - padded
    dst = jnp.sum(onehot * starts[None, :], axis=1) + rank
    n_tiles = p // MOE_TM + N_EXPERTS
    tile_start = jnp.arange(n_tiles, dtype=jnp.int32) * MOE_TM
    tile_expert = jnp.sum((tile_start[:, None] >= ends[None, :]).astype(jnp.int32), axis=1)
    tile_expert = jnp.minimum(tile_expert, N_EXPERTS - 1)
    n_used = (ends[-1] // MOE_TM).reshape(1).astype(jnp.int32)
    last_row = (starts + counts)[tile_expert]
    tile_valid = jnp.clip(last_row - tile_start, 0, MOE_TM).astype(jnp.int32)
    tile_first = jnp.concatenate([jnp.ones((1,), jnp.int32),
                                  (tile_expert[1:] != tile_expert[:-1]).astype(jnp.int32)])
    src = jnp.zeros((n_tiles * MOE_TM,), jnp.int32).at[dst].set(
        jnp.arange(p, dtype=jnp.int32) // 2, unique_indices=True, mode="promise_in_bounds")
    return dst.reshape(t, 2), src, tile_expert.astype(jnp.int32), tile_valid, tile_first, n_used


MOE_SLOTS = 3


def _expert_kernel(te_ref, nu_ref, tv_ref, first_ref, src_ref, src1_ref, src2_ref, x_hbm, nw_ref,
                   wg_ref, wu_ref, wd_ref, o_ref, rows_ref, sem, h_ref, acc_ref, wg_cache, wu_cache):
    i = pl.program_id(0)
    f = pl.program_id(1)
    nf = pl.num_programs(1)
    n_used = nu_ref[0]
    slot = i % MOE_SLOTS
    share = MOE_SHARE
    issued = MOE_SHARE * MOE_FF_BLOCKS

    def gather_copy(idx_ref, r, s):
        row = idx_ref[0, 0, jnp.minimum(r, MOE_TM - 1)]
        return pltpu.make_async_copy(x_hbm.at[pl.ds(row, 1)], rows_ref.at[s, pl.ds(r, 1)], sem.at[s])

    def wait_slot(s):
        pltpu.make_async_copy(x_hbm.at[pl.ds(0, MOE_TM)], rows_ref.at[s, pl.ds(0, MOE_TM)], sem.at[s]).wait()
        for r in range(MOE_TM, issued):
            pltpu.make_async_copy(x_hbm.at[pl.ds(0, 1)], rows_ref.at[s, pl.ds(r, 1)], sem.at[s]).wait()

    @pl.when(f == 0)
    def _():
        acc_ref[...] = jnp.zeros_like(acc_ref)

    @pl.when((f == 0) & (i == 0))
    def _():
        def body(r, carry):
            gather_copy(src_ref, r, 0).start()
            gather_copy(src1_ref, r, 1).start()
            return carry
        lax.fori_loop(0, issued, body, 0, unroll=8)

    @pl.when((f == 0) & (i < n_used))
    def _():
        wait_slot(slot)
        h_ref[...] = _rms_h(rows_ref[slot, :MOE_TM, :], nw_ref[...]).astype(BF16)

    def step(rows):
        for k in range(share):
            gather_copy(src2_ref, f * share + k, (i + 2) % MOE_SLOTS).start(priority=k % 2)
        h = h_ref[:rows, :]
        act = _silu(_dot(h, wg_cache[f])) * _dot(h, wu_cache[f])
        acc_ref[:rows, :] += _dot(act.astype(BF16), wd_ref[0, 0].astype(BF16))

    @pl.when((i < n_used) & (first_ref[i] == 1))
    def _():
        wg_cache[f] = wg_ref[0, 0].astype(BF16)
        wu_cache[f] = wu_ref[0, 0].astype(BF16)

    half = MOE_TM // 2
    valid = tv_ref[i]

    @pl.when((i < n_used) & (valid > half))
    def _():
        step(MOE_TM)

    @pl.when((i < n_used) & (valid <= half))
    def _():
        step(half)

    @pl.when((i == n_used - 1) & (f == nf - 1))
    def _():
        wait_slot((i + 1) % MOE_SLOTS)
        wait_slot((i + 2) % MOE_SLOTS)

    @pl.when(f == nf - 1)
    def _():
        o_ref[...] = acc_ref[...].astype(BF16)


def _experts(x, nw, src, tile_expert, tile_valid, tile_first, n_used, wg, wu, wd, layer):
    n_tiles = tile_expert.shape[0]
    tf = MOE_TF
    nf = MOE_FF_BLOCKS
    src3 = src.reshape(n_tiles, 1, MOE_TM)

    def fblock(i, f, nu):
        return jnp.where(i < nu[0], f, nf - 1)

    def fblock_first(i, f, nu, first):
        return jnp.where((i < nu[0]) & (first[i] == 1), f, nf - 1)

    def smem_row(ahead):
        return pl.BlockSpec((1, 1, MOE_TM),
                            lambda i, f, te, nu, tv, first: (jnp.minimum(i + ahead, n_tiles - 1), 0, 0),
                            memory_space=pltpu.SMEM)

    def gate_up_spec():
        return pl.BlockSpec((1, 1, D_MODEL, tf),
                            lambda i, f, te, nu, tv, first: (layer, te[i], 0, fblock_first(i, f, nu, first)))

    grid_spec = pltpu.PrefetchScalarGridSpec(
        num_scalar_prefetch=4,
        grid=(n_tiles, nf),
        in_specs=[
            smem_row(0), smem_row(1), smem_row(2),
            pl.BlockSpec(memory_space=pl.ANY),
            pl.BlockSpec((1, D_MODEL), lambda i, f, te, nu, tv, first: (0, 0)),
            gate_up_spec(), gate_up_spec(),
            pl.BlockSpec((1, 1, tf, D_MODEL), lambda i, f, te, nu, tv, first: (layer, te[i], fblock(i, f, nu), 0)),
        ],
        out_specs=pl.BlockSpec((MOE_TM, D_MODEL), lambda i, f, te, nu, tv, first: (i, 0)),
        scratch_shapes=[
            pltpu.VMEM((MOE_SLOTS, MOE_ROWS_BUF, D_MODEL), F32),
            pltpu.SemaphoreType.DMA((MOE_SLOTS,)),
            pltpu.VMEM((MOE_TM, D_MODEL), BF16),
            pltpu.VMEM((MOE_TM, D_MODEL), F32),
            pltpu.VMEM((nf, D_MODEL, tf), BF16),
            pltpu.VMEM((nf, D_MODEL, tf), BF16),
        ],
    )
    return pl.pallas_call(
        _expert_kernel,
        grid_spec=grid_spec,
        out_shape=jax.ShapeDtypeStruct((n_tiles * MOE_TM, D_MODEL), BF16),
        compiler_params=_cparams(("arbitrary", "arbitrary")),
        name="experts",
    )(tile_expert, n_used, tile_valid, tile_first, src3, src3, src3, x, nw, wg, wu, wd)


def _combine_kernel(x_ref, y1_ref, y2_ref, gate_ref, nw_ref, o_ref, *, final_norm):
    g = gate_ref[...]
    out = x_ref[...] + g[:, 0:1] * y1_ref[...].astype(F32) + g[:, 1:2] * y2_ref[...].astype(F32)
    if final_norm:
        out = _rms_h(out, nw_ref[...])
    o_ref[...] = out


def _combine(x, y1, y2, gates, nw, final_norm, tm=1024):
    t = x.shape[0]
    return pl.pallas_call(
        functools.partial(_combine_kernel, final_norm=final_norm),
        grid=(t // tm,),
        in_specs=[
            pl.BlockSpec((tm, D_MODEL), lambda m: (m, 0)),
            pl.BlockSpec((tm, D_MODEL), lambda m: (m, 0)),
            pl.BlockSpec((tm, D_MODEL), lambda m: (m, 0)),
            pl.BlockSpec((tm, LANES), lambda m: (m, 0)),
            pl.BlockSpec((1, D_MODEL), lambda m: (0, 0)),
        ],
        out_specs=pl.BlockSpec((tm, D_MODEL), lambda m: (m, 0)),
        out_shape=jax.ShapeDtypeStruct((t, D_MODEL), F32),
        compiler_params=_cparams(("parallel",)),
        name="combine",
    )(x, y1, y2, gates, nw)


def _moe_layer(xf, norm_w, router_w, wg, wu, wd, layer, final_w=None):
    rw = jnp.pad(router_w.astype(F32), ((0, 0), (0, LANES - N_EXPERTS)))
    idx, gates = _router(xf, _row(norm_w), rw)
    dst, src, tile_expert, tile_valid, tile_first, n_used = _moe_plan(idx[:, :2])
    ys = _experts(xf, _row(norm_w), src, tile_expert, tile_valid, tile_first, n_used, wg, wu, wd, layer)
    y1 = ys.at[dst[:, 0]].get(mode="promise_in_bounds")
    y2 = ys.at[dst[:, 1]].get(mode="promise_in_bounds")
    nw = _row(final_w) if final_w is not None else jnp.ones((1, D_MODEL), F32)
    return _combine(xf, y1, y2, gates, nw, final_w is not None)


def _final_norm_kernel(x_ref, nw_ref, o_ref):
    o_ref[...] = _rms_h(x_ref[...], nw_ref[...])


def _final_norm(x, nw, tm=1024):
    t = x.shape[0]
    return pl.pallas_call(
        _final_norm_kernel,
        grid=(t // tm,),
        in_specs=[pl.BlockSpec((tm, D_MODEL), lambda m: (m, 0)), pl.BlockSpec((1, D_MODEL), lambda m: (0, 0))],
        out_specs=pl.BlockSpec((tm, D_MODEL), lambda m: (m, 0)),
        out_shape=jax.ShapeDtypeStruct((t, D_MODEL), F32),
        compiler_params=_cparams(("parallel",)),
        name="final_norm",
    )(x, nw)


def _halves_first(w):
    rows = w.shape[0]
    w = w.reshape(rows, RET_HEADS, 2, HEAD_DIM // 2)
    return jnp.swapaxes(w, 1, 2).reshape(rows, RET_DIM)


def _layout_w_in(w_in):
    dt0 = 4 * RET_DIM + SSD_INNER + SSD_CONV_DIM
    main = jnp.concatenate([
        _halves_first(w_in[:, :RET_DIM]), _halves_first(w_in[:, RET_DIM:2 * RET_DIM]),
        w_in[:, 2 * RET_DIM:dt0], w_in[:, dt0 + SSD_HEADS:],
    ], axis=1).astype(BF16)
    wdt = jnp.pad(w_in[:, dt0:dt0 + SSD_HEADS], ((0, 0), (0, LANES - SSD_HEADS))).astype(BF16)
    return main, wdt


def _row(v, width=None):
    v = v.reshape(1, -1).astype(F32)
    if width is not None and v.shape[1] < width:
        v = jnp.pad(v, ((0, 0), (0, width - v.shape[1])))
    return v


def kernel(x, norm1_w, w_in, ret_norm_w, ssd_conv_w, ssd_conv_b, ssd_dt_bias, ssd_a_log, ssd_d, ssd_norm_w, w_out, norm2_w, ffn_w_gate, ffn_w_up, ffn_w_down, moe_router, moe_w_gate, moe_w_up, moe_w_down, final_norm_w):
    bsz, seq, _ = x.shape
    depth = w_in.shape[0]
    ret_tables = _retention_tables(seq)
    ssd_tables = _ssd_tables()
    sb_u = _stickbreak_tables()

    xf = x.reshape(bsz * seq, D_MODEL)
    for layer in range(depth):
        w_main, w_dt = _layout_w_in(w_in[layer])
        proj, dt_raw = _inproj(xf, _row(norm1_w[layer]), w_main, w_dt)
        ro = _retention(proj, ret_tables, _row(ret_norm_w[layer]), bsz, seq)
        so = _ssd(proj, dt_raw, ssd_conv_w[layer].astype(F32), _row(ssd_conv_b[layer]),
                  _row(ssd_dt_bias[layer], LANES), _row(ssd_a_log[layer], LANES),
                  _row(jnp.repeat(ssd_d[layer], HEAD_DIM)), _row(ssd_norm_w[layer]), ssd_tables, bsz, seq)
        bo = _stickbreak(proj, sb_u, bsz, seq)
        xf = _outproj(xf, ro, so, bo, w_out, layer)
        j = layer // 2
        if layer % 2 == 0:
            xf = _ffn(xf, _row(norm2_w[layer]), ffn_w_gate, ffn_w_up, ffn_w_down, j)
        else:
            last = layer == depth - 1
            xf = _moe_layer(xf, norm2_w[layer], moe_router[j], moe_w_gate, moe_w_up, moe_w_down, j,
                            final_norm_w if last else None)
    if depth % 2 == 1:
        xf = _final_norm(xf, _row(final_norm_w))
    return xf.reshape(bsz, seq, D_MODEL)
```
